```python
import jax, jax.numpy as jnp
from jax import lax
import numpy as np

D_MODEL = 1024
BATCH = 8
SEQ = 2048
DEPTH = 2
DEC_BATCH = 128
DEC_SEQ = 1
PAST_LEN = 16384
PAGE_SIZE = 128

N_EVEN = (DEPTH + 1) // 2
N_ODD = DEPTH // 2
EPS = 1e-6
MEM_LEN = 256
X_HEADS = 4
X_HEAD_DIM = D_MODEL // X_HEADS
D_A = D_MODEL // 2
CONV_A_W = 31
D_B = D_MODEL // 2
POOL_WINDOWS = (2, 4, 8, 16)
MAX_POOL = 16
POOL_GROUP_DIM = D_B // len(POOL_WINDOWS)
D_C = D_MODEL // 2
C_HEADS = 4
C_HEAD_DIM = D_C // C_HEADS
CHUNK = 128
D_D = D_MODEL // 2
CONV_D_W = 3
D_FF = ((8 * D_MODEL + 3 * 256 - 1) // (3 * 256)) * 256

kernel_name = 'hybrid_conv_pool_gmlp_shortconv_decoder_step'


def _rmsnorm(x, g):
    xf = x.astype(jnp.float32)
    y = xf * lax.rsqrt(jnp.mean(xf * xf, axis=-1, keepdims=True) + EPS)
    return (y * g.astype(jnp.float32)).astype(x.dtype)


def _layernorm(x, g, b):
    xf = x.astype(jnp.float32)
    mu = jnp.mean(xf, axis=-1, keepdims=True)
    var = jnp.mean(jnp.square(xf - mu), axis=-1, keepdims=True)
    y = (xf - mu) * lax.rsqrt(var + EPS) * g.astype(jnp.float32) + b.astype(jnp.float32)
    return y.astype(x.dtype)


def _causal_dwconv(z, hist, w, b):
    k = w.shape[0]
    ext = jnp.concatenate([hist.astype(z.dtype), z], axis=1)
    y = lax.conv_general_dilated(ext, w[:, None, :].astype(z.dtype), window_strides=(1,),
                                 padding='VALID', dimension_numbers=('NWC', 'WIO', 'NWC'),
                                 feature_group_count=z.shape[-1])
    if b is not None:
        y = y + b
    return y, ext[:, z.shape[1]:]


def _multiscale_pool(z, hist, pos0, w_grp, scale):
    n, t, c = z.shape
    p = MAX_POOL - 1
    ext = jnp.concatenate([hist.astype(z.dtype), z], axis=1)
    cs = jnp.cumsum(ext.astype(jnp.float32), axis=1)
    cs = jnp.concatenate([jnp.zeros((n, 1, c), jnp.float32), cs], axis=1)
    pos = (pos0 + jnp.arange(t, dtype=jnp.int32))[None, :, None]
    end = cs[:, p + 1:p + 1 + t]
    zf = z.astype(jnp.float32)
    parts = []
    for g, w in enumerate(POOL_WINDOWS):
        sl = slice(g * POOL_GROUP_DIM, (g + 1) * POOL_GROUP_DIM)
        start = cs[:, p + 1 - w:p + 1 - w + t, sl]
        cnt = jnp.minimum(pos + 1, w).astype(jnp.float32)
        parts.append((end[..., sl] - start) / cnt - zf[..., sl])
    pooled = jnp.stack(parts, axis=2).astype(z.dtype)
    mixed = jnp.einsum('ntgc,gce->ntge', pooled, w_grp).reshape(n, t, c)
    return mixed * scale, ext[:, t:]


def _chunk_spatial_gate(u, v, w_s, b_s):
    n, t, h, dh = v.shape
    n_chunks = -(-t // CHUNK)
    pad = n_chunks * CHUNK - t
    vp = jnp.pad(v, ((0, 0), (0, pad), (0, 0), (0, 0))).reshape(n, n_chunks, CHUNK, h, dh)
    mask = jnp.tril(jnp.ones((CHUNK, CHUNK), dtype=bool))
    wm = jnp.where(mask[None], w_s, jnp.zeros_like(w_s))
    s = jnp.einsum('hij,ncjhd->ncihd', wm, vp) + b_s.T[None, None, :, :, None]
    s = s.reshape(n, n_chunks * CHUNK, h, dh)[:, :t]
    return u * s


def _cross_attn(h, k, v, wq, wo):
    n, t, _ = h.shape
    q = (h @ wq).reshape(n, t, X_HEADS, X_HEAD_DIM)
    s = jnp.einsum('nthd,nmhd->nhtm', q, k).astype(jnp.float32) * (X_HEAD_DIM ** -0.5)
    a = jax.nn.softmax(s, axis=-1).astype(v.dtype)
    o = jnp.einsum('nhtm,nmhd->nthd', a, v).reshape(n, t, D_MODEL)
    return o @ wo


def _swiglu(h, wg, wu, wd):
    return (jax.nn.silu(h @ wg) * (h @ wu)) @ wd


def _trunk(x, mem_k, mem_v, hist_a, hist_b, hist_d, pos0, W):
    n, t, _ = x.shape
    open_start = max(((pos0 + t - 1) // CHUNK) * CHUNK - pos0, 0)
    new_a, new_b, new_c, new_d = [], [], [], []
    for layer in range(DEPTH):
        i = layer // 2
        h = _rmsnorm(x, W['norm_mix'][layer])
        if layer % 2 == 0:
            p = h @ W['w_in_even'][i]
            za = p[..., :D_A] * jax.nn.sigmoid(p[..., D_A:2 * D_A])
            zb = p[..., 2 * D_A:]
            ca, ha = _causal_dwconv(za, hist_a[i], W['conv_a_w'][i], W['conv_a_b'][i])
            ya = jax.nn.silu(_layernorm(ca, W['ln_a_g'][i], W['ln_a_b'][i]))
            yb, hb = _multiscale_pool(zb, hist_b[i], pos0, W['pool_b_w'][i], W['pool_b_scale'][i])
            x = x + jnp.concatenate([ya, yb], axis=-1) @ W['w_out_even'][i]
            new_a.append(ha)
            new_b.append(hb)
        else:
            p = h @ W['w_in_odd'][i]
            u = p[..., :D_C].reshape(n, t, C_HEADS, C_HEAD_DIM)
            v = _layernorm(p[..., D_C:2 * D_C], W['ln_c_g'][i], W['ln_c_b'][i]).reshape(n, t, C_HEADS, C_HEAD_DIM)
            o = 2 * D_C
            gb = p[..., o:o + D_D]
            gc = p[..., o + D_D:o + 2 * D_D]
            hd = p[..., o + 2 * D_D:]
            yc = _chunk_spatial_gate(u, v, W['ws_c'][i], W['bs_c'][i]).reshape(n, t, D_C)
            cd, hdn = _causal_dwconv(gc * hd, hist_d[i], W['conv_d_w'][i], None)
            yd = gb * cd
            x = x + jnp.concatenate([yc, yd], axis=-1) @ W['w_out_odd'][i]
            new_c.append(v[:, open_start:])
            new_d.append(hdn)
        h = _rmsnorm(x, W['norm_x'][layer])
        x = x + _cross_attn(h, mem_k[layer], mem_v[layer], W['wq_x'][layer], W['wo_x'][layer])
        h = _rmsnorm(x, W['norm_ffn'][layer])
        x = x + _swiglu(h, W['w_gate'][layer], W['w_up'][layer], W['w_down'][layer])
    y = _rmsnorm(x, W['norm_final'])
    return y, jnp.stack(new_a), jnp.stack(new_b), jnp.stack(new_c), jnp.stack(new_d)


def _nrm(k, shape, scale):
    return jax.random.normal(k, shape, jnp.float32) * scale


def setup_inputs(seed: int = 0) -> dict:
    key = jax.random.key(seed)
    ks = iter(jax.random.split(key, 40))
    f_in_even = 2 * D_A + D_B
    f_in_odd = 2 * D_C + 3 * D_D
    return {
        'x_prompt': _nrm(next(ks), (BATCH, SEQ, D_MODEL), 1.0),
        'x_sample': _nrm(next(ks), (DEC_BATCH, DEC_SEQ, D_MODEL), 1.0),
        'mem_prompt': _nrm(next(ks), (BATCH, MEM_LEN, D_MODEL), 1.0),
        'state_convA': _nrm(next(ks), (N_EVEN, DEC_BATCH, CONV_A_W - 1, D_A), 0.5),
        'state_poolB': _nrm(next(ks), (N_EVEN, DEC_BATCH, MAX_POOL - 1, D_B), 1.0),
        'state_convD': _nrm(next(ks), (N_ODD, DEC_BATCH, CONV_D_W - 1, D_D), 0.5),
        'cache_mem_k': _nrm(next(ks), (DEPTH, DEC_BATCH, MEM_LEN, X_HEADS, X_HEAD_DIM), 1.0),
        'cache_mem_v': _nrm(next(ks), (DEPTH, DEC_BATCH, MEM_LEN, X_HEADS, X_HEAD_DIM), 1.0),
        'norm_mix': 1.0 + _nrm(next(ks), (DEPTH, D_MODEL), 0.01),
        'norm_x': 1.0 + _nrm(next(ks), (DEPTH, D_MODEL), 0.01),
        'norm_ffn': 1.0 + _nrm(next(ks), (DEPTH, D_MODEL), 0.01),
        'norm_final': 1.0 + _nrm(next(ks), (D_MODEL,), 0.01),
        'w_in_even': _nrm(next(ks), (N_EVEN, D_MODEL, f_in_even), D_MODEL ** -0.5),
        'conv_a_w': _nrm(next(ks), (N_EVEN, CONV_A_W, D_A), CONV_A_W ** -0.5),
        'conv_a_b': _nrm(next(ks), (N_EVEN, D_A), 0.01),
        'ln_a_g': 1.0 + _nrm(next(ks), (N_EVEN, D_A), 0.01),
        'ln_a_b': _nrm(next(ks), (N_EVEN, D_A), 0.01),
        'pool_b_w': _nrm(next(ks), (N_EVEN, len(POOL_WINDOWS), POOL_GROUP_DIM, POOL_GROUP_DIM), POOL_GROUP_DIM ** -0.5),
        'pool_b_scale': 1.0 + _nrm(next(ks), (N_EVEN, D_B), 0.1),
        'w_out_even': _nrm(next(ks), (N_EVEN, D_A + D_B, D_MODEL), (D_A + D_B) ** -0.5),
        'w_in_odd': _nrm(next(ks), (N_ODD, D_MODEL, f_in_odd), D_MODEL ** -0.5),
        'ln_c_g': 1.0 + _nrm(next(ks), (N_ODD, D_C), 0.01),
        'ln_c_b': _nrm(next(ks), (N_ODD, D_C), 0.01),
        'ws_c': _nrm(next(ks), (N_ODD, C_HEADS, CHUNK, CHUNK), CHUNK ** -0.5),
        'bs_c': 1.0 + _nrm(next(ks), (N_ODD, C_HEADS, CHUNK), 0.01),
        'conv_d_w': _nrm(next(ks), (N_ODD, CONV_D_W, D_D), CONV_D_W ** -0.5),
        'w_out_odd': _nrm(next(ks), (N_ODD, D_C + D_D, D_MODEL), (D_C + D_D) ** -0.5),
        'wq_x': _nrm(next(ks), (DEPTH, D_MODEL, D_MODEL), D_MODEL ** -0.5),
        'wk_x': _nrm(next(ks), (DEPTH, D_MODEL, D_MODEL), D_MODEL ** -0.5),
        'wv_x': _nrm(next(ks), (DEPTH, D_MODEL, D_MODEL), D_MODEL ** -0.5),
        'wo_x': _nrm(next(ks), (DEPTH, D_MODEL, D_MODEL), D_MODEL ** -0.5),
        'w_gate': _nrm(next(ks), (DEPTH, D_MODEL, D_FF), D_MODEL ** -0.5),
        'w_up': _nrm(next(ks), (DEPTH, D_MODEL, D_FF), D_MODEL ** -0.5),
        'w_down': _nrm(next(ks), (DEPTH, D_FF, D_MODEL), D_FF ** -0.5),
    }


def reference(x_prompt, x_sample, mem_prompt, state_convA, state_poolB, state_convD,
              cache_mem_k, cache_mem_v, norm_mix, norm_x, norm_ffn, norm_final,
              w_in_even, conv_a_w, conv_a_b, ln_a_g, ln_a_b, pool_b_w, pool_b_scale, w_out_even,
              w_in_odd, ln_c_g, ln_c_b, ws_c, bs_c, conv_d_w, w_out_odd,
              wq_x, wk_x, wv_x, wo_x, w_gate, w_up, w_down):
    W = {'norm_mix': norm_mix, 'norm_x': norm_x, 'norm_ffn': norm_ffn, 'norm_final': norm_final,
         'w_in_even': w_in_even, 'conv_a_w': conv_a_w, 'conv_a_b': conv_a_b,
         'ln_a_g': ln_a_g, 'ln_a_b': ln_a_b, 'pool_b_w': pool_b_w, 'pool_b_scale': pool_b_scale,
         'w_out_even': w_out_even, 'w_in_odd': w_in_odd, 'ln_c_g': ln_c_g, 'ln_c_b': ln_c_b,
         'ws_c': ws_c, 'bs_c': bs_c, 'conv_d_w': conv_d_w, 'w_out_odd': w_out_odd,
         'wq_x': wq_x, 'wo_x': wo_x, 'w_gate': w_gate, 'w_up': w_up, 'w_down': w_down}
    nb = x_prompt.shape[0]
    p_mem_k = jnp.einsum('nmd,lde->lnme', mem_prompt, wk_x).reshape(DEPTH, nb, MEM_LEN, X_HEADS, X_HEAD_DIM)
    p_mem_v = jnp.einsum('nmd,lde->lnme', mem_prompt, wv_x).reshape(DEPTH, nb, MEM_LEN, X_HEADS, X_HEAD_DIM)
    ha0 = jnp.zeros((N_EVEN, nb, CONV_A_W - 1, D_A), x_prompt.dtype)
    hb0 = jnp.zeros((N_EVEN, nb, MAX_POOL - 1, D_B), x_prompt.dtype)
    hd0 = jnp.zeros((N_ODD, nb, CONV_D_W - 1, D_D), x_prompt.dtype)
    y_prompt, p_convA, p_poolB, p_chunkC_v, p_convD = _trunk(
        x_prompt, p_mem_k, p_mem_v, ha0, hb0, hd0, 0, W)
    y_sample, s_convA, s_poolB, s_chunkC_v, s_convD = _trunk(
        x_sample, cache_mem_k, cache_mem_v, state_convA, state_poolB, state_convD, PAST_LEN, W)
    return (y_prompt, y_sample, p_convA, p_poolB, p_chunkC_v, p_convD, p_mem_k, p_mem_v,
            s_convA, s_poolB, s_chunkC_v, s_convD)
```

```python
import functools

import jax
import jax.numpy as jnp
from jax import lax
from jax.experimental import pallas as pl
from jax.experimental.pallas import tpu as pltpu

F32 = jnp.float32
BF16 = jnp.bfloat16

_EPS = 1e-6
_X_HEADS = 4
_C_HEADS = 4
_CHUNK = 128
_POOL_WINDOWS = (2, 4, 8, 16)
_PAST_LEN = 16384

_SUBLANE = 8
_LANE = 128
_VMEM_LIMIT = 56 * 1024 * 1024

_PROMPT_TILE = 512
_FF_CHUNKS = (768, 768, 768, 512)


def _round_up(n, m):
    return -(-n // m) * m


def _params(*sem):
    return pltpu.CompilerParams(dimension_semantics=sem, vmem_limit_bytes=_VMEM_LIMIT)


def _const_spec(shape):
    nd = len(shape)
    return pl.BlockSpec(shape, lambda *_: (0,) * nd, pipeline_mode=pl.Buffered(1))


def _dot(a, b):
    return jnp.dot(a, b, preferred_element_type=F32)


def _rms(x, g):
    y = x * lax.rsqrt(jnp.mean(x * x, axis=-1, keepdims=True) + _EPS)
    return y * g


def _layernorm(x, g, b):
    mu = jnp.mean(x, axis=-1, keepdims=True)
    d = x - mu
    var = jnp.mean(d * d, axis=-1, keepdims=True)
    return d * lax.rsqrt(var + _EPS) * g + b


def _silu(x):
    return x * jax.nn.sigmoid(x)


def _softmax_rows(s):
    m = jnp.max(s, axis=-1, keepdims=True)
    e = jnp.exp(s - m)
    return e * (1.0 / jnp.sum(e, axis=-1, keepdims=True))


def _kv_kernel(m_ref, wk_ref, wv_ref, k_ref, v_ref):
    m = m_ref[...].astype(BF16)
    k_ref[...] = _dot(m, wk_ref[...])
    v_ref[...] = _dot(m, wv_ref[...])


def _kv_proj(mem2d, wk, wv, tm):
    rows, d = mem2d.shape
    depth = wk.shape[0]
    w_spec = pl.BlockSpec((None, d, d), lambda l, m: (l, 0, 0))
    o_spec = pl.BlockSpec((None, tm, d), lambda l, m: (l, m, 0))
    return pl.pallas_call(
        _kv_kernel,
        grid=(depth, rows // tm),
        in_specs=[pl.BlockSpec((tm, d), lambda l, m: (m, 0)), w_spec, w_spec],
        out_specs=[o_spec, o_spec],
        out_shape=[jax.ShapeDtypeStruct((depth, rows, d), F32)] * 2,
        compiler_params=_params("arbitrary", "arbitrary"),
        name="kv_proj",
    )(mem2d, wk, wv)


def _ffn_kernel(x_ref, g_ref, wg_ref, wu_ref, wd_ref, gf_ref, o_ref, *, chunks, final_norm):
    x = x_ref[...]
    h = _rms(x, g_ref[...]).astype(BF16)
    acc = x
    lo = 0
    for width in chunks:
        gate = _dot(h, wg_ref[:, lo:lo + width])
        up = _dot(h, wu_ref[:, lo:lo + width])
        act = (_silu(gate) * up).astype(BF16)
        acc = acc + _dot(act, wd_ref[lo:lo + width, :])
        lo += width
    if final_norm:
        acc = _rms(acc, gf_ref[...])
    o_ref[...] = acc


def _ffn(x2d, g, wg, wu, wd, gf, tm, final_norm):
    rows, d = x2d.shape
    ff = wg.shape[1]
    assert sum(_FF_CHUNKS) == ff
    kern = functools.partial(_ffn_kernel, chunks=_FF_CHUNKS, final_norm=final_norm)
    row_spec = pl.BlockSpec((tm, d), lambda i: (i, 0))
    return pl.pallas_call(
        kern,
        grid=(rows // tm,),
        in_specs=[row_spec, _const_spec((1, d)), _const_spec((d, ff)), _const_spec((d, ff)),
                  _const_spec((ff, d)), _const_spec((1, d))],
        out_specs=row_spec,
        out_shape=jax.ShapeDtypeStruct((rows, d), F32),
        compiler_params=_params("arbitrary"),
        name="ffn",
    )(x2d, g, wg, wu, wd, gf)


def _pattn_kernel(x_ref, g_ref, wq_ref, k_ref, v_ref, wo_ref, o_ref, *, heads):
    x = x_ref[...]
    d = x.shape[-1]
    dh = d // heads
    h = _rms(x, g_ref[...]).astype(BF16)
    q = _dot(h, wq_ref[...]).astype(BF16)
    outs = []
    for hd in range(heads):
        sl = slice(hd * dh, (hd + 1) * dh)
        kh = k_ref[:, sl].astype(BF16)
        s = lax.dot_general(q[:, sl], kh, (((1,), (1,)), ((), ())), preferred_element_type=F32)
        a = _softmax_rows(s * (dh ** -0.5)).astype(BF16)
        outs.append(_dot(a, v_ref[:, sl].astype(BF16)))
    o = jnp.concatenate(outs, axis=1).astype(BF16)
    o_ref[...] = x + _dot(o, wo_ref[...])


def _prompt_attn(x, g, wq, k, v, wo, layer, tm):
    nb, t, d = x.shape
    mem_len = k.shape[2]
    x_spec = pl.BlockSpec((None, tm, d), lambda n, i: (n, i, 0))
    kv_spec = pl.BlockSpec((None, None, mem_len, d), lambda n, i: (layer, n, 0, 0))
    return pl.pallas_call(
        functools.partial(_pattn_kernel, heads=_X_HEADS),
        grid=(nb, t // tm),
        in_specs=[x_spec, _const_spec((1, d)), _const_spec((d, d)), kv_spec, kv_spec, _const_spec((d, d))],
        out_specs=x_spec,
        out_shape=jax.ShapeDtypeStruct((nb, t, d), F32),
        compiler_params=_params("arbitrary", "arbitrary"),
        name="prompt_attn",
    )(x, g, wq, k, v, wo)


def _sattn_kernel(x_ref, g_ref, wq_ref, k_ref, v_ref, wo_ref, o_ref, q_scr, o_scr, *, bb, heads):
    i = pl.program_id(0)
    d = x_ref.shape[-1]
    dh = d // heads

    @pl.when(i == 0)
    def _():
        h = _rms(x_ref[...], g_ref[...]).astype(BF16)
        q_scr[...] = _dot(h, wq_ref[...])

    row0 = pl.multiple_of(i * bb, bb)
    q_blk = q_scr[pl.ds(row0, bb), :].astype(BF16)
    rows = lax.broadcasted_iota(jnp.int32, (bb, 1), 0)
    o_blk = jnp.zeros((bb, d), F32)
    for j in range(bb):
        outs = []
        for hd in range(heads):
            sl = slice(hd * dh, (hd + 1) * dh)
            kh = k_ref[j, :, hd, :].astype(BF16)
            s = lax.dot_general(q_blk[:, sl], kh, (((1,), (1,)), ((), ())), preferred_element_type=F32)
            a = _softmax_rows(s * (dh ** -0.5)).astype(BF16)
            outs.append(_dot(a, v_ref[j, :, hd, :].astype(BF16)))
        o_blk = jnp.where(rows == j, jnp.concatenate(outs, axis=1), o_blk)
    o_scr[pl.ds(row0, bb), :] = o_blk

    @pl.when(i == pl.num_programs(0) - 1)
    def _():
        o_ref[...] = x_ref[...] + _dot(o_scr[...].astype(BF16), wo_ref[...])


def _sample_attn(x2d, g, wq, k5, v5, wo, layer, bb):
    b, d = x2d.shape
    _, _, mem_len, heads, dh = k5.shape
    assert bb == _SUBLANE and b % bb == 0
    kv_spec = pl.BlockSpec((None, bb, mem_len, heads, dh), lambda i: (layer, i, 0, 0, 0))
    return pl.pallas_call(
        functools.partial(_sattn_kernel, bb=bb, heads=heads),
        grid=(b // bb,),
        in_specs=[_const_spec((b, d)), _const_spec((1, d)), _const_spec((d, d)), kv_spec, kv_spec,
                  _const_spec((d, d))],
        out_specs=pl.BlockSpec((b, d), lambda i: (0, 0)),
        out_shape=jax.ShapeDtypeStruct((b, d), F32),
        scratch_shapes=[pltpu.VMEM((b, d), F32), pltpu.VMEM((b, d), F32)],
        compiler_params=_params("arbitrary"),
        name="sample_attn",
    )(x2d, g, wq, k5, v5, wo)


_CONV_ROWS = 64


def _peven_kernel(x_ref, g_ref, win_ref, cw_ref, cb_ref, lg_ref, lb_ref, pw_ref, ps_ref, wout_ref,
                  xo_ref, sa_ref, sb_ref, za_ext, zb_ext, ca_buf, *, tm, d_a, ha, hb):
    t = pl.program_id(1)
    ka = cw_ref.shape[0]
    d_b = zb_ext.shape[1]
    dg = d_b // len(_POOL_WINDOWS)

    @pl.when(t == 0)
    def _():
        za_ext[0:ha, :] = jnp.zeros((ha, d_a), F32)
        zb_ext[0:hb, :] = jnp.zeros((hb, d_b), F32)

    x = x_ref[...]
    h = _rms(x, g_ref[...]).astype(BF16)
    p = _dot(h, win_ref[...])
    za = p[:, :d_a] * jax.nn.sigmoid(p[:, d_a:2 * d_a])
    zb = p[:, 2 * d_a:]
    za_ext[ha:ha + tm, :] = za
    zb_ext[hb:hb + tm, :] = zb

    base = ha - (ka - 1)
    for c in range(d_a // _LANE):
        ls = slice(c * _LANE, (c + 1) * _LANE)
        for r in range(tm // _CONV_ROWS):
            r0 = r * _CONV_ROWS
            acc = jnp.broadcast_to(cb_ref[:, ls], (_CONV_ROWS, _LANE))
            for k in range(ka):
                acc = acc + cw_ref[k:k + 1, ls] * za_ext[base + r0 + k:base + r0 + k + _CONV_ROWS, ls]
            ca_buf[r0:r0 + _CONV_ROWS, ls] = acc
    ya = _silu(_layernorm(ca_buf[...], lg_ref[...], lb_ref[...]))

    pos = t * tm + lax.broadcasted_iota(jnp.int32, (tm, 1), 0)
    parts = []
    for gi, w in enumerate(_POOL_WINDOWS):
        ls = slice(gi * dg, (gi + 1) * dg)
        s = zb[:, ls]
        for j in range(1, w):
            s = s + zb_ext[hb - j:hb - j + tm, ls]
        inv_cnt = 1.0 / jnp.minimum(pos + 1, w).astype(F32)
        pooled = (s * inv_cnt - zb[:, ls]).astype(BF16)
        parts.append(_dot(pooled, pw_ref[gi]))
    yb = jnp.concatenate(parts, axis=1) * ps_ref[...]

    y = jnp.concatenate([ya, yb], axis=1).astype(BF16)
    xo_ref[...] = x + _dot(y, wout_ref[...])

    @pl.when(t == pl.num_programs(1) - 1)
    def _():
        sa_ref[...] = za_ext[ha + tm - (ka - 1):ha + tm, :]
        sb_ref[...] = zb_ext[hb + tm - sb_ref.shape[0]:hb + tm, :]

    za_ext[0:ha, :] = za_ext[tm:tm + ha, :]
    zb_ext[0:hb, :] = zb_ext[tm:tm + hb, :]


def _prompt_even(x, g, win, cw, cb, lg, lb, pw, ps, wout, tm):
    nb, t, d = x.shape
    ka, d_a = cw.shape
    d_b = ps.shape[1]
    pb = max(_POOL_WINDOWS) - 1
    ha = _round_up(ka - 1, _SUBLANE)
    hb = _round_up(pb, _SUBLANE)
    assert t % tm == 0 and tm % _CONV_ROWS == 0 and tm >= max(ha, hb)
    x_spec = pl.BlockSpec((None, tm, d), lambda n, i: (n, i, 0))
    kern = functools.partial(_peven_kernel, tm=tm, d_a=d_a, ha=ha, hb=hb)
    return pl.pallas_call(
        kern,
        grid=(nb, t // tm),
        in_specs=[x_spec, _const_spec((1, d)), _const_spec(win.shape), _const_spec(cw.shape),
                  _const_spec((1, d_a)), _const_spec((1, d_a)), _const_spec((1, d_a)),
                  _const_spec(pw.shape), _const_spec((1, d_b)), _const_spec(wout.shape)],
        out_specs=[x_spec,
                   pl.BlockSpec((None, ka - 1, d_a), lambda n, i: (n, 0, 0)),
                   pl.BlockSpec((None, pb, d_b), lambda n, i: (n, 0, 0))],
        out_shape=[jax.ShapeDtypeStruct((nb, t, d), F32),
                   jax.ShapeDtypeStruct((nb, ka - 1, d_a), F32),
                   jax.ShapeDtypeStruct((nb, pb, d_b), F32)],
        scratch_shapes=[pltpu.VMEM((ha + tm, d_a), F32), pltpu.VMEM((hb + tm, d_b), F32),
                        pltpu.VMEM((tm, d_a), F32)],
        compiler_params=_params("arbitrary", "arbitrary"),
        name="prompt_even",
    )(x, g, win, cw, cb, lg, lb, pw, ps, wout)


def _seven_kernel(x_ref, g_ref, win_ref, cw_ref, cb_ref, lg_ref, lb_ref, pw_ref, ps_ref, wout_ref,
                  ha_ref, hb_ref, xo_ref, sa_ref, sb_ref):
    ka, d_a = cw_ref.shape
    pb = hb_ref.shape[1]
    d_b = hb_ref.shape[2]
    dg = d_b // len(_POOL_WINDOWS)
    x = x_ref[...]
    h = _rms(x, g_ref[...]).astype(BF16)
    p = _dot(h, win_ref[...])
    za = p[:, :d_a] * jax.nn.sigmoid(p[:, d_a:2 * d_a])
    zb = p[:, 2 * d_a:]

    acc = cb_ref[...] + cw_ref[ka - 1:ka, :] * za
    for k in range(ka - 1):
        acc = acc + cw_ref[k:k + 1, :] * ha_ref[:, k, :]
    ya = _silu(_layernorm(acc, lg_ref[...], lb_ref[...]))

    parts = []
    for gi, w in enumerate(_POOL_WINDOWS):
        ls = slice(gi * dg, (gi + 1) * dg)
        s = zb[:, ls]
        for j in range(1, w):
            s = s + hb_ref[:, pb - j, ls]
        inv_cnt = 1.0 / float(min(_PAST_LEN + 1, w))
        pooled = (s * inv_cnt - zb[:, ls]).astype(BF16)
        parts.append(_dot(pooled, pw_ref[gi]))
    yb = jnp.concatenate(parts, axis=1) * ps_ref[...]

    y = jnp.concatenate([ya, yb], axis=1).astype(BF16)
    xo_ref[...] = x + _dot(y, wout_ref[...])

    sa_ref[:, 0:ka - 2, :] = ha_ref[:, 1:ka - 1, :]
    sa_ref[:, ka - 2, :] = za
    sb_ref[:, 0:pb - 1, :] = hb_ref[:, 1:pb, :]
    sb_ref[:, pb - 1, :] = zb


def _sample_even(x2d, g, win, cw, cb, lg, lb, pw, ps, wout, hist_a, hist_b):
    b, d = x2d.shape
    return pl.pallas_call(
        _seven_kernel,
        out_shape=[jax.ShapeDtypeStruct((b, d), F32),
                   jax.ShapeDtypeStruct(hist_a.shape, F32),
                   jax.ShapeDtypeStruct(hist_b.shape, F32)],
        compiler_params=pltpu.CompilerParams(vmem_limit_bytes=_VMEM_LIMIT),
        name="sample_even",
    )(x2d, g, win, cw, cb, lg, lb, pw, ps, wout, hist_a, hist_b)


def _podd_kernel(x_ref, g_ref, win_ref, lg_ref, lb_ref, ws_ref, bst_ref, cw_ref, wout_ref,
                 xo_ref, sv_ref, sd_ref, g_ext, *, tm, d_c, hd_rows):
    t = pl.program_id(1)
    kd, d_d = cw_ref.shape
    heads = ws_ref.shape[0]
    dh = d_c // heads

    @pl.when(t == 0)
    def _():
        g_ext[0:hd_rows, :] = jnp.zeros((hd_rows, d_d), F32)

    x = x_ref[...]
    h = _rms(x, g_ref[...]).astype(BF16)
    p = _dot(h, win_ref[...])
    u = p[:, :d_c]
    v = _layernorm(p[:, d_c:2 * d_c], lg_ref[...], lb_ref[...])
    o = 2 * d_c
    gb = p[:, o:o + d_d]
    gated = p[:, o + d_d:o + 2 * d_d] * p[:, o + 2 * d_d:]

    ri = lax.broadcasted_iota(jnp.int32, (_CHUNK, _CHUNK), 0)
    ci = lax.broadcasted_iota(jnp.int32, (_CHUNK, _CHUNK), 1)
    wm = [jnp.where(ci <= ri, ws_ref[hh], 0.0).astype(BF16) for hh in range(heads)]
    vb = v.astype(BF16)
    rows = []
    for c in range(tm // _CHUNK):
        rs = slice(c * _CHUNK, (c + 1) * _CHUNK)
        parts = [_dot(wm[hh], vb[rs, hh * dh:(hh + 1) * dh]) + bst_ref[:, hh:hh + 1] for hh in range(heads)]
        rows.append(jnp.concatenate(parts, axis=1))
    yc = u * jnp.concatenate(rows, axis=0)

    g_ext[hd_rows:hd_rows + tm, :] = gated
    base = hd_rows - (kd - 1)
    cd = cw_ref[0:1, :] * g_ext[base:base + tm, :]
    for k in range(1, kd):
        cd = cd + cw_ref[k:k + 1, :] * g_ext[base + k:base + k + tm, :]
    yd = gb * cd

    y = jnp.concatenate([yc, yd], axis=1).astype(BF16)
    xo_ref[...] = x + _dot(y, wout_ref[...])

    @pl.when(t == pl.num_programs(1) - 1)
    def _():
        n_open = sv_ref.shape[0]
        sv_ref[...] = v[tm - n_open:, :]
        sd_ref[...] = g_ext[hd_rows + tm - (kd - 1):hd_rows + tm, :]

    g_ext[0:hd_rows, :] = g_ext[tm:tm + hd_rows, :]


def _prompt_odd(x, g, win, lg, lb, ws, bst, cw, wout, tm):
    nb, t, d = x.shape
    kd, d_d = cw.shape
    d_c = lg.shape[1]
    n_open = t - ((t - 1) // _CHUNK) * _CHUNK
    hd_rows = _round_up(kd - 1, _SUBLANE)
    assert t % tm == 0 and tm % _CHUNK == 0 and n_open % _SUBLANE == 0 and n_open <= tm
    x_spec = pl.BlockSpec((None, tm, d), lambda n, i: (n, i, 0))
    kern = functools.partial(_podd_kernel, tm=tm, d_c=d_c, hd_rows=hd_rows)
    return pl.pallas_call(
        kern,
        grid=(nb, t // tm),
        in_specs=[x_spec, _const_spec((1, d)), _const_spec(win.shape), _const_spec((1, d_c)),
                  _const_spec((1, d_c)), _const_spec(ws.shape), _const_spec(bst.shape),
                  _const_spec(cw.shape), _const_spec(wout.shape)],
        out_specs=[x_spec,
                   pl.BlockSpec((None, n_open, d_c), lambda n, i: (n, 0, 0)),
                   pl.BlockSpec((None, kd - 1, d_d), lambda n, i: (n, 0, 0))],
        out_shape=[jax.ShapeDtypeStruct((nb, t, d), F32),
                   jax.ShapeDtypeStruct((nb, n_open, d_c), F32),
                   jax.ShapeDtypeStruct((nb, kd - 1, d_d), F32)],
        scratch_shapes=[pltpu.VMEM((hd_rows + tm, d_d), F32)],
        compiler_params=_params("arbitrary", "arbitrary"),
        name="prompt_odd",
    )(x, g, win, lg, lb, ws, bst, cw, wout)


def _sodd_kernel(x_ref, g_ref, win_ref, lg_ref, lb_ref, ws0_ref, bs0_ref, cw_ref, wout_ref, hd_ref,
                 xo_ref, sv_ref, sd_ref):
    kd, d_d = cw_ref.shape
    d_c = lg_ref.shape[1]
    x = x_ref[...]
    h = _rms(x, g_ref[...]).astype(BF16)
    p = _dot(h, win_ref[...])
    u = p[:, :d_c]
    v = _layernorm(p[:, d_c:2 * d_c], lg_ref[...], lb_ref[...])
    o = 2 * d_c
    gb = p[:, o:o + d_d]
    gated = p[:, o + d_d:o + 2 * d_d] * p[:, o + 2 * d_d:]

    yc = u * (ws0_ref[...] * v + bs0_ref[...])

    cd = cw_ref[kd - 1:kd, :] * gated
    for k in range(kd - 1):
        cd = cd + cw_ref[k:k + 1, :] * hd_ref[:, k, :]
    yd = gb * cd

    y = jnp.concatenate([yc, yd], axis=1).astype(BF16)
    xo_ref[...] = x + _dot(y, wout_ref[...])

    sv_ref[...] = v
    for k in range(kd - 2):
        sd_ref[:, k, :] = hd_ref[:, k + 1, :]
    sd_ref[:, kd - 2, :] = gated


def _sample_odd(x2d, g, win, lg, lb, ws0, bs0, cw, wout, hist_d):
    b, d = x2d.shape
    d_c = lg.shape[1]
    return pl.pallas_call(
        _sodd_kernel,
        out_shape=[jax.ShapeDtypeStruct((b, d), F32),
                   jax.ShapeDtypeStruct((b, d_c), F32),
                   jax.ShapeDtypeStruct(hist_d.shape, F32)],
        compiler_params=pltpu.CompilerParams(vmem_limit_bytes=_VMEM_LIMIT),
        name="sample_odd",
    )(x2d, g, win, lg, lb, ws0, bs0, cw, wout, hist_d)


def kernel(x_prompt, x_sample, mem_prompt, state_convA, state_poolB, state_convD, cache_mem_k, cache_mem_v, norm_mix, norm_x, norm_ffn, norm_final, w_in_even, conv_a_w, conv_a_b, ln_a_g, ln_a_b, pool_b_w, pool_b_scale, w_out_even, w_in_odd, ln_c_g, ln_c_b, ws_c, bs_c, conv_d_w, w_out_odd, wq_x, wk_x, wv_x, wo_x, w_gate, w_up, w_down):
    nb, seq, d = x_prompt.shape
    db, dec_seq, _ = x_sample.shape
    assert dec_seq == 1
    depth = wq_x.shape[0]
    mem_len = mem_prompt.shape[1]
    heads, dh = cache_mem_k.shape[3], cache_mem_k.shape[4]
    tm = _PROMPT_TILE

    def bf(a):
        return a.astype(BF16)

    def row(a):
        return a.reshape(1, -1)

    p_k, p_v = _kv_proj(mem_prompt.reshape(nb * mem_len, d), bf(wk_x), bf(wv_x), tm=1024)
    p_k4 = p_k.reshape(depth, nb, mem_len, d)
    p_v4 = p_v.reshape(depth, nb, mem_len, d)

    xp = x_prompt
    xs = x_sample.reshape(db, d)
    p_a, p_b, p_c, p_d = [], [], [], []
    s_a, s_b, s_c, s_d = [], [], [], []
    for layer in range(depth):
        i = layer // 2
        g_mix = row(norm_mix[layer])
        if layer % 2 == 0:
            win, wout, pw = bf(w_in_even[i]), bf(w_out_even[i]), bf(pool_b_w[i])
            args = (win, conv_a_w[i], row(conv_a_b[i]), row(ln_a_g[i]), row(ln_a_b[i]), pw,
                    row(pool_b_scale[i]), wout)
            xp, a, b = _prompt_even(xp, g_mix, *args, tm=tm)
            p_a.append(a)
            p_b.append(b)
            xs, a, b = _sample_even(xs, g_mix, *args, state_convA[i], state_poolB[i])
            s_a.append(a)
            s_b.append(b)
        else:
            win, wout = bf(w_in_odd[i]), bf(w_out_odd[i])
            lg, lb = row(ln_c_g[i]), row(ln_c_b[i])
            xp, c, dd = _prompt_odd(xp, g_mix, win, lg, lb, ws_c[i], bs_c[i].T, conv_d_w[i], wout, tm=tm)
            p_c.append(c.reshape(nb, -1, _C_HEADS, c.shape[-1] // _C_HEADS))
            p_d.append(dd)
            ws0 = row(jnp.repeat(ws_c[i, :, 0, 0], lg.shape[1] // _C_HEADS))
            bs0 = row(jnp.repeat(bs_c[i, :, 0], lg.shape[1] // _C_HEADS))
            xs, c, dd = _sample_odd(xs, g_mix, win, lg, lb, ws0, bs0, conv_d_w[i], wout, state_convD[i])
            s_c.append(c.reshape(db, 1, _C_HEADS, c.shape[-1] // _C_HEADS))
            s_d.append(dd)

        g_x = row(norm_x[layer])
        wq, wo = bf(wq_x[layer]), bf(wo_x[layer])
        xp = _prompt_attn(xp, g_x, wq, p_k4, p_v4, wo, layer, tm=tm)
        xs = _sample_attn(xs, g_x, wq, cache_mem_k, cache_mem_v, wo, layer, bb=_SUBLANE)

        g_f = row(norm_ffn[layer])
        wg, wu, wd = bf(w_gate[layer]), bf(w_up[layer]), bf(w_down[layer])
        last = layer == depth - 1
        g_fin = row(norm_final)
        xp = _ffn(xp.reshape(nb * seq, d), g_f, wg, wu, wd, g_fin, tm=tm, final_norm=last).reshape(nb, seq, d)
        xs = _ffn(xs, g_f, wg, wu, wd, g_fin, tm=db, final_norm=last)

    shape5 = (depth, nb, mem_len, heads, dh)
    return (xp, xs.reshape(db, 1, d), jnp.stack(p_a), jnp.stack(p_b), jnp.stack(p_c), jnp.stack(p_d),
            p_k.reshape(shape5), p_v.reshape(shape5),
            jnp.stack(s_a), jnp.stack(s_b), jnp.stack(s_c), jnp.stack(s_d))
```

```python
import functools

import jax
import jax.numpy as jnp
from jax import lax
from jax.experimental import pallas as pl
from jax.experimental.pallas import tpu as pltpu

F32 = jnp.float32
BF16 = jnp.bfloat16

_EPS = 1e-6
_X_HEADS = 4
_C_HEADS = 4
_CHUNK = 128
_POOL_WINDOWS = (2, 4, 8, 16)
_PAST_LEN = 16384

_SUBLANE = 8
_LANE = 128
_VMEM_LIMIT = 56 * 1024 * 1024

_PROMPT_TILE = 512
_FF_CHUNKS = (768, 768, 768, 512)


def _round_up(n, m):
    return -(-n // m) * m


def _params(*sem):
    return pltpu.CompilerParams(dimension_semantics=sem, vmem_limit_bytes=_VMEM_LIMIT)


def _const_spec(shape):
    nd = len(shape)
    return pl.BlockSpec(shape, lambda *_: (0,) * nd, pipeline_mode=pl.Buffered(1))


def _dot(a, b):
    return jnp.dot(a, b, preferred_element_type=F32)


def _rms(x, g):
    y = x * lax.rsqrt(jnp.mean(x * x, axis=-1, keepdims=True) + _EPS)
    return y * g


def _layernorm(x, g, b):
    mu = jnp.mean(x, axis=-1, keepdims=True)
    d = x - mu
    var = jnp.mean(d * d, axis=-1, keepdims=True)
    return d * lax.rsqrt(var + _EPS) * g + b


def _silu(x):
    return x * jax.nn.sigmoid(x)


def _softmax_rows(s):
    m = jnp.max(s, axis=-1, keepdims=True)
    e = jnp.exp(s - m)
    return e * (1.0 / jnp.sum(e, axis=-1, keepdims=True))


def _kv_kernel(m_ref, wk_ref, wv_ref, k_ref, v_ref):
    m = m_ref[...].astype(BF16)
    k_ref[...] = _dot(m, wk_ref[...])
    v_ref[...] = _dot(m, wv_ref[...])


def _kv_proj(mem2d, wk, wv, tm):
    rows, d = mem2d.shape
    depth = wk.shape[0]
    w_spec = pl.BlockSpec((None, d, d), lambda l, m: (l, 0, 0))
    o_spec = pl.BlockSpec((None, tm, d), lambda l, m: (l, m, 0))
    return pl.pallas_call(
        _kv_kernel,
        grid=(depth, rows // tm),
        in_specs=[pl.BlockSpec((tm, d), lambda l, m: (m, 0)), w_spec, w_spec],
        out_specs=[o_spec, o_spec],
        out_shape=[jax.ShapeDtypeStruct((depth, rows, d), F32)] * 2,
        compiler_params=_params("arbitrary", "arbitrary"),
        name="kv_proj",
    )(mem2d, wk, wv)


def _ffn_kernel(x_ref, g_ref, wg_ref, wu_ref, wd_ref, gf_ref, o_ref, *, chunks, final_norm):
    x = x_ref[...]
    h = _rms(x, g_ref[...]).astype(BF16)
    acc = x
    lo = 0
    for width in chunks:
        gate = _dot(h, wg_ref[:, lo:lo + width])
        up = _dot(h, wu_ref[:, lo:lo + width])
        act = (_silu(gate) * up).astype(BF16)
        acc = acc + _dot(act, wd_ref[lo:lo + width, :])
        lo += width
    if final_norm:
        acc = _rms(acc, gf_ref[...])
    o_ref[...] = acc


def _ffn(x2d, g, wg, wu, wd, gf, tm, final_norm):
    rows, d = x2d.shape
    ff = wg.shape[1]
    assert sum(_FF_CHUNKS) == ff
    kern = functools.partial(_ffn_kernel, chunks=_FF_CHUNKS, final_norm=final_norm)
    row_spec = pl.BlockSpec((tm, d), lambda i: (i, 0))
    return pl.pallas_call(
        kern,
        grid=(rows // tm,),
        in_specs=[row_spec, _const_spec((1, d)), _const_spec((d, ff)), _const_spec((d, ff)),
                  _const_spec((ff, d)), _const_spec((1, d))],
        out_specs=row_spec,
        out_shape=jax.ShapeDtypeStruct((rows, d), F32),
        compiler_params=_params("arbitrary"),
        name="ffn",
    )(x2d, g, wg, wu, wd, gf)


def _pattn_kernel(x_ref, g_ref, wq_ref, k_ref, v_ref, wo_ref, o_ref, *, heads):
    x = x_ref[...]
    d = x.shape[-1]
    dh = d // heads
    h = _rms(x, g_ref[...]).astype(BF16)
    q = _dot(h, wq_ref[...]).astype(BF16)
    outs = []
    for hd in range(heads):
        sl = slice(hd * dh, (hd + 1) * dh)
        kh = k_ref[:, sl].astype(BF16)
        s = lax.dot_general(q[:, sl], kh, (((1,), (1,)), ((), ())), preferred_element_type=F32)
        a = _softmax_rows(s * (dh ** -0.5)).astype(BF16)
        outs.append(_dot(a, v_ref[:, sl].astype(BF16)))
    o = jnp.concatenate(outs, axis=1).astype(BF16)
    o_ref[...] = x + _dot(o, wo_ref[...])


def _prompt_attn(x, g, wq, k, v, wo, layer, tm):
    nb, t, d = x.shape
    mem_len = k.shape[2]
    x_spec = pl.BlockSpec((None, tm, d), lambda n, i: (n, i, 0))
    kv_spec = pl.BlockSpec((None, None, mem_len, d), lambda n, i: (layer, n, 0, 0))
    return pl.pallas_call(
        functools.partial(_pattn_kernel, heads=_X_HEADS),
        grid=(nb, t // tm),
        in_specs=[x_spec, _const_spec((1, d)), _const_spec((d, d)), kv_spec, kv_spec, _const_spec((d, d))],
        out_specs=x_spec,
        out_shape=jax.ShapeDtypeStruct((nb, t, d), F32),
        compiler_params=_params("arbitrary", "arbitrary"),
        name="prompt_attn",
    )(x, g, wq, k, v, wo)


def _sattn_copies(k_hbm, v_hbm, kbuf, vbuf, sem, layer, blk, slot):
    _, heads, bb = kbuf.shape[:3]
    rows = pl.ds(blk * bb, bb)
    cps = []
    for hd in range(heads):
        cps.append(pltpu.make_async_copy(k_hbm.at[layer, rows, :, hd, :], kbuf.at[slot, hd], sem.at[slot, 0, hd]))
        cps.append(pltpu.make_async_copy(v_hbm.at[layer, rows, :, hd, :], vbuf.at[slot, hd], sem.at[slot, 1, hd]))
    return cps


def _sattn_kernel(x_ref, g_ref, wq_ref, k_hbm, v_hbm, wo_ref, o_ref, q_scr, o_scr, kbuf, vbuf, sem, *, layer):
    i = pl.program_id(0)
    n = pl.num_programs(0)
    _, heads, bb, _, dh = kbuf.shape
    d = heads * dh
    slot = lax.rem(i, 2)
    copies = functools.partial(_sattn_copies, k_hbm, v_hbm, kbuf, vbuf, sem, layer)

    @pl.when(i == 0)
    def _():
        for cp in copies(0, 0):
            cp.start()
        h = _rms(x_ref[...], g_ref[...]).astype(BF16)
        q_scr[...] = _dot(h, wq_ref[...])

    @pl.when(i + 1 < n)
    def _():
        for cp in copies(i + 1, 1 - slot):
            cp.start()

    for cp in copies(i, slot):
        cp.wait()

    row0 = pl.multiple_of(i * bb, bb)
    q_blk = q_scr[pl.ds(row0, bb), :] * (dh ** -0.5)
    rows = lax.broadcasted_iota(jnp.int32, (bb, 1), 0)
    o_blk = jnp.zeros((bb, d), F32)
    for j in range(bb):
        outs = []
        for hd in range(heads):
            sl = slice(hd * dh, (hd + 1) * dh)
            kh = kbuf[slot, hd, j].astype(BF16)
            q_rep = jnp.broadcast_to(q_blk[j:j + 1, sl], (_LANE, dh)).astype(BF16)
            s = lax.dot_general(kh, q_rep, (((1,), (1,)), ((), ())), preferred_element_type=F32)
            e = jnp.exp(s - jnp.max(s, axis=0, keepdims=True))
            den = jnp.sum(e, axis=0, keepdims=True)
            reps = dh // _LANE
            num = jnp.sum(jnp.concatenate([e] * reps, axis=1) * vbuf[slot, hd, j], axis=0, keepdims=True)
            outs.append(num * (1.0 / jnp.concatenate([den] * reps, axis=1)))
        o_blk = jnp.where(rows == j, jnp.concatenate(outs, axis=1), o_blk)
    o_scr[pl.ds(row0, bb), :] = o_blk

    @pl.when(i == n - 1)
    def _():
        o_ref[...] = x_ref[...] + _dot(o_scr[...].astype(BF16), wo_ref[...])


def _sample_attn(x2d, g, wq, k5, v5, wo, layer, bb):
    b, d = x2d.shape
    _, _, mem_len, heads, dh = k5.shape
    assert bb == _SUBLANE and b % bb == 0
    any_spec = pl.BlockSpec(memory_space=pl.ANY)
    kv_buf = pltpu.VMEM((2, heads, bb, mem_len, dh), F32)
    return pl.pallas_call(
        functools.partial(_sattn_kernel, layer=layer),
        grid=(b // bb,),
        in_specs=[_const_spec((b, d)), _const_spec((1, d)), _const_spec((d, d)), any_spec, any_spec,
                  _const_spec((d, d))],
        out_specs=pl.BlockSpec((b, d), lambda i: (0, 0)),
        out_shape=jax.ShapeDtypeStruct((b, d), F32),
        scratch_shapes=[pltpu.VMEM((b, d), F32), pltpu.VMEM((b, d), F32), kv_buf, kv_buf,
                        pltpu.SemaphoreType.DMA((2, 2, heads))],
        compiler_params=_params("arbitrary"),
        name="sample_attn",
    )(x2d, g, wq, k5, v5, wo)


_CONV_ROWS = 64


def _peven_kernel(x_ref, g_ref, win_ref, cw_ref, cb_ref, lg_ref, lb_ref, pw_ref, ps_ref, wout_ref,
                  xo_ref, sa_ref, sb_ref, za_ext, zb_ext, ca_buf, *, tm, d_a, ha, hb):
    t = pl.program_id(1)
    ka = cw_ref.shape[0]
    nca = za_ext.shape[0]
    ncb = zb_ext.shape[0]
    pb = sb_ref.shape[0]

    @pl.when(t == 0)
    def _():
        za_ext[:, 0:ha, :] = jnp.zeros((nca, ha, _LANE), F32)
        zb_ext[:, 0:hb, :] = jnp.zeros((ncb, hb, _LANE), F32)

    x = x_ref[...]
    h = _rms(x, g_ref[...]).astype(BF16)
    p = _dot(h, win_ref[...])
    za = p[:, :d_a] * jax.nn.sigmoid(p[:, d_a:2 * d_a])
    zb = p[:, 2 * d_a:]
    for c in range(nca):
        za_ext[c, ha:ha + tm, :] = za[:, c * _LANE:(c + 1) * _LANE]
    for c in range(ncb):
        zb_ext[c, hb:hb + tm, :] = zb[:, c * _LANE:(c + 1) * _LANE]

    base = ha - (ka - 1)
    for c in range(nca):
        ls = slice(c * _LANE, (c + 1) * _LANE)
        for r in range(tm // _CONV_ROWS):
            r0 = r * _CONV_ROWS
            acc = jnp.broadcast_to(cb_ref[:, ls], (_CONV_ROWS, _LANE))
            for k in range(ka):
                acc = acc + cw_ref[k:k + 1, ls] * za_ext[c, base + r0 + k:base + r0 + k + _CONV_ROWS, :]
            ca_buf[r0:r0 + _CONV_ROWS, ls] = acc
    ya = _silu(_layernorm(ca_buf[...], lg_ref[...], lb_ref[...]))

    pos = t * tm + lax.broadcasted_iota(jnp.int32, (tm, 1), 0)
    parts = []
    for gi, w in enumerate(_POOL_WINDOWS):
        zg = zb[:, gi * _LANE:(gi + 1) * _LANE]
        s = zg
        for j in range(1, w):
            s = s + zb_ext[gi, hb - j:hb - j + tm, :]
        inv_cnt = 1.0 / jnp.minimum(pos + 1, w).astype(F32)
        pooled = (s * inv_cnt - zg).astype(BF16)
        parts.append(_dot(pooled, pw_ref[gi]))
    yb = jnp.concatenate(parts, axis=1) * ps_ref[...]

    y = jnp.concatenate([ya, yb], axis=1).astype(BF16)
    xo_ref[...] = x + _dot(y, wout_ref[...])

    @pl.when(t == pl.num_programs(1) - 1)
    def _():
        for c in range(nca):
            sa_ref[:, c * _LANE:(c + 1) * _LANE] = za_ext[c, ha + tm - (ka - 1):ha + tm, :]
        for c in range(ncb):
            sb_ref[:, c * _LANE:(c + 1) * _LANE] = zb_ext[c, hb + tm - pb:hb + tm, :]

    za_ext[:, 0:ha, :] = za_ext[:, tm:tm + ha, :]
    zb_ext[:, 0:hb, :] = zb_ext[:, tm:tm + hb, :]


def _prompt_even(x, g, win, cw, cb, lg, lb, pw, ps, wout, tm):
    nb, t, d = x.shape
    ka, d_a = cw.shape
    d_b = ps.shape[1]
    pb = max(_POOL_WINDOWS) - 1
    ha = _round_up(ka - 1, _SUBLANE)
    hb = _round_up(pb, _SUBLANE)
    assert t % tm == 0 and tm % _CONV_ROWS == 0 and tm >= max(ha, hb)
    assert d_a % _LANE == 0 and d_b == _LANE * len(_POOL_WINDOWS)
    x_spec = pl.BlockSpec((None, tm, d), lambda n, i: (n, i, 0))
    kern = functools.partial(_peven_kernel, tm=tm, d_a=d_a, ha=ha, hb=hb)
    return pl.pallas_call(
        kern,
        grid=(nb, t // tm),
        in_specs=[x_spec, _const_spec((1, d)), _const_spec(win.shape), _const_spec(cw.shape),
                  _const_spec((1, d_a)), _const_spec((1, d_a)), _const_spec((1, d_a)),
                  _const_spec(pw.shape), _const_spec((1, d_b)), _const_spec(wout.shape)],
        out_specs=[x_spec,
                   pl.BlockSpec((None, ka - 1, d_a), lambda n, i: (n, 0, 0)),
                   pl.BlockSpec((None, pb, d_b), lambda n, i: (n, 0, 0))],
        out_shape=[jax.ShapeDtypeStruct((nb, t, d), F32),
                   jax.ShapeDtypeStruct((nb, ka - 1, d_a), F32),
                   jax.ShapeDtypeStruct((nb, pb, d_b), F32)],
        scratch_shapes=[pltpu.VMEM((d_a // _LANE, ha + tm, _LANE), F32),
                        pltpu.VMEM((d_b // _LANE, hb + tm, _LANE), F32),
                        pltpu.VMEM((tm, d_a), F32)],
        compiler_params=_params("arbitrary", "arbitrary"),
        name="prompt_even",
    )(x, g, win, cw, cb, lg, lb, pw, ps, wout)


def _seven_kernel(x_ref, g_ref, win_ref, cw_ref, cb_ref, lg_ref, lb_ref, pw_ref, ps_ref, wout_ref,
                  ha_ref, hb_ref, xo_ref, sa_ref, sb_ref):
    ka, d_a = cw_ref.shape
    pb = hb_ref.shape[1]
    d_b = hb_ref.shape[2]
    dg = d_b // len(_POOL_WINDOWS)
    x = x_ref[...]
    h = _rms(x, g_ref[...]).astype(BF16)
    p = _dot(h, win_ref[...])
    za = p[:, :d_a] * jax.nn.sigmoid(p[:, d_a:2 * d_a])
    zb = p[:, 2 * d_a:]

    acc = cb_ref[...] + cw_ref[ka - 1:ka, :] * za
    for k in range(ka - 1):
        acc = acc + cw_ref[k:k + 1, :] * ha_ref[:, k, :]
    ya = _silu(_layernorm(acc, lg_ref[...], lb_ref[...]))

    parts = []
    for gi, w in enumerate(_POOL_WINDOWS):
        ls = slice(gi * dg, (gi + 1) * dg)
        s = zb[:, ls]
        for j in range(1, w):
            s = s + hb_ref[:, pb - j, ls]
        inv_cnt = 1.0 / float(min(_PAST_LEN + 1, w))
        pooled = (s * inv_cnt - zb[:, ls]).astype(BF16)
        parts.append(_dot(pooled, pw_ref[gi]))
    yb = jnp.concatenate(parts, axis=1) * ps_ref[...]

    y = jnp.concatenate([ya, yb], axis=1).astype(BF16)
    xo_ref[...] = x + _dot(y, wout_ref[...])

    sa_ref[:, 0:ka - 2, :] = ha_ref[:, 1:ka - 1, :]
    sa_ref[:, ka - 2, :] = za
    sb_ref[:, 0:pb - 1, :] = hb_ref[:, 1:pb, :]
    sb_ref[:, pb - 1, :] = zb


def _sample_even(x2d, g, win, cw, cb, lg, lb, pw, ps, wout, hist_a, hist_b):
    b, d = x2d.shape
    return pl.pallas_call(
        _seven_kernel,
        out_shape=[jax.ShapeDtypeStruct((b, d), F32),
                   jax.ShapeDtypeStruct(hist_a.shape, F32),
                   jax.ShapeDtypeStruct(hist_b.shape, F32)],
        compiler_params=pltpu.CompilerParams(vmem_limit_bytes=_VMEM_LIMIT),
        name="sample_even",
    )(x2d, g, win, cw, cb, lg, lb, pw, ps, wout, hist_a, hist_b)


def _podd_kernel(x_ref, g_ref, win_ref, lg_ref, lb_ref, ws_ref, bst_ref, cw_ref, wout_ref,
                 xo_ref, sv_ref, sd_ref, g_ext, *, tm, d_c, hd_rows):
    t = pl.program_id(1)
    kd, d_d = cw_ref.shape
    heads = ws_ref.shape[0]
    dh = d_c // heads
    ncd = g_ext.shape[0]

    @pl.when(t == 0)
    def _():
        g_ext[:, 0:hd_rows, :] = jnp.zeros((ncd, hd_rows, _LANE), F32)

    x = x_ref[...]
    h = _rms(x, g_ref[...]).astype(BF16)
    p = _dot(h, win_ref[...])
    u = p[:, :d_c]
    v = _layernorm(p[:, d_c:2 * d_c], lg_ref[...], lb_ref[...])
    o = 2 * d_c
    gb = p[:, o:o + d_d]
    gated = p[:, o + d_d:o + 2 * d_d] * p[:, o + 2 * d_d:]

    ri = lax.broadcasted_iota(jnp.int32, (_CHUNK, _CHUNK), 0)
    ci = lax.broadcasted_iota(jnp.int32, (_CHUNK, _CHUNK), 1)
    wm = [jnp.where(ci <= ri, ws_ref[hh], 0.0).astype(BF16) for hh in range(heads)]
    vb = v.astype(BF16)
    rows = []
    for c in range(tm // _CHUNK):
        rs = slice(c * _CHUNK, (c + 1) * _CHUNK)
        parts = [_dot(wm[hh], vb[rs, hh * dh:(hh + 1) * dh]) + bst_ref[:, hh:hh + 1] for hh in range(heads)]
        rows.append(jnp.concatenate(parts, axis=1))
    yc = u * jnp.concatenate(rows, axis=0)

    base = hd_rows - (kd - 1)
    cols = []
    for c in range(ncd):
        ls = slice(c * _LANE, (c + 1) * _LANE)
        g_ext[c, hd_rows:hd_rows + tm, :] = gated[:, ls]
        cd = cw_ref[0:1, ls] * g_ext[c, base:base + tm, :]
        for k in range(1, kd):
            cd = cd + cw_ref[k:k + 1, ls] * g_ext[c, base + k:base + k + tm, :]
        cols.append(cd)
    yd = gb * jnp.concatenate(cols, axis=1)

    y = jnp.concatenate([yc, yd], axis=1).astype(BF16)
    xo_ref[...] = x + _dot(y, wout_ref[...])

    @pl.when(t == pl.num_programs(1) - 1)
    def _():
        n_open = sv_ref.shape[0]
        sv_ref[...] = v[tm - n_open:, :]
        for c in range(ncd):
            sd_ref[:, c * _LANE:(c + 1) * _LANE] = g_ext[c, hd_rows + tm - (kd - 1):hd_rows + tm, :]

    g_ext[:, 0:hd_rows, :] = g_ext[:, tm:tm + hd_rows, :]


def _prompt_odd(x, g, win, lg, lb, ws, bst, cw, wout, tm):
    nb, t, d = x.shape
    kd, d_d = cw.shape
    d_c = lg.shape[1]
    n_open = t - ((t - 1) // _CHUNK) * _CHUNK
    hd_rows = _round_up(kd - 1, _SUBLANE)
    assert t % tm == 0 and tm % _CHUNK == 0 and n_open % _SUBLANE == 0 and n_open <= tm
    x_spec = pl.BlockSpec((None, tm, d), lambda n, i: (n, i, 0))
    kern = functools.partial(_podd_kernel, tm=tm, d_c=d_c, hd_rows=hd_rows)
    return pl.pallas_call(
        kern,
        grid=(nb, t // tm),
        in_specs=[x_spec, _const_spec((1, d)), _const_spec(win.shape), _const_spec((1, d_c)),
                  _const_spec((1, d_c)), _const_spec(ws.shape), _const_spec(bst.shape),
                  _const_spec(cw.shape), _const_spec(wout.shape)],
        out_specs=[x_spec,
                   pl.BlockSpec((None, n_open, d_c), lambda n, i: (n, 0, 0)),
                   pl.BlockSpec((None, kd - 1, d_d), lambda n, i: (n, 0, 0))],
        out_shape=[jax.ShapeDtypeStruct((nb, t, d), F32),
                   jax.ShapeDtypeStruct((nb, n_open, d_c), F32),
                   jax.ShapeDtypeStruct((nb, kd - 1, d_d), F32)],
        scratch_shapes=[pltpu.VMEM((d_d // _LANE, hd_rows + tm, _LANE), F32)],
        compiler_params=_params("arbitrary", "arbitrary"),
        name="prompt_odd",
    )(x, g, win, lg, lb, ws, bst, cw, wout)


def _sodd_kernel(x_ref, g_ref, win_ref, lg_ref, lb_ref, ws0_ref, bs0_ref, cw_ref, wout_ref, hd_ref,
                 xo_ref, sv_ref, sd_ref):
    kd, d_d = cw_ref.shape
    d_c = lg_ref.shape[1]
    x = x_ref[...]
    h = _rms(x, g_ref[...]).astype(BF16)
    p = _dot(h, win_ref[...])
    u = p[:, :d_c]
    v = _layernorm(p[:, d_c:2 * d_c], lg_ref[...], lb_ref[...])
    o = 2 * d_c
    gb = p[:, o:o + d_d]
    gated = p[:, o + d_d:o + 2 * d_d] * p[:, o + 2 * d_d:]

    yc = u * (ws0_ref[...] * v + bs0_ref[...])

    cd = cw_ref[kd - 1:kd, :] * gated
    for k in range(kd - 1):
        cd = cd + cw_ref[k:k + 1, :] * hd_ref[:, k, :]
    yd = gb * cd

    y = jnp.concatenate([yc, yd], axis=1).astype(BF16)
    xo_ref[...] = x + _dot(y, wout_ref[...])

    sv_ref[...] = v
    for k in range(kd - 2):
        sd_ref[:, k, :] = hd_ref[:, k + 1, :]
    sd_ref[:, kd - 2, :] = gated


def _sample_odd(x2d, g, win, lg, lb, ws0, bs0, cw, wout, hist_d):
    b, d = x2d.shape
    d_c = lg.shape[1]
    return pl.pallas_call(
        _sodd_kernel,
        out_shape=[jax.ShapeDtypeStruct((b, d), F32),
                   jax.ShapeDtypeStruct((b, d_c), F32),
                   jax.ShapeDtypeStruct(hist_d.shape, F32)],
        compiler_params=pltpu.CompilerParams(vmem_limit_bytes=_VMEM_LIMIT),
        name="sample_odd",
    )(x2d, g, win, lg, lb, ws0, bs0, cw, wout, hist_d)


def kernel(x_prompt, x_sample, mem_prompt, state_convA, state_poolB, state_convD, cache_mem_k, cache_mem_v, norm_mix, norm_x, norm_ffn, norm_final, w_in_even, conv_a_w, conv_a_b, ln_a_g, ln_a_b, pool_b_w, pool_b_scale, w_out_even, w_in_odd, ln_c_g, ln_c_b, ws_c, bs_c, conv_d_w, w_out_odd, wq_x, wk_x, wv_x, wo_x, w_gate, w_up, w_down):
    nb, seq, d = x_prompt.shape
    db, dec_seq, _ = x_sample.shape
    assert dec_seq == 1
    depth = wq_x.shape[0]
    mem_len = mem_prompt.shape[1]
    heads, dh = cache_mem_k.shape[3], cache_mem_k.shape[4]
    tm = _PROMPT_TILE

    def bf(a):
        return a.astype(BF16)

    def row(a):
        return a.reshape(1, -1)

    p_k, p_v = _kv_proj(mem_prompt.reshape(nb * mem_len, d), bf(wk_x), bf(wv_x), tm=1024)
    p_k4 = p_k.reshape(depth, nb, mem_len, d)
    p_v4 = p_v.reshape(depth, nb, mem_len, d)

    xp = x_prompt
    xs = x_sample.reshape(db, d)
    p_a, p_b, p_c, p_d = [], [], [], []
    s_a, s_b, s_c, s_d = [], [], [], []
    for layer in range(depth):
        i = layer // 2
        g_mix = row(norm_mix[layer])
        if layer % 2 == 0:
            win, wout, pw = bf(w_in_even[i]), bf(w_out_even[i]), bf(pool_b_w[i])
            args = (win, conv_a_w[i], row(conv_a_b[i]), row(ln_a_g[i]), row(ln_a_b[i]), pw,
                    row(pool_b_scale[i]), wout)
            xp, a, b = _prompt_even(xp, g_mix, *args, tm=tm)
            p_a.append(a)
            p_b.append(b)
            xs, a, b = _sample_even(xs, g_mix, *args, state_convA[i], state_poolB[i])
            s_a.append(a)
            s_b.append(b)
        else:
            win, wout = bf(w_in_odd[i]), bf(w_out_odd[i])
            lg, lb = row(ln_c_g[i]), row(ln_c_b[i])
            xp, c, dd = _prompt_odd(xp, g_mix, win, lg, lb, ws_c[i], bs_c[i].T, conv_d_w[i], wout, tm=tm)
            p_c.append(c.reshape(nb, -1, _C_HEADS, c.shape[-1] // _C_HEADS))
            p_d.append(dd)
            ws0 = row(jnp.repeat(ws_c[i, :, 0, 0], lg.shape[1] // _C_HEADS))
            bs0 = row(jnp.repeat(bs_c[i, :, 0], lg.shape[1] // _C_HEADS))
            xs, c, dd = _sample_odd(xs, g_mix, win, lg, lb, ws0, bs0, conv_d_w[i], wout, state_convD[i])
            s_c.append(c.reshape(db, 1, _C_HEADS, c.shape[-1] // _C_HEADS))
            s_d.append(dd)

        g_x = row(norm_x[layer])
        wq, wo = bf(wq_x[layer]), bf(wo_x[layer])
        xp = _prompt_attn(xp, g_x, wq, p_k4, p_v4, wo, layer, tm=tm)
        xs = _sample_attn(xs, g_x, wq, cache_mem_k, cache_mem_v, wo, layer, bb=_SUBLANE)

        g_f = row(norm_ffn[layer])
        wg, wu, wd = bf(w_gate[layer]), bf(w_up[layer]), bf(w_down[layer])
        last = layer == depth - 1
        g_fin = row(norm_final)
        xp = _ffn(xp.reshape(nb * seq, d), g_f, wg, wu, wd, g_fin, tm=tm, final_norm=last).reshape(nb, seq, d)
        xs = _ffn(xs, g_f, wg, wu, wd, g_fin, tm=db, final_norm=last)

    shape5 = (depth, nb, mem_len, heads, dh)
    return (xp, xs.reshape(db, 1, d), jnp.stack(p_a), jnp.stack(p_b), jnp.stack(p_c), jnp.stack(p_d),
            p_k.reshape(shape5), p_v.reshape(shape5),
            jnp.stack(s_a), jnp.stack(s_b), jnp.stack(s_c), jnp.stack(s_d))
```

```python
import functools

import jax
import jax.numpy as jnp
from jax import lax
from jax.experimental import pallas as pl
from jax.experimental.pallas import tpu as pltpu

F32 = jnp.float32
BF16 = jnp.bfloat16

_EPS = 1e-6
_X_HEADS = 4
_C_HEADS = 4
_CHUNK = 128
_POOL_WINDOWS = (2, 4, 8, 16)
_PAST_LEN = 16384

_SUBLANE = 8
_LANE = 128
_VMEM_LIMIT = 56 * 1024 * 1024

_PROMPT_TILE = 512
_ATTN_TILE = 1024
_ATTN_PARTS = 2
_KV_BATCHES = 2
_FF_CHUNKS = (768, 768, 768, 512)
_CONV_ROWS = 64


def _round_up(n, m):
    return -(-n // m) * m


def _params(*sem):
    return pltpu.CompilerParams(dimension_semantics=sem, vmem_limit_bytes=_VMEM_LIMIT)


def _const_spec(shape):
    nd = len(shape)
    return pl.BlockSpec(shape, lambda *_: (0,) * nd, pipeline_mode=pl.Buffered(1))


def _whole_spec(shape):
    nd = len(shape)
    return pl.BlockSpec(shape, lambda *_: (0,) * nd)


def _sel(stacked, idx):
    if stacked.ndim == 2:
        stacked = stacked[:, None, :]
    tail = stacked.shape[1:]
    spec = pl.BlockSpec((None,) + tail, lambda *_: (idx,) + (0,) * len(tail), pipeline_mode=pl.Buffered(1))
    return stacked, spec


def _dot(a, b):
    return jnp.dot(a, b, preferred_element_type=F32)


def _rms(x, g):
    y = x * lax.rsqrt(jnp.mean(x * x, axis=-1, keepdims=True) + _EPS)
    return y * g


def _layernorm(x, g, b):
    mu = jnp.mean(x, axis=-1, keepdims=True)
    d = x - mu
    var = jnp.mean(d * d, axis=-1, keepdims=True)
    return d * lax.rsqrt(var + _EPS) * g + b


def _silu(x):
    return x * jax.nn.sigmoid(x)


def _softmax_rows(s):
    m = jnp.max(s, axis=-1, keepdims=True)
    e = jnp.exp(s - m)
    return e * (1.0 / jnp.sum(e, axis=-1, keepdims=True))


_DONE = object()


def _run_staggered(seqs):
    live = []
    pending = list(seqs)
    while live or pending:
        if pending:
            live.append(pending.pop(0))
        live = [s for s in live if next(s, _DONE) is not _DONE]


def _kv_out_copies(kbuf, vbuf, k_hbm, v_hbm, sem, layer, blk, slot):
    _, heads, nbk = kbuf.shape[:3]
    rows = pl.ds(blk * nbk, nbk)
    cps = []
    for hd in range(heads):
        cps.append(pltpu.make_async_copy(kbuf.at[slot, hd], k_hbm.at[layer, rows, :, hd, :], sem.at[slot, 0, hd]))
        cps.append(pltpu.make_async_copy(vbuf.at[slot, hd], v_hbm.at[layer, rows, :, hd, :], sem.at[slot, 1, hd]))
    return cps


def _kv_kernel(m_ref, wk_ref, wv_ref, k_hbm, v_hbm, kbuf, vbuf, sem):
    nj = pl.num_programs(1)
    layer = pl.program_id(0)
    blk = pl.program_id(1)
    step = layer * nj + blk
    last = pl.num_programs(0) * nj - 1
    slot = lax.rem(step, 2)
    _, heads, nbk, mem, dh = kbuf.shape
    copies = functools.partial(_kv_out_copies, kbuf, vbuf, k_hbm, v_hbm, sem)

    @pl.when(step >= 2)
    def _():
        prev = step - 2
        for cp in copies(lax.div(prev, nj), lax.rem(prev, nj), slot):
            cp.wait()

    m = m_ref[...].reshape(nbk * mem, m_ref.shape[-1]).astype(BF16)
    k = _dot(m, wk_ref[...])
    v = _dot(m, wv_ref[...])
    for hd in range(heads):
        for b in range(nbk):
            kbuf[slot, hd, b] = k[b * mem:(b + 1) * mem, hd * dh:(hd + 1) * dh]
            vbuf[slot, hd, b] = v[b * mem:(b + 1) * mem, hd * dh:(hd + 1) * dh]
    for cp in copies(layer, blk, slot):
        cp.start()

    @pl.when(step == last)
    def _():
        @pl.when(step >= 1)
        def _():
            prev = step - 1
            for cp in copies(lax.div(prev, nj), lax.rem(prev, nj), 1 - slot):
                cp.wait()
        for cp in copies(layer, blk, slot):
            cp.wait()


def _kv_proj(mem3, wk, wv, heads):
    nb, mem, d = mem3.shape
    depth = wk.shape[0]
    dh = d // heads
    nbk = _KV_BATCHES
    assert nb % nbk == 0 and mem % _SUBLANE == 0
    w_spec = pl.BlockSpec((None, d, d), lambda l, j: (l, 0, 0))
    any_spec = pl.BlockSpec(memory_space=pl.ANY)
    buf = pltpu.VMEM((2, heads, nbk, mem, dh), F32)
    return pl.pallas_call(
        _kv_kernel,
        grid=(depth, nb // nbk),
        in_specs=[pl.BlockSpec((nbk, mem, d), lambda l, j: (j, 0, 0)), w_spec, w_spec],
        out_specs=[any_spec, any_spec],
        out_shape=[jax.ShapeDtypeStruct((depth, nb, mem, heads, dh), F32)] * 2,
        scratch_shapes=[buf, buf, pltpu.SemaphoreType.DMA((2, 2, heads))],
        compiler_params=_params("arbitrary", "arbitrary"),
        name="kv_proj",
    )(mem3, wk, wv)


def _ffn_kernel(x_ref, g_ref, wg_ref, wu_ref, wd_ref, gf_ref, o_ref, *, chunks, final_norm):
    x = x_ref[...]
    h = _rms(x, g_ref[...]).astype(BF16)
    acc = x
    lo = 0
    for width in chunks:
        gate = _dot(h, wg_ref[:, lo:lo + width])
        up = _dot(h, wu_ref[:, lo:lo + width])
        act = (_silu(gate) * up).astype(BF16)
        acc = acc + _dot(act, wd_ref[lo:lo + width, :])
        lo += width
    if final_norm:
        acc = _rms(acc, gf_ref[...])
    o_ref[...] = acc


def _ffn(x2d, g, wg, wu, wd, gf, layer, tm, final_norm):
    rows, d = x2d.shape
    ff = wg.shape[-1]
    assert sum(_FF_CHUNKS) == ff and rows % tm == 0
    kern = functools.partial(_ffn_kernel, chunks=_FF_CHUNKS, final_norm=final_norm)
    row_spec = pl.BlockSpec((tm, d), lambda i: (i, 0))
    ops, specs = zip(_sel(g, layer), _sel(wg, layer), _sel(wu, layer), _sel(wd, layer), _sel(gf, 0))
    return pl.pallas_call(
        kern,
        grid=(rows // tm,),
        in_specs=[row_spec, *specs],
        out_specs=row_spec,
        out_shape=jax.ShapeDtypeStruct((rows, d), F32),
        compiler_params=_params("arbitrary"),
        name="ffn",
    )(x2d, *ops)


def _pattn_kv_copies(k_hbm, v_hbm, kvbuf, sem, layer, n, slot):
    heads = kvbuf.shape[2]
    cps = []
    for j, src in enumerate((k_hbm, v_hbm)):
        for hd in range(heads):
            cps.append(pltpu.make_async_copy(src.at[layer, n, :, hd, :], kvbuf.at[slot, j, hd], sem.at[slot, j, hd]))
    return cps


def _pattn_kernel(x_ref, g_ref, wq_ref, k_hbm, v_hbm, wo_ref, o_ref, kvbuf, kv16, sem, *, layer, parts):
    n = pl.program_id(0)
    i = pl.program_id(1)
    heads, _, dh = kv16.shape[1:]
    tm = x_ref.shape[0]
    rows = tm // parts
    slot = lax.rem(n, 2)
    copies = functools.partial(_pattn_kv_copies, k_hbm, v_hbm, kvbuf, sem, layer)

    @pl.when(i == 0)
    def _():
        @pl.when(n == 0)
        def _():
            for cp in copies(0, 0):
                cp.start()

        @pl.when(n + 1 < pl.num_programs(0))
        def _():
            for cp in copies(n + 1, 1 - slot):
                cp.start()

        for cp in copies(n, slot):
            cp.wait()
        kv16[...] = kvbuf[slot].astype(BF16)

    def attend(r0):
        x = x_ref[r0:r0 + rows, :]
        h = _rms(x, g_ref[...]).astype(BF16)
        yield
        q = _dot(h, wq_ref[...]).astype(BF16)
        yield
        outs = []
        for hd in range(heads):
            s = lax.dot_general(q[:, hd * dh:(hd + 1) * dh], kv16[0, hd], (((1,), (1,)), ((), ())),
                                preferred_element_type=F32)
            yield
            a = _softmax_rows(s * (dh ** -0.5)).astype(BF16)
            yield
            outs.append(_dot(a, kv16[1, hd]))
            yield
        o = jnp.concatenate(outs, axis=1).astype(BF16)
        o_ref[r0:r0 + rows, :] = x + _dot(o, wo_ref[...])
        yield

    _run_staggered([attend(p * rows) for p in range(parts)])


def _prompt_attn(x, g, wq, k5, v5, wo, layer, tm):
    nb, t, d = x.shape
    _, _, mem_len, heads, dh = k5.shape
    assert t % tm == 0 and tm % (_ATTN_PARTS * _SUBLANE) == 0
    x_spec = pl.BlockSpec((None, tm, d), lambda n, i: (n, i, 0))
    any_spec = pl.BlockSpec(memory_space=pl.ANY)
    (g, g_spec), (wq, wq_spec), (wo, wo_spec) = _sel(g, layer), _sel(wq, layer), _sel(wo, layer)
    return pl.pallas_call(
        functools.partial(_pattn_kernel, layer=layer, parts=_ATTN_PARTS),
        grid=(nb, t // tm),
        in_specs=[x_spec, g_spec, wq_spec, any_spec, any_spec, wo_spec],
        out_specs=x_spec,
        out_shape=jax.ShapeDtypeStruct((nb, t, d), F32),
        scratch_shapes=[pltpu.VMEM((2, 2, heads, mem_len, dh), F32), pltpu.VMEM((2, heads, mem_len, dh), BF16),
                        pltpu.SemaphoreType.DMA((2, 2, heads))],
        compiler_params=_params("arbitrary", "arbitrary"),
        name="prompt_attn",
    )(x, g, wq, k5, v5, wo)


def _sattn_copies(k_hbm, v_hbm, kbuf, vbuf, sem, layer, blk, slot):
    _, heads, bb = kbuf.shape[:3]
    rows = pl.ds(blk * bb, bb)
    cps = []
    for hd in range(heads):
        cps.append(pltpu.make_async_copy(k_hbm.at[layer, rows, :, hd, :], kbuf.at[slot, hd], sem.at[slot, 0, hd]))
        cps.append(pltpu.make_async_copy(v_hbm.at[layer, rows, :, hd, :], vbuf.at[slot, hd], sem.at[slot, 1, hd]))
    return cps


def _sattn_kernel(x_ref, g_ref, wq_ref, k_hbm, v_hbm, wo_ref, o_ref, q_scr, o_scr, kbuf, vbuf, sem, *, layer):
    i = pl.program_id(0)
    n = pl.num_programs(0)
    _, heads, bb, _, dh = kbuf.shape
    d = heads * dh
    slot = lax.rem(i, 2)
    copies = functools.partial(_sattn_copies, k_hbm, v_hbm, kbuf, vbuf, sem, layer)

    @pl.when(i == 0)
    def _():
        for cp in copies(0, 0):
            cp.start()
        h = _rms(x_ref[...], g_ref[...]).astype(BF16)
        q_scr[...] = _dot(h, wq_ref[...])

    @pl.when(i + 1 < n)
    def _():
        for cp in copies(i + 1, 1 - slot):
            cp.start()

    for cp in copies(i, slot):
        cp.wait()

    row0 = pl.multiple_of(i * bb, bb)
    q_blk = q_scr[pl.ds(row0, bb), :] * (dh ** -0.5)
    rows = lax.broadcasted_iota(jnp.int32, (bb, 1), 0)
    o_blk = jnp.zeros((bb, d), F32)
    for j in range(bb):
        outs = []
        for hd in range(heads):
            sl = slice(hd * dh, (hd + 1) * dh)
            kh = kbuf[slot, hd, j].astype(BF16)
            q_rep = jnp.broadcast_to(q_blk[j:j + 1, sl], (_LANE, dh)).astype(BF16)
            s = lax.dot_general(kh, q_rep, (((1,), (1,)), ((), ())), preferred_element_type=F32)
            e = jnp.exp(s - jnp.max(s, axis=0, keepdims=True))
            den = jnp.sum(e, axis=0, keepdims=True)
            reps = dh // _LANE
            num = jnp.sum(jnp.concatenate([e] * reps, axis=1) * vbuf[slot, hd, j], axis=0, keepdims=True)
            outs.append(num * (1.0 / jnp.concatenate([den] * reps, axis=1)))
        o_blk = jnp.where(rows == j, jnp.concatenate(outs, axis=1), o_blk)
    o_scr[pl.ds(row0, bb), :] = o_blk

    @pl.when(i == n - 1)
    def _():
        o_ref[...] = x_ref[...] + _dot(o_scr[...].astype(BF16), wo_ref[...])


def _sample_attn(x2d, g, wq, k5, v5, wo, layer, bb):
    b, d = x2d.shape
    _, _, mem_len, heads, dh = k5.shape
    assert bb == _SUBLANE and b % bb == 0
    any_spec = pl.BlockSpec(memory_space=pl.ANY)
    kv_buf = pltpu.VMEM((2, heads, bb, mem_len, dh), F32)
    (g, g_spec), (wq, wq_spec), (wo, wo_spec) = _sel(g, layer), _sel(wq, layer), _sel(wo, layer)
    return pl.pallas_call(
        functools.partial(_sattn_kernel, layer=layer),
        grid=(b // bb,),
        in_specs=[_const_spec((b, d)), g_spec, wq_spec, any_spec, any_spec, wo_spec],
        out_specs=pl.BlockSpec((b, d), lambda i: (0, 0)),
        out_shape=jax.ShapeDtypeStruct((b, d), F32),
        scratch_shapes=[pltpu.VMEM((b, d), F32), pltpu.VMEM((b, d), F32), kv_buf, kv_buf,
                        pltpu.SemaphoreType.DMA((2, 2, heads))],
        compiler_params=_params("arbitrary"),
        name="sample_attn",
    )(x2d, g, wq, k5, v5, wo)


def _peven_kernel(x_ref, g_ref, win_ref, cw_ref, cb_ref, lg_ref, lb_ref, pw_ref, ps_ref, wout_ref,
                  xo_ref, sa_ref, sb_ref, za_ext, zb_ext, ca_buf, *, tm, d_a, ha, hb):
    t = pl.program_id(1)
    ka = cw_ref.shape[0]
    nca = za_ext.shape[0]
    ncb = zb_ext.shape[0]
    pb = sb_ref.shape[0]

    @pl.when(t == 0)
    def _():
        za_ext[:, 0:ha, :] = jnp.zeros((nca, ha, _LANE), F32)
        zb_ext[:, 0:hb, :] = jnp.zeros((ncb, hb, _LANE), F32)

    x = x_ref[...]
    h = _rms(x, g_ref[...]).astype(BF16)
    p = _dot(h, win_ref[...])
    za = p[:, :d_a] * jax.nn.sigmoid(p[:, d_a:2 * d_a])
    zb = p[:, 2 * d_a:]
    for c in range(nca):
        za_ext[c, ha:ha + tm, :] = za[:, c * _LANE:(c + 1) * _LANE]
    for c in range(ncb):
        zb_ext[c, hb:hb + tm, :] = zb[:, c * _LANE:(c + 1) * _LANE]

    base = ha - (ka - 1)
    for c in range(nca):
        ls = slice(c * _LANE, (c + 1) * _LANE)
        for r in range(tm // _CONV_ROWS):
            r0 = r * _CONV_ROWS
            acc = jnp.broadcast_to(cb_ref[:, ls], (_CONV_ROWS, _LANE))
            for k in range(ka):
                acc = acc + cw_ref[k:k + 1, ls] * za_ext[c, base + r0 + k:base + r0 + k + _CONV_ROWS, :]
            ca_buf[r0:r0 + _CONV_ROWS, ls] = acc
    ya = _silu(_layernorm(ca_buf[...], lg_ref[...], lb_ref[...]))

    pos = t * tm + lax.broadcasted_iota(jnp.int32, (tm, 1), 0)
    parts = []
    for gi, w in enumerate(_POOL_WINDOWS):
        zg = zb[:, gi * _LANE:(gi + 1) * _LANE]
        s = zg
        for j in range(1, w):
            s = s + zb_ext[gi, hb - j:hb - j + tm, :]
        inv_cnt = 1.0 / jnp.minimum(pos + 1, w).astype(F32)
        pooled = (s * inv_cnt - zg).astype(BF16)
        parts.append(_dot(pooled, pw_ref[gi]))
    yb = jnp.concatenate(parts, axis=1) * ps_ref[...]

    y = jnp.concatenate([ya, yb], axis=1).astype(BF16)
    xo_ref[...] = x + _dot(y, wout_ref[...])

    @pl.when(t == pl.num_programs(1) - 1)
    def _():
        for c in range(nca):
            sa_ref[:, c * _LANE:(c + 1) * _LANE] = za_ext[c, ha + tm - (ka - 1):ha + tm, :]
        for c in range(ncb):
            sb_ref[:, c * _LANE:(c + 1) * _LANE] = zb_ext[c, hb + tm - pb:hb + tm, :]

    za_ext[:, 0:ha, :] = za_ext[:, tm:tm + ha, :]
    zb_ext[:, 0:hb, :] = zb_ext[:, tm:tm + hb, :]


def _prompt_even(x, params, tm):
    nb, t, d = x.shape
    ops, specs = zip(*params)
    ka, d_a = ops[2].shape[-2:]
    d_b = ops[7].shape[-1]
    pb = max(_POOL_WINDOWS) - 1
    ha = _round_up(ka - 1, _SUBLANE)
    hb = _round_up(pb, _SUBLANE)
    assert t % tm == 0 and tm % _CONV_ROWS == 0 and tm >= max(ha, hb)
    assert d_a % _LANE == 0 and d_b == _LANE * len(_POOL_WINDOWS)
    x_spec = pl.BlockSpec((None, tm, d), lambda n, i: (n, i, 0))
    kern = functools.partial(_peven_kernel, tm=tm, d_a=d_a, ha=ha, hb=hb)
    return pl.pallas_call(
        kern,
        grid=(nb, t // tm),
        in_specs=[x_spec, *specs],
        out_specs=[x_spec,
                   pl.BlockSpec((None, ka - 1, d_a), lambda n, i: (n, 0, 0)),
                   pl.BlockSpec((None, pb, d_b), lambda n, i: (n, 0, 0))],
        out_shape=[jax.ShapeDtypeStruct((nb, t, d), F32),
                   jax.ShapeDtypeStruct((nb, ka - 1, d_a), F32),
                   jax.ShapeDtypeStruct((nb, pb, d_b), F32)],
        scratch_shapes=[pltpu.VMEM((d_a // _LANE, ha + tm, _LANE), F32),
                        pltpu.VMEM((d_b // _LANE, hb + tm, _LANE), F32),
                        pltpu.VMEM((tm, d_a), F32)],
        compiler_params=_params("arbitrary", "arbitrary"),
        name="prompt_even",
    )(x, *ops)


def _seven_kernel(x_ref, g_ref, win_ref, cw_ref, cb_ref, lg_ref, lb_ref, pw_ref, ps_ref, wout_ref,
                  ha_ref, hb_ref, xo_ref, sa_ref, sb_ref):
    ka, d_a = cw_ref.shape
    pb = hb_ref.shape[0]
    d_b = hb_ref.shape[2]
    dg = d_b // len(_POOL_WINDOWS)
    x = x_ref[...]
    h = _rms(x, g_ref[...]).astype(BF16)
    p = _dot(h, win_ref[...])
    za = p[:, :d_a] * jax.nn.sigmoid(p[:, d_a:2 * d_a])
    zb = p[:, 2 * d_a:]

    acc = cb_ref[...] + cw_ref[ka - 1:ka, :] * za
    for k in range(ka - 1):
        acc = acc + cw_ref[k:k + 1, :] * ha_ref[k]
    ya = _silu(_layernorm(acc, lg_ref[...], lb_ref[...]))

    parts = []
    for gi, w in enumerate(_POOL_WINDOWS):
        ls = slice(gi * dg, (gi + 1) * dg)
        s = zb[:, ls]
        for j in range(1, w):
            s = s + hb_ref[pb - j, :, ls]
        inv_cnt = 1.0 / float(min(_PAST_LEN + 1, w))
        pooled = (s * inv_cnt - zb[:, ls]).astype(BF16)
        parts.append(_dot(pooled, pw_ref[gi]))
    yb = jnp.concatenate(parts, axis=1) * ps_ref[...]

    y = jnp.concatenate([ya, yb], axis=1).astype(BF16)
    xo_ref[...] = x + _dot(y, wout_ref[...])

    sa_ref[0:ka - 2] = ha_ref[1:ka - 1]
    sa_ref[ka - 2] = za
    sb_ref[0:pb - 1] = hb_ref[1:pb]
    sb_ref[pb - 1] = zb


def _sample_even(x2d, params, hist_a, hist_b):
    b, d = x2d.shape
    ops, specs = zip(*params)
    return pl.pallas_call(
        _seven_kernel,
        grid=(1,),
        in_specs=[_const_spec((b, d)), *specs, _const_spec(hist_a.shape), _const_spec(hist_b.shape)],
        out_specs=[_whole_spec((b, d)), _whole_spec(hist_a.shape), _whole_spec(hist_b.shape)],
        out_shape=[jax.ShapeDtypeStruct((b, d), F32),
                   jax.ShapeDtypeStruct(hist_a.shape, F32),
                   jax.ShapeDtypeStruct(hist_b.shape, F32)],
        compiler_params=_params("arbitrary"),
        name="sample_even",
    )(x2d, *ops, hist_a, hist_b)


def _podd_kernel(x_ref, g_ref, win_ref, lg_ref, lb_ref, ws_ref, bst_ref, cw_ref, wout_ref,
                 xo_ref, sv_ref, sd_ref, g_ext, *, tm, d_c, hd_rows):
    t = pl.program_id(1)
    kd, d_d = cw_ref.shape
    heads = ws_ref.shape[0]
    dh = d_c // heads
    ncd = g_ext.shape[0]

    @pl.when(t == 0)
    def _():
        g_ext[:, 0:hd_rows, :] = jnp.zeros((ncd, hd_rows, _LANE), F32)

    x = x_ref[...]
    h = _rms(x, g_ref[...]).astype(BF16)
    p = _dot(h, win_ref[...])
    u = p[:, :d_c]
    v = _layernorm(p[:, d_c:2 * d_c], lg_ref[...], lb_ref[...])
    o = 2 * d_c
    gb = p[:, o:o + d_d]
    gated = p[:, o + d_d:o + 2 * d_d] * p[:, o + 2 * d_d:]

    ri = lax.broadcasted_iota(jnp.int32, (_CHUNK, _CHUNK), 0)
    ci = lax.broadcasted_iota(jnp.int32, (_CHUNK, _CHUNK), 1)
    wm = [jnp.where(ci <= ri, ws_ref[hh], 0.0).astype(BF16) for hh in range(heads)]
    vb = v.astype(BF16)
    rows = []
    for c in range(tm // _CHUNK):
        rs = slice(c * _CHUNK, (c + 1) * _CHUNK)
        parts = [_dot(wm[hh], vb[rs, hh * dh:(hh + 1) * dh]) + bst_ref[:, hh:hh + 1] for hh in range(heads)]
        rows.append(jnp.concatenate(parts, axis=1))
    yc = u * jnp.concatenate(rows, axis=0)

    base = hd_rows - (kd - 1)
    cols = []
    for c in range(ncd):
        ls = slice(c * _LANE, (c + 1) * _LANE)
        g_ext[c, hd_rows:hd_rows + tm, :] = gated[:, ls]
        cd = cw_ref[0:1, ls] * g_ext[c, base:base + tm, :]
        for k in range(1, kd):
            cd = cd + cw_ref[k:k + 1, ls] * g_ext[c, base + k:base + k + tm, :]
        cols.append(cd)
    yd = gb * jnp.concatenate(cols, axis=1)

    y = jnp.concatenate([yc, yd], axis=1).astype(BF16)
    xo_ref[...] = x + _dot(y, wout_ref[...])

    @pl.when(t == pl.num_programs(1) - 1)
    def _():
        n_open = sv_ref.shape[0]
        sv_ref[...] = v[tm - n_open:, :]
        for c in range(ncd):
            sd_ref[:, c * _LANE:(c + 1) * _LANE] = g_ext[c, hd_rows + tm - (kd - 1):hd_rows + tm, :]

    g_ext[:, 0:hd_rows, :] = g_ext[:, tm:tm + hd_rows, :]


def _prompt_odd(x, params, tm):
    nb, t, d = x.shape
    ops, specs = zip(*params)
    d_c = ops[2].shape[-1]
    kd, d_d = ops[6].shape[-2:]
    n_open = t - ((t - 1) // _CHUNK) * _CHUNK
    hd_rows = _round_up(kd - 1, _SUBLANE)
    assert t % tm == 0 and tm % _CHUNK == 0 and n_open % _SUBLANE == 0 and n_open <= tm
    x_spec = pl.BlockSpec((None, tm, d), lambda n, i: (n, i, 0))
    kern = functools.partial(_podd_kernel, tm=tm, d_c=d_c, hd_rows=hd_rows)
    return pl.pallas_call(
        kern,
        grid=(nb, t // tm),
        in_specs=[x_spec, *specs],
        out_specs=[x_spec,
                   pl.BlockSpec((None, n_open, d_c), lambda n, i: (n, 0, 0)),
                   pl.BlockSpec((None, kd - 1, d_d), lambda n, i: (n, 0, 0))],
        out_shape=[jax.ShapeDtypeStruct((nb, t, d), F32),
                   jax.ShapeDtypeStruct((nb, n_open, d_c), F32),
                   jax.ShapeDtypeStruct((nb, kd - 1, d_d), F32)],
        scratch_shapes=[pltpu.VMEM((d_d // _LANE, hd_rows + tm, _LANE), F32)],
        compiler_params=_params("arbitrary", "arbitrary"),
        name="prompt_odd",
    )(x, *ops)


def _sodd_kernel(x_ref, g_ref, win_ref, lg_ref, lb_ref, ws0_ref, bs0_ref, cw_ref, wout_ref, hd_ref,
                 xo_ref, sv_ref, sd_ref):
    kd, d_d = cw_ref.shape
    d_c = lg_ref.shape[1]
    x = x_ref[...]
    h = _rms(x, g_ref[...]).astype(BF16)
    p = _dot(h, win_ref[...])
    u = p[:, :d_c]
    v = _layernorm(p[:, d_c:2 * d_c], lg_ref[...], lb_ref[...])
    o = 2 * d_c
    gb = p[:, o:o + d_d]
    gated = p[:, o + d_d:o + 2 * d_d] * p[:, o + 2 * d_d:]

    yc = u * (ws0_ref[...] * v + bs0_ref[...])

    cd = cw_ref[kd - 1:kd, :] * gated
    for k in range(kd - 1):
        cd = cd + cw_ref[k:k + 1, :] * hd_ref[:, k, :]
    yd = gb * cd

    y = jnp.concatenate([yc, yd], axis=1).astype(BF16)
    xo_ref[...] = x + _dot(y, wout_ref[...])

    sv_ref[...] = v
    for k in range(kd - 2):
        sd_ref[:, k, :] = hd_ref[:, k + 1, :]
    sd_ref[:, kd - 2, :] = gated


def _sample_odd(x2d, params, ws0, bs0, hist_d):
    b, d = x2d.shape
    ops, specs = zip(*params)
    d_c = ops[2].shape[-1]
    ops = ops[:4] + (ws0, bs0) + ops[6:]
    specs = specs[:4] + (_const_spec(ws0.shape), _const_spec(bs0.shape)) + specs[6:]
    return pl.pallas_call(
        _sodd_kernel,
        grid=(1,),
        in_specs=[_const_spec((b, d)), *specs, _const_spec(hist_d.shape)],
        out_specs=[_whole_spec((b, d)), _whole_spec((b, d_c)), _whole_spec(hist_d.shape)],
        out_shape=[jax.ShapeDtypeStruct((b, d), F32),
                   jax.ShapeDtypeStruct((b, d_c), F32),
                   jax.ShapeDtypeStruct(hist_d.shape, F32)],
        compiler_params=_params("arbitrary"),
        name="sample_odd",
    )(x2d, *ops, hist_d)


def kernel(x_prompt, x_sample, mem_prompt, state_convA, state_poolB, state_convD, cache_mem_k, cache_mem_v, norm_mix, norm_x, norm_ffn, norm_final, w_in_even, conv_a_w, conv_a_b, ln_a_g, ln_a_b, pool_b_w, pool_b_scale, w_out_even, w_in_odd, ln_c_g, ln_c_b, ws_c, bs_c, conv_d_w, w_out_odd, wq_x, wk_x, wv_x, wo_x, w_gate, w_up, w_down):
    nb, seq, d = x_prompt.shape
    db, dec_seq, _ = x_sample.shape
    assert dec_seq == 1
    depth = wq_x.shape[0]
    heads = cache_mem_k.shape[3]

    def bf(a):
        return a.astype(BF16)

    def unstack(parts):
        return parts[0][None] if len(parts) == 1 else jnp.stack(parts)

    wq16, wo16 = bf(wq_x), bf(wo_x)
    wg16, wu16, wd16 = bf(w_gate), bf(w_up), bf(w_down)
    win_e16, wout_e16, pw16 = bf(w_in_even), bf(w_out_even), bf(pool_b_w)
    win_o16, wout_o16 = bf(w_in_odd), bf(w_out_odd)
    g_fin = norm_final[None]

    p_k, p_v = _kv_proj(mem_prompt, bf(wk_x), bf(wv_x), heads)

    xp = x_prompt
    xs = x_sample.reshape(db, d)
    p_a, p_b, p_c, p_d = [], [], [], []
    s_a, s_b, s_c, s_d = [], [], [], []
    for layer in range(depth):
        i = layer // 2
        if layer % 2 == 0:
            params = [_sel(norm_mix, layer), _sel(win_e16, i), _sel(conv_a_w, i), _sel(conv_a_b, i),
                      _sel(ln_a_g, i), _sel(ln_a_b, i), _sel(pw16, i), _sel(pool_b_scale, i), _sel(wout_e16, i)]
            xp, a, b = _prompt_even(xp, params, tm=_PROMPT_TILE)
            p_a.append(a)
            p_b.append(b)
            xs, a, b = _sample_even(xs, params, state_convA[i].transpose(1, 0, 2), state_poolB[i].transpose(1, 0, 2))
            s_a.append(a.transpose(1, 0, 2))
            s_b.append(b.transpose(1, 0, 2))
        else:
            params = [_sel(norm_mix, layer), _sel(win_o16, i), _sel(ln_c_g, i), _sel(ln_c_b, i), _sel(ws_c, i),
                      _sel(bs_c.transpose(0, 2, 1), i), _sel(conv_d_w, i), _sel(wout_o16, i)]
            xp, c, dd = _prompt_odd(xp, params, tm=_PROMPT_TILE)
            dc = c.shape[-1]
            p_c.append(c.reshape(nb, -1, _C_HEADS, dc // _C_HEADS))
            p_d.append(dd)
            ws0 = jnp.repeat(ws_c[i, :, 0, 0], dc // _C_HEADS)[None]
            bs0 = jnp.repeat(bs_c[i, :, 0], dc // _C_HEADS)[None]
            xs, c, dd = _sample_odd(xs, params, ws0, bs0, state_convD[i])
            s_c.append(c.reshape(db, 1, _C_HEADS, dc // _C_HEADS))
            s_d.append(dd)

        xp = _prompt_attn(xp, norm_x, wq16, p_k, p_v, wo16, layer, tm=_ATTN_TILE)
        xs = _sample_attn(xs, norm_x, wq16, cache_mem_k, cache_mem_v, wo16, layer, bb=_SUBLANE)

        last = layer == depth - 1
        xp = _ffn(xp.reshape(nb * seq, d), norm_ffn, wg16, wu16, wd16, g_fin, layer, tm=_PROMPT_TILE,
                  final_norm=last).reshape(nb, seq, d)
        xs = _ffn(xs, norm_ffn, wg16, wu16, wd16, g_fin, layer, tm=db, final_norm=last)

    return (xp, xs.reshape(db, 1, d), unstack(p_a), unstack(p_b), unstack(p_c), unstack(p_d), p_k, p_v,
            unstack(s_a), unstack(s_b), unstack(s_c), unstack(s_d))
```

```python
import functools

import jax
import jax.numpy as jnp
from jax import lax
from jax.experimental import pallas as pl
from jax.experimental.pallas import tpu as pltpu

F32 = jnp.float32
BF16 = jnp.bfloat16

_EPS = 1e-6
_X_HEADS = 4
_C_HEADS = 4
_CHUNK = 128
_POOL_WINDOWS = (2, 4, 8, 16)
_PAST_LEN = 16384

_SUBLANE = 8
_LANE = 128
_VMEM_LIMIT = 56 * 1024 * 1024

_PROMPT_TILE = 512
_ATTN_TILE = 512
_ATTN_PARTS = 2
_KV_BATCHES = 2
_FF_CHUNKS = (768, 768, 768, 512)
_CONV_ROWS = 64


def _round_up(n, m):
    return -(-n // m) * m


def _params(*sem):
    return pltpu.CompilerParams(dimension_semantics=sem, vmem_limit_bytes=_VMEM_LIMIT)


def _const_spec(shape):
    nd = len(shape)
    return pl.BlockSpec(shape, lambda *_: (0,) * nd, pipeline_mode=pl.Buffered(1))


def _whole_spec(shape):
    nd = len(shape)
    return pl.BlockSpec(shape, lambda *_: (0,) * nd)


def _sel(stacked, idx):
    if stacked.ndim == 2:
        stacked = stacked[:, None, :]
    tail = stacked.shape[1:]
    spec = pl.BlockSpec((None,) + tail, lambda *_: (idx,) + (0,) * len(tail), pipeline_mode=pl.Buffered(1))
    return stacked, spec


def _dot(a, b):
    return jnp.dot(a, b, preferred_element_type=F32)


def _rms(x, g):
    y = x * lax.rsqrt(jnp.mean(x * x, axis=-1, keepdims=True) + _EPS)
    return y * g


def _layernorm(x, g, b):
    mu = jnp.mean(x, axis=-1, keepdims=True)
    d = x - mu
    var = jnp.mean(d * d, axis=-1, keepdims=True)
    return d * lax.rsqrt(var + _EPS) * g + b


def _silu(x):
    return x * jax.nn.sigmoid(x)


def _softmax_rows(s):
    m = jnp.max(s, axis=-1, keepdims=True)
    e = jnp.exp(s - m)
    return e * (1.0 / jnp.sum(e, axis=-1, keepdims=True))


_DONE = object()


def _run_staggered(seqs):
    live = []
    pending = list(seqs)
    while live or pending:
        if pending:
            live.append(pending.pop(0))
        live = [s for s in live if next(s, _DONE) is not _DONE]


def _kv_out_copies(kbuf, vbuf, k_hbm, v_hbm, sem, layer, blk, slot):
    _, heads, nbk = kbuf.shape[:3]
    rows = pl.ds(blk * nbk, nbk)
    cps = []
    for hd in range(heads):
        cps.append(pltpu.make_async_copy(kbuf.at[slot, hd], k_hbm.at[layer, rows, :, hd, :], sem.at[slot, 0, hd]))
        cps.append(pltpu.make_async_copy(vbuf.at[slot, hd], v_hbm.at[layer, rows, :, hd, :], sem.at[slot, 1, hd]))
    return cps


def _kv_kernel(m_ref, wk_ref, wv_ref, k_hbm, v_hbm, kbuf, vbuf, sem):
    nj = pl.num_programs(1)
    layer = pl.program_id(0)
    blk = pl.program_id(1)
    step = layer * nj + blk
    last = pl.num_programs(0) * nj - 1
    slot = lax.rem(step, 2)
    _, heads, nbk, mem, dh = kbuf.shape
    copies = functools.partial(_kv_out_copies, kbuf, vbuf, k_hbm, v_hbm, sem)

    @pl.when(step >= 2)
    def _():
        prev = step - 2
        for cp in copies(lax.div(prev, nj), lax.rem(prev, nj), slot):
            cp.wait()

    m = m_ref[...].reshape(nbk * mem, m_ref.shape[-1]).astype(BF16)
    k = _dot(m, wk_ref[...])
    v = _dot(m, wv_ref[...])
    for hd in range(heads):
        for b in range(nbk):
            kbuf[slot, hd, b] = k[b * mem:(b + 1) * mem, hd * dh:(hd + 1) * dh]
            vbuf[slot, hd, b] = v[b * mem:(b + 1) * mem, hd * dh:(hd + 1) * dh]
    for cp in copies(layer, blk, slot):
        cp.start()

    @pl.when(step == last)
    def _():
        @pl.when(step >= 1)
        def _():
            prev = step - 1
            for cp in copies(lax.div(prev, nj), lax.rem(prev, nj), 1 - slot):
                cp.wait()
        for cp in copies(layer, blk, slot):
            cp.wait()


def _kv_proj(mem3, wk, wv, heads):
    nb, mem, d = mem3.shape
    depth = wk.shape[0]
    dh = d // heads
    nbk = _KV_BATCHES
    assert nb % nbk == 0 and mem % _SUBLANE == 0
    w_spec = pl.BlockSpec((None, d, d), lambda l, j: (l, 0, 0))
    any_spec = pl.BlockSpec(memory_space=pl.ANY)
    buf = pltpu.VMEM((2, heads, nbk, mem, dh), F32)
    return pl.pallas_call(
        _kv_kernel,
        grid=(depth, nb // nbk),
        in_specs=[pl.BlockSpec((nbk, mem, d), lambda l, j: (j, 0, 0)), w_spec, w_spec],
        out_specs=[any_spec, any_spec],
        out_shape=[jax.ShapeDtypeStruct((depth, nb, mem, heads, dh), F32)] * 2,
        scratch_shapes=[buf, buf, pltpu.SemaphoreType.DMA((2, 2, heads))],
        compiler_params=_params("arbitrary", "arbitrary"),
        name="kv_proj",
    )(mem3, wk, wv)


def _ffn_kernel(xp_ref, xs_ref, g_ref, wg_ref, wu_ref, wd_ref, gf_ref, op_ref, os_ref, *, chunks, final_norm):
    def swiglu(x_ref, o_ref):
        x = x_ref[...]
        h = _rms(x, g_ref[...]).astype(BF16)
        acc = x
        lo = 0
        for width in chunks:
            gate = _dot(h, wg_ref[:, lo:lo + width])
            up = _dot(h, wu_ref[:, lo:lo + width])
            act = (_silu(gate) * up).astype(BF16)
            acc = acc + _dot(act, wd_ref[lo:lo + width, :])
            lo += width
        if final_norm:
            acc = _rms(acc, gf_ref[...])
        o_ref[...] = acc

    is_sample_step = pl.program_id(0) == pl.num_programs(0) - 1

    @pl.when(jnp.logical_not(is_sample_step))
    def _():
        swiglu(xp_ref, op_ref)

    @pl.when(is_sample_step)
    def _():
        swiglu(xs_ref, os_ref)


def _ffn(xp2d, xs2d, g, wg, wu, wd, gf, layer, tm, final_norm):
    rows, d = xp2d.shape
    b = xs2d.shape[0]
    ff = wg.shape[-1]
    assert sum(_FF_CHUNKS) == ff and rows % tm == 0
    n_tiles = rows // tm
    kern = functools.partial(_ffn_kernel, chunks=_FF_CHUNKS, final_norm=final_norm)
    row_spec = pl.BlockSpec((tm, d), lambda i: (jnp.minimum(i, n_tiles - 1), 0))
    ops, specs = zip(_sel(g, layer), _sel(wg, layer), _sel(wu, layer), _sel(wd, layer), _sel(gf, 0))
    return pl.pallas_call(
        kern,
        grid=(n_tiles + 1,),
        in_specs=[row_spec, _const_spec((b, d)), *specs],
        out_specs=[row_spec, _whole_spec((b, d))],
        out_shape=[jax.ShapeDtypeStruct((rows, d), F32), jax.ShapeDtypeStruct((b, d), F32)],
        compiler_params=_params("arbitrary"),
        name="ffn",
    )(xp2d, xs2d, *ops)


def _pattn_kv_copies(k_hbm, v_hbm, kvbuf, sem, layer, n, slot):
    heads = kvbuf.shape[2]
    cps = []
    for j, src in enumerate((k_hbm, v_hbm)):
        for hd in range(heads):
            cps.append(pltpu.make_async_copy(src.at[layer, n, :, hd, :], kvbuf.at[slot, j, hd], sem.at[slot, j, hd]))
    return cps


def _sattn_copies(k_hbm, v_hbm, cbuf, sem, layer, blk, slot):
    _, _, heads, bb = cbuf.shape[:4]
    rows = pl.ds(blk * bb, bb)
    cps = []
    for j, src in enumerate((k_hbm, v_hbm)):
        for hd in range(heads):
            cps.append(pltpu.make_async_copy(src.at[layer, rows, :, hd, :], cbuf.at[slot, j, hd], sem.at[slot, j, hd]))
    return cps


def _attn_kernel(xp_ref, xs_ref, g_ref, wq_ref, pk_hbm, pv_hbm, ck_hbm, cv_hbm, wo_ref, op_ref, os_ref,
                 kvbuf, kv16, psem, q_scr, o_scr, cbuf, csem, *, layer, parts):
    n = pl.program_id(0)
    i = pl.program_id(1)
    nt = pl.num_programs(1)
    step = n * nt + i
    last = pl.num_programs(0) * nt - 1
    heads, _, dh = kv16.shape[1:]
    bb = cbuf.shape[3]
    tm = xp_ref.shape[0]
    rows = tm // parts
    pslot = lax.rem(n, 2)
    cslot = lax.rem(step, 2)
    pcopies = functools.partial(_pattn_kv_copies, pk_hbm, pv_hbm, kvbuf, psem, layer)
    ccopies = functools.partial(_sattn_copies, ck_hbm, cv_hbm, cbuf, csem, layer)

    @pl.when(step == 0)
    def _():
        for cp in pcopies(0, 0) + ccopies(0, 0):
            cp.start()
        h = _rms(xs_ref[...], g_ref[...]).astype(BF16)
        q_scr[...] = _dot(h, wq_ref[...]) * (dh ** -0.5)

    @pl.when(step < last)
    def _():
        for cp in ccopies(step + 1, 1 - cslot):
            cp.start()

    @pl.when(i == 0)
    def _():
        @pl.when(n + 1 < pl.num_programs(0))
        def _():
            for cp in pcopies(n + 1, 1 - pslot):
                cp.start()

        for cp in pcopies(n, pslot):
            cp.wait()
        kv16[...] = kvbuf[pslot].astype(BF16)

    for cp in ccopies(step, cslot):
        cp.wait()

    def attend(r0):
        x = xp_ref[r0:r0 + rows, :]
        h = _rms(x, g_ref[...]).astype(BF16)
        yield
        q = _dot(h, wq_ref[...]).astype(BF16)
        yield
        outs = []
        for hd in range(heads):
            s = lax.dot_general(q[:, hd * dh:(hd + 1) * dh], kv16[0, hd], (((1,), (1,)), ((), ())),
                                preferred_element_type=F32)
            yield
            a = _softmax_rows(s * (dh ** -0.5)).astype(BF16)
            yield
            outs.append(_dot(a, kv16[1, hd]))
            yield
        o = jnp.concatenate(outs, axis=1).astype(BF16)
        op_ref[r0:r0 + rows, :] = x + _dot(o, wo_ref[...])
        yield

    def sample_rows():
        reps = dh // _LANE
        for j in range(bb):
            r = step * bb + j
            outs = []
            for hd in range(heads):
                q_row = q_scr[pl.ds(r, 1), hd * dh:(hd + 1) * dh]
                q_rep = jnp.broadcast_to(q_row, (_LANE, dh)).astype(BF16)
                kh = cbuf[cslot, 0, hd, j].astype(BF16)
                s = lax.dot_general(kh, q_rep, (((1,), (1,)), ((), ())), preferred_element_type=F32)
                e = jnp.exp(s - jnp.max(s, axis=0, keepdims=True))
                den = jnp.sum(e, axis=0, keepdims=True)
                num = jnp.sum(jnp.concatenate([e] * reps, axis=1) * cbuf[cslot, 1, hd, j], axis=0, keepdims=True)
                outs.append(num * (1.0 / jnp.concatenate([den] * reps, axis=1)))
                yield
            o_scr[pl.ds(r, 1), :] = jnp.concatenate(outs, axis=1)

    _run_staggered([attend(p * rows) for p in range(parts)] + [sample_rows()])

    @pl.when(step == last)
    def _():
        os_ref[...] = xs_ref[...] + _dot(o_scr[...].astype(BF16), wo_ref[...])


def _attn(xp, xs, g, wq, pk5, pv5, ck5, cv5, wo, layer, tm):
    nb, t, d = xp.shape
    b = xs.shape[0]
    _, _, mem_len, heads, dh = pk5.shape
    steps = nb * (t // tm)
    assert t % tm == 0 and tm % (_ATTN_PARTS * _SUBLANE) == 0 and b % steps == 0
    bb = b // steps
    x_spec = pl.BlockSpec((None, tm, d), lambda n, i: (n, i, 0))
    any_spec = pl.BlockSpec(memory_space=pl.ANY)
    (g, g_spec), (wq, wq_spec), (wo, wo_spec) = _sel(g, layer), _sel(wq, layer), _sel(wo, layer)
    return pl.pallas_call(
        functools.partial(_attn_kernel, layer=layer, parts=_ATTN_PARTS),
        grid=(nb, t // tm),
        in_specs=[x_spec, _const_spec((b, d)), g_spec, wq_spec, any_spec, any_spec, any_spec, any_spec, wo_spec],
        out_specs=[x_spec, _whole_spec((b, d))],
        out_shape=[jax.ShapeDtypeStruct((nb, t, d), F32), jax.ShapeDtypeStruct((b, d), F32)],
        scratch_shapes=[pltpu.VMEM((2, 2, heads, mem_len, dh), F32), pltpu.VMEM((2, heads, mem_len, dh), BF16),
                        pltpu.SemaphoreType.DMA((2, 2, heads)),
                        pltpu.VMEM((b, d), F32), pltpu.VMEM((b, d), F32),
                        pltpu.VMEM((2, 2, heads, bb, mem_len, dh), F32), pltpu.SemaphoreType.DMA((2, 2, heads))],
        compiler_params=_params("arbitrary", "arbitrary"),
        name="attn",
    )(xp, xs, g, wq, pk5, pv5, ck5, cv5, wo)


def _peven_kernel(x_ref, g_ref, win_ref, cw_ref, cb_ref, lg_ref, lb_ref, pw_ref, ps_ref, wout_ref,
                  xo_ref, sa_ref, sb_ref, za_ext, zb_ext, ca_buf, *, tm, d_a, ha, hb):
    t = pl.program_id(1)
    ka = cw_ref.shape[0]
    nca = za_ext.shape[0]
    ncb = zb_ext.shape[0]
    pb = sb_ref.shape[0]

    @pl.when(t == 0)
    def _():
        za_ext[:, 0:ha, :] = jnp.zeros((nca, ha, _LANE), F32)
        zb_ext[:, 0:hb, :] = jnp.zeros((ncb, hb, _LANE), F32)

    x = x_ref[...]
    h = _rms(x, g_ref[...]).astype(BF16)
    p = _dot(h, win_ref[...])
    za = p[:, :d_a] * jax.nn.sigmoid(p[:, d_a:2 * d_a])
    zb = p[:, 2 * d_a:]
    for c in range(nca):
        za_ext[c, ha:ha + tm, :] = za[:, c * _LANE:(c + 1) * _LANE]
    for c in range(ncb):
        zb_ext[c, hb:hb + tm, :] = zb[:, c * _LANE:(c + 1) * _LANE]

    base = ha - (ka - 1)
    for c in range(nca):
        ls = slice(c * _LANE, (c + 1) * _LANE)
        for r in range(tm // _CONV_ROWS):
            r0 = r * _CONV_ROWS
            acc = jnp.broadcast_to(cb_ref[:, ls], (_CONV_ROWS, _LANE))
            for k in range(ka):
                acc = acc + cw_ref[k:k + 1, ls] * za_ext[c, base + r0 + k:base + r0 + k + _CONV_ROWS, :]
            ca_buf[r0:r0 + _CONV_ROWS, ls] = acc
    ya = _silu(_layernorm(ca_buf[...], lg_ref[...], lb_ref[...]))

    pos = t * tm + lax.broadcasted_iota(jnp.int32, (tm, 1), 0)
    parts = []
    for gi, w in enumerate(_POOL_WINDOWS):
        zg = zb[:, gi * _LANE:(gi + 1) * _LANE]
        s = zg
        for j in range(1, w):
            s = s + zb_ext[gi, hb - j:hb - j + tm, :]
        inv_cnt = 1.0 / jnp.minimum(pos + 1, w).astype(F32)
        pooled = (s * inv_cnt - zg).astype(BF16)
        parts.append(_dot(pooled, pw_ref[gi]))
    yb = jnp.concatenate(parts, axis=1) * ps_ref[...]

    y = jnp.concatenate([ya, yb], axis=1).astype(BF16)
    xo_ref[...] = x + _dot(y, wout_ref[...])

    @pl.when(t == pl.num_programs(1) - 1)
    def _():
        for c in range(nca):
            sa_ref[:, c * _LANE:(c + 1) * _LANE] = za_ext[c, ha + tm - (ka - 1):ha + tm, :]
        for c in range(ncb):
            sb_ref[:, c * _LANE:(c + 1) * _LANE] = zb_ext[c, hb + tm - pb:hb + tm, :]

    za_ext[:, 0:ha, :] = za_ext[:, tm:tm + ha, :]
    zb_ext[:, 0:hb, :] = zb_ext[:, tm:tm + hb, :]


def _prompt_even(x, params, tm):
    nb, t, d = x.shape
    ops, specs = zip(*params)
    ka, d_a = ops[2].shape[-2:]
    d_b = ops[7].shape[-1]
    pb = max(_POOL_WINDOWS) - 1
    ha = _round_up(ka - 1, _SUBLANE)
    hb = _round_up(pb, _SUBLANE)
    assert t % tm == 0 and tm % _CONV_ROWS == 0 and tm >= max(ha, hb)
    assert d_a % _LANE == 0 and d_b == _LANE * len(_POOL_WINDOWS)
    x_spec = pl.BlockSpec((None, tm, d), lambda n, i: (n, i, 0))
    kern = functools.partial(_peven_kernel, tm=tm, d_a=d_a, ha=ha, hb=hb)
    return pl.pallas_call(
        kern,
        grid=(nb, t // tm),
        in_specs=[x_spec, *specs],
        out_specs=[x_spec,
                   pl.BlockSpec((None, ka - 1, d_a), lambda n, i: (n, 0, 0)),
                   pl.BlockSpec((None, pb, d_b), lambda n, i: (n, 0, 0))],
        out_shape=[jax.ShapeDtypeStruct((nb, t, d), F32),
                   jax.ShapeDtypeStruct((nb, ka - 1, d_a), F32),
                   jax.ShapeDtypeStruct((nb, pb, d_b), F32)],
        scratch_shapes=[pltpu.VMEM((d_a // _LANE, ha + tm, _LANE), F32),
                        pltpu.VMEM((d_b // _LANE, hb + tm, _LANE), F32),
                        pltpu.VMEM((tm, d_a), F32)],
        compiler_params=_params("arbitrary", "arbitrary"),
        name="prompt_even",
    )(x, *ops)


def _seven_kernel(x_ref, g_ref, win_ref, cw_ref, cb_ref, lg_ref, lb_ref, pw_ref, ps_ref, wout_ref,
                  ha_ref, hb_ref, xo_ref, sa_ref, sb_ref):
    ka, d_a = cw_ref.shape
    pb = hb_ref.shape[0]
    d_b = hb_ref.shape[2]
    dg = d_b // len(_POOL_WINDOWS)
    x = x_ref[...]
    h = _rms(x, g_ref[...]).astype(BF16)
    p = _dot(h, win_ref[...])
    za = p[:, :d_a] * jax.nn.sigmoid(p[:, d_a:2 * d_a])
    zb = p[:, 2 * d_a:]

    acc = cb_ref[...] + cw_ref[ka - 1:ka, :] * za
    for k in range(ka - 1):
        acc = acc + cw_ref[k:k + 1, :] * ha_ref[k]
    ya = _silu(_layernorm(acc, lg_ref[...], lb_ref[...]))

    parts = []
    for gi, w in enumerate(_POOL_WINDOWS):
        ls = slice(gi * dg, (gi + 1) * dg)
        s = zb[:, ls]
        for j in range(1, w):
            s = s + hb_ref[pb - j, :, ls]
        inv_cnt = 1.0 / float(min(_PAST_LEN + 1, w))
        pooled = (s * inv_cnt - zb[:, ls]).astype(BF16)
        parts.append(_dot(pooled, pw_ref[gi]))
    yb = jnp.concatenate(parts, axis=1) * ps_ref[...]

    y = jnp.concatenate([ya, yb], axis=1).astype(BF16)
    xo_ref[...] = x + _dot(y, wout_ref[...])

    sa_ref[0:ka - 2] = ha_ref[1:ka - 1]
    sa_ref[ka - 2] = za
    sb_ref[0:pb - 1] = hb_ref[1:pb]
    sb_ref[pb - 1] = zb


def _sample_even(x2d, params, hist_a, hist_b):
    b, d = x2d.shape
    ops, specs = zip(*params)
    return pl.pallas_call(
        _seven_kernel,
        grid=(1,),
        in_specs=[_const_spec((b, d)), *specs, _const_spec(hist_a.shape), _const_spec(hist_b.shape)],
        out_specs=[_whole_spec((b, d)), _whole_spec(hist_a.shape), _whole_spec(hist_b.shape)],
        out_shape=[jax.ShapeDtypeStruct((b, d), F32),
                   jax.ShapeDtypeStruct(hist_a.shape, F32),
                   jax.ShapeDtypeStruct(hist_b.shape, F32)],
        compiler_params=_params("arbitrary"),
        name="sample_even",
    )(x2d, *ops, hist_a, hist_b)


def _podd_kernel(x_ref, g_ref, win_ref, lg_ref, lb_ref, ws_ref, bst_ref, cw_ref, wout_ref,
                 xo_ref, sv_ref, sd_ref, g_ext, *, tm, d_c, hd_rows):
    t = pl.program_id(1)
    kd, d_d = cw_ref.shape
    heads = ws_ref.shape[0]
    dh = d_c // heads
    ncd = g_ext.shape[0]

    @pl.when(t == 0)
    def _():
        g_ext[:, 0:hd_rows, :] = jnp.zeros((ncd, hd_rows, _LANE), F32)

    x = x_ref[...]
    h = _rms(x, g_ref[...]).astype(BF16)
    p = _dot(h, win_ref[...])
    u = p[:, :d_c]
    v = _layernorm(p[:, d_c:2 * d_c], lg_ref[...], lb_ref[...])
    o = 2 * d_c
    gb = p[:, o:o + d_d]
    gated = p[:, o + d_d:o + 2 * d_d] * p[:, o + 2 * d_d:]

    ri = lax.broadcasted_iota(jnp.int32, (_CHUNK, _CHUNK), 0)
    ci = lax.broadcasted_iota(jnp.int32, (_CHUNK, _CHUNK), 1)
    wm = [jnp.where(ci <= ri, ws_ref[hh], 0.0).astype(BF16) for hh in range(heads)]
    vb = v.astype(BF16)
    rows = []
    for c in range(tm // _CHUNK):
        rs = slice(c * _CHUNK, (c + 1) * _CHUNK)
        parts = [_dot(wm[hh], vb[rs, hh * dh:(hh + 1) * dh]) + bst_ref[:, hh:hh + 1] for hh in range(heads)]
        rows.append(jnp.concatenate(parts, axis=1))
    yc = u * jnp.concatenate(rows, axis=0)

    base = hd_rows - (kd - 1)
    cols = []
    for c in range(ncd):
        ls = slice(c * _LANE, (c + 1) * _LANE)
        g_ext[c, hd_rows:hd_rows + tm, :] = gated[:, ls]
        cd = cw_ref[0:1, ls] * g_ext[c, base:base + tm, :]
        for k in range(1, kd):
            cd = cd + cw_ref[k:k + 1, ls] * g_ext[c, base + k:base + k + tm, :]
        cols.append(cd)
    yd = gb * jnp.concatenate(cols, axis=1)

    y = jnp.concatenate([yc, yd], axis=1).astype(BF16)
    xo_ref[...] = x + _dot(y, wout_ref[...])

    @pl.when(t == pl.num_programs(1) - 1)
    def _():
        n_open = sv_ref.shape[0]
        sv_ref[...] = v[tm - n_open:, :]
        for c in range(ncd):
            sd_ref[:, c * _LANE:(c + 1) * _LANE] = g_ext[c, hd_rows + tm - (kd - 1):hd_rows + tm, :]

    g_ext[:, 0:hd_rows, :] = g_ext[:, tm:tm + hd_rows, :]


def _prompt_odd(x, params, tm):
    nb, t, d = x.shape
    ops, specs = zip(*params)
    d_c = ops[2].shape[-1]
    kd, d_d = ops[6].shape[-2:]
    n_open = t - ((t - 1) // _CHUNK) * _CHUNK
    hd_rows = _round_up(kd - 1, _SUBLANE)
    assert t % tm == 0 and tm % _CHUNK == 0 and n_open % _SUBLANE == 0 and n_open <= tm
    x_spec = pl.BlockSpec((None, tm, d), lambda n, i: (n, i, 0))
    kern = functools.partial(_podd_kernel, tm=tm, d_c=d_c, hd_rows=hd_rows)
    return pl.pallas_call(
        kern,
        grid=(nb, t // tm),
        in_specs=[x_spec, *specs],
        out_specs=[x_spec,
                   pl.BlockSpec((None, n_open, d_c), lambda n, i: (n, 0, 0)),
                   pl.BlockSpec((None, kd - 1, d_d), lambda n, i: (n, 0, 0))],
        out_shape=[jax.ShapeDtypeStruct((nb, t, d), F32),
                   jax.ShapeDtypeStruct((nb, n_open, d_c), F32),
                   jax.ShapeDtypeStruct((nb, kd - 1, d_d), F32)],
        scratch_shapes=[pltpu.VMEM((d_d // _LANE, hd_rows + tm, _LANE), F32)],
        compiler_params=_params("arbitrary", "arbitrary"),
        name="prompt_odd",
    )(x, *ops)


def _sodd_kernel(x_ref, g_ref, win_ref, lg_ref, lb_ref, ws0_ref, bs0_ref, cw_ref, wout_ref, hd_ref,
                 xo_ref, sv_ref, sd_ref):
    kd, d_d = cw_ref.shape
    d_c = lg_ref.shape[1]
    x = x_ref[...]
    h = _rms(x, g_ref[...]).astype(BF16)
    p = _dot(h, win_ref[...])
    u = p[:, :d_c]
    v = _layernorm(p[:, d_c:2 * d_c], lg_ref[...], lb_ref[...])
    o = 2 * d_c
    gb = p[:, o:o + d_d]
    gated = p[:, o + d_d:o + 2 * d_d] * p[:, o + 2 * d_d:]

    yc = u * (ws0_ref[...] * v + bs0_ref[...])

    cd = cw_ref[kd - 1:kd, :] * gated
    for k in range(kd - 1):
        cd = cd + cw_ref[k:k + 1, :] * hd_ref[:, k, :]
    yd = gb * cd

    y = jnp.concatenate([yc, yd], axis=1).astype(BF16)
    xo_ref[...] = x + _dot(y, wout_ref[...])

    sv_ref[...] = v
    for k in range(kd - 2):
        sd_ref[:, k, :] = hd_ref[:, k + 1, :]
    sd_ref[:, kd - 2, :] = gated


def _sample_odd(x2d, params, ws0, bs0, hist_d):
    b, d = x2d.shape
    ops, specs = zip(*params)
    d_c = ops[2].shape[-1]
    ops = ops[:4] + (ws0, bs0) + ops[6:]
    specs = specs[:4] + (_const_spec(ws0.shape), _const_spec(bs0.shape)) + specs[6:]
    return pl.pallas_call(
        _sodd_kernel,
        grid=(1,),
        in_specs=[_const_spec((b, d)), *specs, _const_spec(hist_d.shape)],
        out_specs=[_whole_spec((b, d)), _whole_spec((b, d_c)), _whole_spec(hist_d.shape)],
        out_shape=[jax.ShapeDtypeStruct((b, d), F32),
                   jax.ShapeDtypeStruct((b, d_c), F32),
                   jax.ShapeDtypeStruct(hist_d.shape, F32)],
        compiler_params=_params("arbitrary"),
        name="sample_odd",
    )(x2d, *ops, hist_d)


def kernel(x_prompt, x_sample, mem_prompt, state_convA, state_poolB, state_convD, cache_mem_k, cache_mem_v, norm_mix, norm_x, norm_ffn, norm_final, w_in_even, conv_a_w, conv_a_b, ln_a_g, ln_a_b, pool_b_w, pool_b_scale, w_out_even, w_in_odd, ln_c_g, ln_c_b, ws_c, bs_c, conv_d_w, w_out_odd, wq_x, wk_x, wv_x, wo_x, w_gate, w_up, w_down):
    nb, seq, d = x_prompt.shape
    db, dec_seq, _ = x_sample.shape
    assert dec_seq == 1
    depth = wq_x.shape[0]
    heads = cache_mem_k.shape[3]

    def bf(a):
        return a.astype(BF16)

    def unstack(parts):
        return parts[0][None] if len(parts) == 1 else jnp.stack(parts)

    wq16, wo16 = bf(wq_x), bf(wo_x)
    wg16, wu16, wd16 = bf(w_gate), bf(w_up), bf(w_down)
    win_e16, wout_e16, pw16 = bf(w_in_even), bf(w_out_even), bf(pool_b_w)
    win_o16, wout_o16 = bf(w_in_odd), bf(w_out_odd)
    g_fin = norm_final[None]

    p_k, p_v = _kv_proj(mem_prompt, bf(wk_x), bf(wv_x), heads)

    xp = x_prompt
    xs = x_sample.reshape(db, d)
    p_a, p_b, p_c, p_d = [], [], [], []
    s_a, s_b, s_c, s_d = [], [], [], []
    for layer in range(depth):
        i = layer // 2
        if layer % 2 == 0:
            params = [_sel(norm_mix, layer), _sel(win_e16, i), _sel(conv_a_w, i), _sel(conv_a_b, i),
                      _sel(ln_a_g, i), _sel(ln_a_b, i), _sel(pw16, i), _sel(pool_b_scale, i), _sel(wout_e16, i)]
            xp, a, b = _prompt_even(xp, params, tm=_PROMPT_TILE)
            p_a.append(a)
            p_b.append(b)
            xs, a, b = _sample_even(xs, params, state_convA[i].transpose(1, 0, 2), state_poolB[i].transpose(1, 0, 2))
            s_a.append(a.transpose(1, 0, 2))
            s_b.append(b.transpose(1, 0, 2))
        else:
            params = [_sel(norm_mix, layer), _sel(win_o16, i), _sel(ln_c_g, i), _sel(ln_c_b, i), _sel(ws_c, i),
                      _sel(bs_c.transpose(0, 2, 1), i), _sel(conv_d_w, i), _sel(wout_o16, i)]
            xp, c, dd = _prompt_odd(xp, params, tm=_PROMPT_TILE)
            dc = c.shape[-1]
            p_c.append(c.reshape(nb, -1, _C_HEADS, dc // _C_HEADS))
            p_d.append(dd)
            ws0 = jnp.repeat(ws_c[i, :, 0, 0], dc // _C_HEADS)[None]
            bs0 = jnp.repeat(bs_c[i, :, 0], dc // _C_HEADS)[None]
            xs, c, dd = _sample_odd(xs, params, ws0, bs0, state_convD[i])
            s_c.append(c.reshape(db, 1, _C_HEADS, dc // _C_HEADS))
            s_d.append(dd)

        xp, xs = _attn(xp, xs, norm_x, wq16, p_k, p_v, cache_mem_k, cache_mem_v, wo16, layer, tm=_ATTN_TILE)

        xp2d, xs = _ffn(xp.reshape(nb * seq, d), xs, norm_ffn, wg16, wu16, wd16, g_fin, layer, tm=_PROMPT_TILE,
                        final_norm=layer == depth - 1)
        xp = xp2d.reshape(nb, seq, d)

    return (xp, xs.reshape(db, 1, d), unstack(p_a), unstack(p_b), unstack(p_c), unstack(p_d), p_k, p_v,
            unstack(s_a), unstack(s_b), unstack(s_c), unstack(s_d))
```

```python
import functools

import jax
import jax.numpy as jnp
from jax import lax
from jax.experimental import pallas as pl
from jax.experimental.pallas import tpu as pltpu

F32 = jnp.float32
BF16 = jnp.bfloat16

_EPS = 1e-6
_X_HEADS = 4
_C_HEADS = 4
_CHUNK = 128
_POOL_WINDOWS = (2, 4, 8, 16)
_PAST_LEN = 16384

_SUBLANE = 8
_LANE = 128
_VMEM_LIMIT = 56 * 1024 * 1024
_STAGE_BYTES = 3 * 512 * 1024

_PROMPT_TILE = 1024
_ATTN_TILE = 512
_ATTN_PARTS = 2
_KV_BATCHES = 2
_FF_CHUNKS = (768, 768, 768, 512)
_CONV_ROWS = 64


def _round_up(n, m):
    return -(-n // m) * m


def _params(*sem):
    return pltpu.CompilerParams(dimension_semantics=sem, vmem_limit_bytes=_VMEM_LIMIT)


def _const_spec(shape):
    nd = len(shape)
    return pl.BlockSpec(shape, lambda *_: (0,) * nd, pipeline_mode=pl.Buffered(1))


def _whole_spec(shape):
    nd = len(shape)
    return pl.BlockSpec(shape, lambda *_: (0,) * nd)


def _sel(stacked, idx):
    if stacked.ndim == 2:
        stacked = stacked[:, None, :]
    tail = stacked.shape[1:]
    spec = pl.BlockSpec((None,) + tail, lambda *_: (idx,) + (0,) * len(tail), pipeline_mode=pl.Buffered(1))
    return stacked, spec


def _chunk_rows(rows, cols):
    cands = [r for r in range(2 * _SUBLANE, rows + 1, 2 * _SUBLANE)
             if rows % r == 0 and r * cols * 4 <= _STAGE_BYTES]
    return max(cands)


def _weight_scratch(rows, cols):
    return [pltpu.VMEM((rows, cols), BF16), pltpu.VMEM((2, _chunk_rows(rows, cols), cols), F32),
            pltpu.SemaphoreType.DMA((2,))]


def _fetch_bf16(w_hbm, idx, dst, stage, sem):
    rows = dst.shape[0]
    rc = stage.shape[1]

    def chunk(c):
        return pltpu.make_async_copy(w_hbm.at[idx, pl.ds(c * rc, rc), :], stage.at[c % 2], sem.at[c % 2])

    n = rows // rc
    chunk(0).start()
    for c in range(n):
        if c + 1 < n:
            chunk(c + 1).start()
        chunk(c).wait()
        dst[c * rc:(c + 1) * rc, :] = stage[c % 2].astype(BF16)


def _dot(a, b):
    return jnp.dot(a, b, preferred_element_type=F32)


def _rms(x, g):
    y = x * lax.rsqrt(jnp.mean(x * x, axis=-1, keepdims=True) + _EPS)
    return y * g


def _layernorm(x, g, b):
    mu = jnp.mean(x, axis=-1, keepdims=True)
    d = x - mu
    var = jnp.mean(d * d, axis=-1, keepdims=True)
    return d * lax.rsqrt(var + _EPS) * g + b


def _silu(x):
    return x * jax.nn.sigmoid(x)


def _softmax_rows(s):
    m = jnp.max(s, axis=-1, keepdims=True)
    e = jnp.exp(s - m)
    return e * (1.0 / jnp.sum(e, axis=-1, keepdims=True))


_DONE = object()


def _run_staggered(seqs):
    live = []
    pending = list(seqs)
    while live or pending:
        if pending:
            live.append(pending.pop(0))
        live = [s for s in live if next(s, _DONE) is not _DONE]


def _kv_out_copies(kbuf, vbuf, k_hbm, v_hbm, sem, layer, blk, slot):
    _, heads, nbk = kbuf.shape[:3]
    rows = pl.ds(blk * nbk, nbk)
    cps = []
    for hd in range(heads):
        cps.append(pltpu.make_async_copy(kbuf.at[slot, hd], k_hbm.at[layer, rows, :, hd, :], sem.at[slot, 0, hd]))
        cps.append(pltpu.make_async_copy(vbuf.at[slot, hd], v_hbm.at[layer, rows, :, hd, :], sem.at[slot, 1, hd]))
    return cps


def _kv_kernel(m_ref, wk_hbm, wv_hbm, k_hbm, v_hbm, kbuf, vbuf, sem, wk_ref, wv_ref, stage, wsem):
    nj = pl.num_programs(1)
    layer = pl.program_id(0)
    blk = pl.program_id(1)
    step = layer * nj + blk
    last = pl.num_programs(0) * nj - 1
    slot = lax.rem(step, 2)
    _, heads, nbk, mem, dh = kbuf.shape
    copies = functools.partial(_kv_out_copies, kbuf, vbuf, k_hbm, v_hbm, sem)

    @pl.when(blk == 0)
    def _():
        _fetch_bf16(wk_hbm, layer, wk_ref, stage, wsem)
        _fetch_bf16(wv_hbm, layer, wv_ref, stage, wsem)

    @pl.when(step >= 2)
    def _():
        prev = step - 2
        for cp in copies(lax.div(prev, nj), lax.rem(prev, nj), slot):
            cp.wait()

    m = m_ref[...].reshape(nbk * mem, m_ref.shape[-1]).astype(BF16)
    k = _dot(m, wk_ref[...])
    v = _dot(m, wv_ref[...])
    for hd in range(heads):
        for b in range(nbk):
            kbuf[slot, hd, b] = k[b * mem:(b + 1) * mem, hd * dh:(hd + 1) * dh]
            vbuf[slot, hd, b] = v[b * mem:(b + 1) * mem, hd * dh:(hd + 1) * dh]
    for cp in copies(layer, blk, slot):
        cp.start()

    @pl.when(step == last)
    def _():
        @pl.when(step >= 1)
        def _():
            prev = step - 1
            for cp in copies(lax.div(prev, nj), lax.rem(prev, nj), 1 - slot):
                cp.wait()
        for cp in copies(layer, blk, slot):
            cp.wait()


def _kv_proj(mem3, wk, wv, heads):
    nb, mem, d = mem3.shape
    depth = wk.shape[0]
    dh = d // heads
    nbk = _KV_BATCHES
    assert nb % nbk == 0 and mem % _SUBLANE == 0
    any_spec = pl.BlockSpec(memory_space=pl.ANY)
    buf = pltpu.VMEM((2, heads, nbk, mem, dh), F32)
    wk_scr, stage, wsem = _weight_scratch(d, d)
    return pl.pallas_call(
        _kv_kernel,
        grid=(depth, nb // nbk),
        in_specs=[pl.BlockSpec((nbk, mem, d), lambda l, j: (j, 0, 0)), any_spec, any_spec],
        out_specs=[any_spec, any_spec],
        out_shape=[jax.ShapeDtypeStruct((depth, nb, mem, heads, dh), F32)] * 2,
        scratch_shapes=[buf, buf, pltpu.SemaphoreType.DMA((2, 2, heads)), wk_scr, wk_scr, stage, wsem],
        compiler_params=_params("arbitrary", "arbitrary"),
        name="kv_proj",
    )(mem3, wk, wv)


def _ffn_kernel(xp_ref, xs_ref, g_ref, gf_ref, wg_hbm, wu_hbm, wd_hbm, op_ref, os_ref,
                wg_ref, wu_ref, up_stage, up_sem, wd_ref, dn_stage, dn_sem, *, layer, chunks, final_norm):
    @pl.when(pl.program_id(0) == 0)
    def _():
        _fetch_bf16(wg_hbm, layer, wg_ref, up_stage, up_sem)
        _fetch_bf16(wu_hbm, layer, wu_ref, up_stage, up_sem)
        _fetch_bf16(wd_hbm, layer, wd_ref, dn_stage, dn_sem)

    def swiglu(x_ref, o_ref):
        x = x_ref[...]
        h = _rms(x, g_ref[...]).astype(BF16)
        acc = x
        lo = 0
        for width in chunks:
            gate = _dot(h, wg_ref[:, lo:lo + width])
            up = _dot(h, wu_ref[:, lo:lo + width])
            act = (_silu(gate) * up).astype(BF16)
            acc = acc + _dot(act, wd_ref[lo:lo + width, :])
            lo += width
        if final_norm:
            acc = _rms(acc, gf_ref[...])
        o_ref[...] = acc

    is_sample_step = pl.program_id(0) == pl.num_programs(0) - 1

    @pl.when(jnp.logical_not(is_sample_step))
    def _():
        swiglu(xp_ref, op_ref)

    @pl.when(is_sample_step)
    def _():
        swiglu(xs_ref, os_ref)


def _ffn(xp2d, xs2d, g, wg, wu, wd, gf, layer, tm, final_norm):
    rows, d = xp2d.shape
    b = xs2d.shape[0]
    ff = wg.shape[-1]
    assert sum(_FF_CHUNKS) == ff and rows % tm == 0
    n_tiles = rows // tm
    kern = functools.partial(_ffn_kernel, layer=layer, chunks=_FF_CHUNKS, final_norm=final_norm)
    row_spec = pl.BlockSpec((tm, d), lambda i: (jnp.minimum(i, n_tiles - 1), 0))
    any_spec = pl.BlockSpec(memory_space=pl.ANY)
    ops, specs = zip(_sel(g, layer), _sel(gf, 0))
    wg_scr, up_stage, up_sem = _weight_scratch(d, ff)
    return pl.pallas_call(
        kern,
        grid=(n_tiles + 1,),
        in_specs=[row_spec, _const_spec((b, d)), *specs, any_spec, any_spec, any_spec],
        out_specs=[row_spec, _whole_spec((b, d))],
        out_shape=[jax.ShapeDtypeStruct((rows, d), F32), jax.ShapeDtypeStruct((b, d), F32)],
        scratch_shapes=[wg_scr, wg_scr, up_stage, up_sem, *_weight_scratch(ff, d)],
        compiler_params=_params("arbitrary"),
        name="ffn",
    )(xp2d, xs2d, *ops, wg, wu, wd)


def _pattn_kv_copies(k_hbm, v_hbm, kvbuf, sem, layer, n, slot):
    heads = kvbuf.shape[2]
    cps = []
    for j, src in enumerate((k_hbm, v_hbm)):
        for hd in range(heads):
            cps.append(pltpu.make_async_copy(src.at[layer, n, :, hd, :], kvbuf.at[slot, j, hd], sem.at[slot, j, hd]))
    return cps


def _sattn_copies(k_hbm, v_hbm, cbuf, sem, layer, blk, slot):
    _, _, heads, bb = cbuf.shape[:4]
    rows = pl.ds(blk * bb, bb)
    cps = []
    for j, src in enumerate((k_hbm, v_hbm)):
        for hd in range(heads):
            cps.append(pltpu.make_async_copy(src.at[layer, rows, :, hd, :], cbuf.at[slot, j, hd], sem.at[slot, j, hd]))
    return cps


def _attn_kernel(xp_ref, xs_ref, g_ref, pk_hbm, pv_hbm, ck_hbm, cv_hbm, wq_hbm, wo_hbm, op_ref, os_ref,
                 kvbuf, kv16, psem, q_scr, o_scr, cbuf, csem, wq_ref, wo_ref, stage, wsem, *, layer, parts):
    n = pl.program_id(0)
    i = pl.program_id(1)
    nt = pl.num_programs(1)
    step = n * nt + i
    last = pl.num_programs(0) * nt - 1
    heads, _, dh = kv16.shape[1:]
    bb = cbuf.shape[3]
    tm = xp_ref.shape[0]
    rows = tm // parts
    pslot = lax.rem(n, 2)
    cslot = lax.rem(step, 2)
    pcopies = functools.partial(_pattn_kv_copies, pk_hbm, pv_hbm, kvbuf, psem, layer)
    ccopies = functools.partial(_sattn_copies, ck_hbm, cv_hbm, cbuf, csem, layer)

    @pl.when(step == 0)
    def _():
        for cp in pcopies(0, 0) + ccopies(0, 0):
            cp.start()
        _fetch_bf16(wq_hbm, layer, wq_ref, stage, wsem)
        _fetch_bf16(wo_hbm, layer, wo_ref, stage, wsem)
        h = _rms(xs_ref[...], g_ref[...]).astype(BF16)
        q_scr[...] = _dot(h, wq_ref[...]) * (dh ** -0.5)

    @pl.when(step < last)
    def _():
        for cp in ccopies(step + 1, 1 - cslot):
            cp.start()

    @pl.when(i == 0)
    def _():
        @pl.when(n + 1 < pl.num_programs(0))
        def _():
            for cp in pcopies(n + 1, 1 - pslot):
                cp.start()

        for cp in pcopies(n, pslot):
            cp.wait()
        kv16[...] = kvbuf[pslot].astype(BF16)

    for cp in ccopies(step, cslot):
        cp.wait()

    def attend(r0):
        x = xp_ref[r0:r0 + rows, :]
        h = _rms(x, g_ref[...]).astype(BF16)
        yield
        q = _dot(h, wq_ref[...]).astype(BF16)
        yield
        outs = []
        for hd in range(heads):
            s = lax.dot_general(q[:, hd * dh:(hd + 1) * dh], kv16[0, hd], (((1,), (1,)), ((), ())),
                                preferred_element_type=F32)
            yield
            a = _softmax_rows(s * (dh ** -0.5)).astype(BF16)
            yield
            outs.append(_dot(a, kv16[1, hd]))
            yield
        o = jnp.concatenate(outs, axis=1).astype(BF16)
        op_ref[r0:r0 + rows, :] = x + _dot(o, wo_ref[...])
        yield

    def sample_rows():
        reps = dh // _LANE
        for j in range(bb):
            r = step * bb + j
            outs = []
            for hd in range(heads):
                q_row = q_scr[pl.ds(r, 1), hd * dh:(hd + 1) * dh]
                q_rep = jnp.broadcast_to(q_row, (_LANE, dh)).astype(BF16)
                kh = cbuf[cslot, 0, hd, j].astype(BF16)
                s = lax.dot_general(kh, q_rep, (((1,), (1,)), ((), ())), preferred_element_type=F32)
                e = jnp.exp(s - jnp.max(s, axis=0, keepdims=True))
                den = jnp.sum(e, axis=0, keepdims=True)
                num = jnp.sum(jnp.concatenate([e] * reps, axis=1) * cbuf[cslot, 1, hd, j], axis=0, keepdims=True)
                outs.append(num * (1.0 / jnp.concatenate([den] * reps, axis=1)))
                yield
            o_scr[pl.ds(r, 1), :] = jnp.concatenate(outs, axis=1)

    _run_staggered([attend(p * rows) for p in range(parts)] + [sample_rows()])

    @pl.when(step == last)
    def _():
        os_ref[...] = xs_ref[...] + _dot(o_scr[...].astype(BF16), wo_ref[...])


def _attn(xp, xs, g, wq, pk5, pv5, ck5, cv5, wo, layer, tm):
    nb, t, d = xp.shape
    b = xs.shape[0]
    _, _, mem_len, heads, dh = pk5.shape
    steps = nb * (t // tm)
    assert t % tm == 0 and tm % (_ATTN_PARTS * _SUBLANE) == 0 and b % steps == 0
    bb = b // steps
    x_spec = pl.BlockSpec((None, tm, d), lambda n, i: (n, i, 0))
    any_spec = pl.BlockSpec(memory_space=pl.ANY)
    g, g_spec = _sel(g, layer)
    wq_scr, stage, wsem = _weight_scratch(d, d)
    return pl.pallas_call(
        functools.partial(_attn_kernel, layer=layer, parts=_ATTN_PARTS),
        grid=(nb, t // tm),
        in_specs=[x_spec, _const_spec((b, d)), g_spec] + [any_spec] * 6,
        out_specs=[x_spec, _whole_spec((b, d))],
        out_shape=[jax.ShapeDtypeStruct((nb, t, d), F32), jax.ShapeDtypeStruct((b, d), F32)],
        scratch_shapes=[pltpu.VMEM((2, 2, heads, mem_len, dh), F32), pltpu.VMEM((2, heads, mem_len, dh), BF16),
                        pltpu.SemaphoreType.DMA((2, 2, heads)),
                        pltpu.VMEM((b, d), F32), pltpu.VMEM((b, d), F32),
                        pltpu.VMEM((2, 2, heads, bb, mem_len, dh), F32), pltpu.SemaphoreType.DMA((2, 2, heads)),
                        wq_scr, wq_scr, stage, wsem],
        compiler_params=_params("arbitrary", "arbitrary"),
        name="attn",
    )(xp, xs, g, pk5, pv5, ck5, cv5, wq, wo)


def _peven_kernel(x_ref, g_ref, cw_ref, cb_ref, lg_ref, lb_ref, ps_ref, win_hbm, wout_hbm, pw_hbm,
                  xo_ref, sa_ref, sb_ref, za_ext, zb_ext, ca_buf,
                  win_ref, win_stage, win_sem, wout_ref, wout_stage, wout_sem, pw_ref, pw_stage, pw_sem,
                  *, idx, tm, d_a, ha, hb):
    t = pl.program_id(1)
    ka = cw_ref.shape[0]
    nca = za_ext.shape[0]
    ncb = zb_ext.shape[0]
    pb = sb_ref.shape[0]

    @pl.when(jnp.logical_and(pl.program_id(0) == 0, t == 0))
    def _():
        _fetch_bf16(win_hbm, idx, win_ref, win_stage, win_sem)
        _fetch_bf16(wout_hbm, idx, wout_ref, wout_stage, wout_sem)
        _fetch_bf16(pw_hbm, idx, pw_ref, pw_stage, pw_sem)

    @pl.when(t == 0)
    def _():
        za_ext[:, 0:ha, :] = jnp.zeros((nca, ha, _LANE), F32)
        zb_ext[:, 0:hb, :] = jnp.zeros((ncb, hb, _LANE), F32)

    x = x_ref[...]
    h = _rms(x, g_ref[...]).astype(BF16)
    p = _dot(h, win_ref[...])
    za = p[:, :d_a] * jax.nn.sigmoid(p[:, d_a:2 * d_a])
    zb = p[:, 2 * d_a:]
    for c in range(nca):
        za_ext[c, ha:ha + tm, :] = za[:, c * _LANE:(c + 1) * _LANE]
    for c in range(ncb):
        zb_ext[c, hb:hb + tm, :] = zb[:, c * _LANE:(c + 1) * _LANE]

    base = ha - (ka - 1)
    for c in range(nca):
        ls = slice(c * _LANE, (c + 1) * _LANE)
        for r in range(tm // _CONV_ROWS):
            r0 = r * _CONV_ROWS
            acc = jnp.broadcast_to(cb_ref[:, ls], (_CONV_ROWS, _LANE))
            for k in range(ka):
                acc = acc + cw_ref[k:k + 1, ls] * za_ext[c, base + r0 + k:base + r0 + k + _CONV_ROWS, :]
            ca_buf[r0:r0 + _CONV_ROWS, ls] = acc
    ya = _silu(_layernorm(ca_buf[...], lg_ref[...], lb_ref[...]))

    pos = t * tm + lax.broadcasted_iota(jnp.int32, (tm, 1), 0)
    parts = []
    for gi, w in enumerate(_POOL_WINDOWS):
        zg = zb[:, gi * _LANE:(gi + 1) * _LANE]
        s = zg
        for j in range(1, w):
            s = s + zb_ext[gi, hb - j:hb - j + tm, :]
        inv_cnt = 1.0 / jnp.minimum(pos + 1, w).astype(F32)
        pooled = (s * inv_cnt - zg).astype(BF16)
        parts.append(_dot(pooled, pw_ref[gi * _LANE:(gi + 1) * _LANE, :]))
    yb = jnp.concatenate(parts, axis=1) * ps_ref[...]

    y = jnp.concatenate([ya, yb], axis=1).astype(BF16)
    xo_ref[...] = x + _dot(y, wout_ref[...])

    @pl.when(t == pl.num_programs(1) - 1)
    def _():
        for c in range(nca):
            sa_ref[:, c * _LANE:(c + 1) * _LANE] = za_ext[c, ha + tm - (ka - 1):ha + tm, :]
        for c in range(ncb):
            sb_ref[:, c * _LANE:(c + 1) * _LANE] = zb_ext[c, hb + tm - pb:hb + tm, :]

    za_ext[:, 0:ha, :] = za_ext[:, tm:tm + ha, :]
    zb_ext[:, 0:hb, :] = zb_ext[:, tm:tm + hb, :]


def _prompt_even(x, params, big, idx, tm):
    nb, t, d = x.shape
    ops, specs = zip(*params)
    win, wout, pw = big
    ka, d_a = ops[1].shape[-2:]
    d_b = ops[5].shape[-1]
    pb = max(_POOL_WINDOWS) - 1
    ha = _round_up(ka - 1, _SUBLANE)
    hb = _round_up(pb, _SUBLANE)
    assert t % tm == 0 and tm % _CONV_ROWS == 0 and tm >= max(ha, hb)
    assert d_a % _LANE == 0 and d_b == _LANE * len(_POOL_WINDOWS)
    x_spec = pl.BlockSpec((None, tm, d), lambda n, i: (n, i, 0))
    any_spec = pl.BlockSpec(memory_space=pl.ANY)
    kern = functools.partial(_peven_kernel, idx=idx, tm=tm, d_a=d_a, ha=ha, hb=hb)
    return pl.pallas_call(
        kern,
        grid=(nb, t // tm),
        in_specs=[x_spec, *specs, any_spec, any_spec, any_spec],
        out_specs=[x_spec,
                   pl.BlockSpec((None, ka - 1, d_a), lambda n, i: (n, 0, 0)),
                   pl.BlockSpec((None, pb, d_b), lambda n, i: (n, 0, 0))],
        out_shape=[jax.ShapeDtypeStruct((nb, t, d), F32),
                   jax.ShapeDtypeStruct((nb, ka - 1, d_a), F32),
                   jax.ShapeDtypeStruct((nb, pb, d_b), F32)],
        scratch_shapes=[pltpu.VMEM((d_a // _LANE, ha + tm, _LANE), F32),
                        pltpu.VMEM((d_b // _LANE, hb + tm, _LANE), F32),
                        pltpu.VMEM((tm, d_a), F32),
                        *_weight_scratch(*win.shape[1:]), *_weight_scratch(*wout.shape[1:]),
                        *_weight_scratch(*pw.shape[1:])],
        compiler_params=_params("arbitrary", "arbitrary"),
        name="prompt_even",
    )(x, *ops, win, wout, pw)


def _seven_kernel(x_ref, g_ref, cw_ref, cb_ref, lg_ref, lb_ref, ps_ref, ha_ref, hb_ref, win_hbm, wout_hbm, pw_hbm,
                  xo_ref, sa_ref, sb_ref,
                  win_ref, win_stage, win_sem, wout_ref, wout_stage, wout_sem, pw_ref, pw_stage, pw_sem, *, idx):
    _fetch_bf16(win_hbm, idx, win_ref, win_stage, win_sem)
    _fetch_bf16(wout_hbm, idx, wout_ref, wout_stage, wout_sem)
    _fetch_bf16(pw_hbm, idx, pw_ref, pw_stage, pw_sem)
    ka, d_a = cw_ref.shape
    pb = hb_ref.shape[0]
    d_b = hb_ref.shape[2]
    dg = d_b // len(_POOL_WINDOWS)
    x = x_ref[...]
    h = _rms(x, g_ref[...]).astype(BF16)
    p = _dot(h, win_ref[...])
    za = p[:, :d_a] * jax.nn.sigmoid(p[:, d_a:2 * d_a])
    zb = p[:, 2 * d_a:]

    acc = cb_ref[...] + cw_ref[ka - 1:ka, :] * za
    for k in range(ka - 1):
        acc = acc + cw_ref[k:k + 1, :] * ha_ref[k]
    ya = _silu(_layernorm(acc, lg_ref[...], lb_ref[...]))

    parts = []
    for gi, w in enumerate(_POOL_WINDOWS):
        ls = slice(gi * dg, (gi + 1) * dg)
        s = zb[:, ls]
        for j in range(1, w):
            s = s + hb_ref[pb - j, :, ls]
        inv_cnt = 1.0 / float(min(_PAST_LEN + 1, w))
        pooled = (s * inv_cnt - zb[:, ls]).astype(BF16)
        parts.append(_dot(pooled, pw_ref[gi * dg:(gi + 1) * dg, :]))
    yb = jnp.concatenate(parts, axis=1) * ps_ref[...]

    y = jnp.concatenate([ya, yb], axis=1).astype(BF16)
    xo_ref[...] = x + _dot(y, wout_ref[...])

    sa_ref[0:ka - 2] = ha_ref[1:ka - 1]
    sa_ref[ka - 2] = za
    sb_ref[0:pb - 1] = hb_ref[1:pb]
    sb_ref[pb - 1] = zb


def _sample_even(x2d, params, big, idx, hist_a, hist_b):
    b, d = x2d.shape
    ops, specs = zip(*params)
    win, wout, pw = big
    any_spec = pl.BlockSpec(memory_space=pl.ANY)
    return pl.pallas_call(
        functools.partial(_seven_kernel, idx=idx),
        grid=(1,),
        in_specs=[_const_spec((b, d)), *specs, _const_spec(hist_a.shape), _const_spec(hist_b.shape),
                  any_spec, any_spec, any_spec],
        out_specs=[_whole_spec((b, d)), _whole_spec(hist_a.shape), _whole_spec(hist_b.shape)],
        out_shape=[jax.ShapeDtypeStruct((b, d), F32),
                   jax.ShapeDtypeStruct(hist_a.shape, F32),
                   jax.ShapeDtypeStruct(hist_b.shape, F32)],
        scratch_shapes=[*_weight_scratch(*win.shape[1:]), *_weight_scratch(*wout.shape[1:]),
                        *_weight_scratch(*pw.shape[1:])],
        compiler_params=_params("arbitrary"),
        name="sample_even",
    )(x2d, *ops, hist_a, hist_b, win, wout, pw)


def _podd_kernel(x_ref, g_ref, lg_ref, lb_ref, ws_ref, bst_ref, cw_ref, win_hbm, wout_hbm,
                 xo_ref, sv_ref, sd_ref, g_ext, win_ref, win_stage, win_sem, wout_ref, wout_stage, wout_sem,
                 *, idx, tm, d_c, hd_rows):
    t = pl.program_id(1)
    kd, d_d = cw_ref.shape
    heads = ws_ref.shape[0]
    dh = d_c // heads
    ncd = g_ext.shape[0]

    @pl.when(jnp.logical_and(pl.program_id(0) == 0, t == 0))
    def _():
        _fetch_bf16(win_hbm, idx, win_ref, win_stage, win_sem)
        _fetch_bf16(wout_hbm, idx, wout_ref, wout_stage, wout_sem)

    @pl.when(t == 0)
    def _():
        g_ext[:, 0:hd_rows, :] = jnp.zeros((ncd, hd_rows, _LANE), F32)

    x = x_ref[...]
    h = _rms(x, g_ref[...]).astype(BF16)
    p = _dot(h, win_ref[...])
    u = p[:, :d_c]
    v = _layernorm(p[:, d_c:2 * d_c], lg_ref[...], lb_ref[...])
    o = 2 * d_c
    gb = p[:, o:o + d_d]
    gated = p[:, o + d_d:o + 2 * d_d] * p[:, o + 2 * d_d:]

    ri = lax.broadcasted_iota(jnp.int32, (_CHUNK, _CHUNK), 0)
    ci = lax.broadcasted_iota(jnp.int32, (_CHUNK, _CHUNK), 1)
    wm = [jnp.where(ci <= ri, ws_ref[hh], 0.0).astype(BF16) for hh in range(heads)]
    vb = v.astype(BF16)
    rows = []
    for c in range(tm // _CHUNK):
        rs = slice(c * _CHUNK, (c + 1) * _CHUNK)
        parts = [_dot(wm[hh], vb[rs, hh * dh:(hh + 1) * dh]) + bst_ref[:, hh:hh + 1] for hh in range(heads)]
        rows.append(jnp.concatenate(parts, axis=1))
    yc = u * jnp.concatenate(rows, axis=0)

    base = hd_rows - (kd - 1)
    cols = []
    for c in range(ncd):
        ls = slice(c * _LANE, (c + 1) * _LANE)
        g_ext[c, hd_rows:hd_rows + tm, :] = gated[:, ls]
        cd = cw_ref[0:1, ls] * g_ext[c, base:base + tm, :]
        for k in range(1, kd):
            cd = cd + cw_ref[k:k + 1, ls] * g_ext[c, base + k:base + k + tm, :]
        cols.append(cd)
    yd = gb * jnp.concatenate(cols, axis=1)

    y = jnp.concatenate([yc, yd], axis=1).astype(BF16)
    xo_ref[...] = x + _dot(y, wout_ref[...])

    @pl.when(t == pl.num_programs(1) - 1)
    def _():
        n_open = sv_ref.shape[0]
        sv_ref[...] = v[tm - n_open:, :]
        for c in range(ncd):
            sd_ref[:, c * _LANE:(c + 1) * _LANE] = g_ext[c, hd_rows + tm - (kd - 1):hd_rows + tm, :]

    g_ext[:, 0:hd_rows, :] = g_ext[:, tm:tm + hd_rows, :]


def _prompt_odd(x, params, big, idx, tm):
    nb, t, d = x.shape
    ops, specs = zip(*params)
    win, wout = big
    d_c = ops[1].shape[-1]
    kd, d_d = ops[5].shape[-2:]
    n_open = t - ((t - 1) // _CHUNK) * _CHUNK
    hd_rows = _round_up(kd - 1, _SUBLANE)
    assert t % tm == 0 and tm % _CHUNK == 0 and n_open % _SUBLANE == 0 and n_open <= tm
    x_spec = pl.BlockSpec((None, tm, d), lambda n, i: (n, i, 0))
    any_spec = pl.BlockSpec(memory_space=pl.ANY)
    kern = functools.partial(_podd_kernel, idx=idx, tm=tm, d_c=d_c, hd_rows=hd_rows)
    return pl.pallas_call(
        kern,
        grid=(nb, t // tm),
        in_specs=[x_spec, *specs, any_spec, any_spec],
        out_specs=[x_spec,
                   pl.BlockSpec((None, n_open, d_c), lambda n, i: (n, 0, 0)),
                   pl.BlockSpec((None, kd - 1, d_d), lambda n, i: (n, 0, 0))],
        out_shape=[jax.ShapeDtypeStruct((nb, t, d), F32),
                   jax.ShapeDtypeStruct((nb, n_open, d_c), F32),
                   jax.ShapeDtypeStruct((nb, kd - 1, d_d), F32)],
        scratch_shapes=[pltpu.VMEM((d_d // _LANE, hd_rows + tm, _LANE), F32),
                        *_weight_scratch(*win.shape[1:]), *_weight_scratch(*wout.shape[1:])],
        compiler_params=_params("arbitrary", "arbitrary"),
        name="prompt_odd",
    )(x, *ops, win, wout)


def _sodd_kernel(x_ref, g_ref, lg_ref, lb_ref, ws0_ref, bs0_ref, cw_ref, hd_ref, win_hbm, wout_hbm,
                 xo_ref, sv_ref, sd_ref, win_ref, win_stage, win_sem, wout_ref, wout_stage, wout_sem, *, idx):
    _fetch_bf16(win_hbm, idx, win_ref, win_stage, win_sem)
    _fetch_bf16(wout_hbm, idx, wout_ref, wout_stage, wout_sem)
    kd, d_d = cw_ref.shape
    d_c = lg_ref.shape[1]
    x = x_ref[...]
    h = _rms(x, g_ref[...]).astype(BF16)
    p = _dot(h, win_ref[...])
    u = p[:, :d_c]
    v = _layernorm(p[:, d_c:2 * d_c], lg_ref[...], lb_ref[...])
    o = 2 * d_c
    gb = p[:, o:o + d_d]
    gated = p[:, o + d_d:o + 2 * d_d] * p[:, o + 2 * d_d:]

    yc = u * (ws0_ref[...] * v + bs0_ref[...])

    cd = cw_ref[kd - 1:kd, :] * gated
    for k in range(kd - 1):
        cd = cd + cw_ref[k:k + 1, :] * hd_ref[:, k, :]
    yd = gb * cd

    y = jnp.concatenate([yc, yd], axis=1).astype(BF16)
    xo_ref[...] = x + _dot(y, wout_ref[...])

    sv_ref[...] = v
    for k in range(kd - 2):
        sd_ref[:, k, :] = hd_ref[:, k + 1, :]
    sd_ref[:, kd - 2, :] = gated


def _sample_odd(x2d, params, big, idx, ws0, bs0, hist_d):
    b, d = x2d.shape
    ops, specs = zip(*params)
    win, wout = big
    d_c = ops[1].shape[-1]
    ops = ops[:3] + (ws0, bs0) + ops[5:]
    specs = specs[:3] + (_const_spec(ws0.shape), _const_spec(bs0.shape)) + specs[5:]
    any_spec = pl.BlockSpec(memory_space=pl.ANY)
    return pl.pallas_call(
        functools.partial(_sodd_kernel, idx=idx),
        grid=(1,),
        in_specs=[_const_spec((b, d)), *specs, _const_spec(hist_d.shape), any_spec, any_spec],
        out_specs=[_whole_spec((b, d)), _whole_spec((b, d_c)), _whole_spec(hist_d.shape)],
        out_shape=[jax.ShapeDtypeStruct((b, d), F32),
                   jax.ShapeDtypeStruct((b, d_c), F32),
                   jax.ShapeDtypeStruct(hist_d.shape, F32)],
        scratch_shapes=[*_weight_scratch(*win.shape[1:]), *_weight_scratch(*wout.shape[1:])],
        compiler_params=_params("arbitrary"),
        name="sample_odd",
    )(x2d, *ops, hist_d, win, wout)


def kernel(x_prompt, x_sample, mem_prompt, state_convA, state_poolB, state_convD, cache_mem_k, cache_mem_v, norm_mix, norm_x, norm_ffn, norm_final, w_in_even, conv_a_w, conv_a_b, ln_a_g, ln_a_b, pool_b_w, pool_b_scale, w_out_even, w_in_odd, ln_c_g, ln_c_b, ws_c, bs_c, conv_d_w, w_out_odd, wq_x, wk_x, wv_x, wo_x, w_gate, w_up, w_down):
    nb, seq, d = x_prompt.shape
    db, dec_seq, _ = x_sample.shape
    assert dec_seq == 1
    depth = wq_x.shape[0]
    heads = cache_mem_k.shape[3]

    def unstack(parts):
        return parts[0][None] if len(parts) == 1 else jnp.stack(parts)

    g_fin = norm_final[None]
    pool_w2d = pool_b_w.reshape(pool_b_w.shape[0], -1, pool_b_w.shape[-1])

    p_k, p_v = _kv_proj(mem_prompt, wk_x, wv_x, heads)

    xp = x_prompt
    xs = x_sample.reshape(db, d)
    p_a, p_b, p_c, p_d = [], [], [], []
    s_a, s_b, s_c, s_d = [], [], [], []
    for layer in range(depth):
        i = layer // 2
        if layer % 2 == 0:
            params = [_sel(norm_mix, layer), _sel(conv_a_w, i), _sel(conv_a_b, i), _sel(ln_a_g, i), _sel(ln_a_b, i),
                      _sel(pool_b_scale, i)]
            big = (w_in_even, w_out_even, pool_w2d)
            xp, a, b = _prompt_even(xp, params, big, i, tm=_PROMPT_TILE)
            p_a.append(a)
            p_b.append(b)
            xs, a, b = _sample_even(xs, params, big, i, state_convA[i].transpose(1, 0, 2),
                                    state_poolB[i].transpose(1, 0, 2))
            s_a.append(a.transpose(1, 0, 2))
            s_b.append(b.transpose(1, 0, 2))
        else:
            params = [_sel(norm_mix, layer), _sel(ln_c_g, i), _sel(ln_c_b, i), _sel(ws_c, i),
                      _sel(bs_c.transpose(0, 2, 1), i), _sel(conv_d_w, i)]
            big = (w_in_odd, w_out_odd)
            xp, c, dd = _prompt_odd(xp, params, big, i, tm=_PROMPT_TILE)
            dc = c.shape[-1]
            p_c.append(c.reshape(nb, -1, _C_HEADS, dc // _C_HEADS))
            p_d.append(dd)
            ws0 = jnp.repeat(ws_c[i, :, 0, 0], dc // _C_HEADS)[None]
            bs0 = jnp.repeat(bs_c[i, :, 0], dc // _C_HEADS)[None]
            xs, c, dd = _sample_odd(xs, params, big, i, ws0, bs0, state_convD[i])
            s_c.append(c.reshape(db, 1, _C_HEADS, dc // _C_HEADS))
            s_d.append(dd)

        xp, xs = _attn(xp, xs, norm_x, wq_x, p_k, p_v, cache_mem_k, cache_mem_v, wo_x, layer, tm=_ATTN_TILE)

        xp2d, xs = _ffn(xp.reshape(nb * seq, d), xs, norm_ffn, w_gate, w_up, w_down, g_fin, layer, tm=_PROMPT_TILE,
                        final_norm=layer == depth - 1)
        xp = xp2d.reshape(nb, seq, d)

    return (xp, xs.reshape(db, 1, d), unstack(p_a), unstack(p_b), unstack(p_c), unstack(p_d), p_k, p_v,
            unstack(s_a), unstack(s_b), unstack(s_c), unstack(s_d))
```

```python
import functools

import jax
import jax.numpy as jnp
from jax import lax
from jax.experimental import pallas as pl
from jax.experimental.pallas import tpu as pltpu

F32 = jnp.float32
BF16 = jnp.bfloat16

_EPS = 1e-6
_X_HEADS = 4
_C_HEADS = 4
_CHUNK = 128
_POOL_WINDOWS = (2, 4, 8, 16)
_PAST_LEN = 16384

_SUBLANE = 8
_LANE = 128
_VMEM_LIMIT = 56 * 1024 * 1024
_STAGE_BYTES = 1024 * 1024
_STAGE_SLOTS = 4

_PROMPT_TILE = 1024
_ATTN_TILE = 512
_ATTN_PARTS = 2
_KV_BATCHES = 2
_FF_CHUNKS = (768, 768, 768, 512)
_CONV_ROWS = 64


def _round_up(n, m):
    return -(-n // m) * m


def _params(*sem):
    return pltpu.CompilerParams(dimension_semantics=sem, vmem_limit_bytes=_VMEM_LIMIT)


def _const_spec(shape):
    nd = len(shape)
    return pl.BlockSpec(shape, lambda *_: (0,) * nd, pipeline_mode=pl.Buffered(1))


def _whole_spec(shape):
    nd = len(shape)
    return pl.BlockSpec(shape, lambda *_: (0,) * nd)


def _sel(stacked, idx):
    if stacked.ndim == 2:
        stacked = stacked[:, None, :]
    tail = stacked.shape[1:]
    spec = pl.BlockSpec((None,) + tail, lambda *_: (idx,) + (0,) * len(tail), pipeline_mode=pl.Buffered(1))
    return stacked, spec


def _chunk_rows(rows, cols):
    cands = [r for r in range(2 * _SUBLANE, rows + 1, 2 * _SUBLANE)
             if rows % r == 0 and r * cols * 4 <= _STAGE_BYTES]
    return max(cands)


def _weight_scratch(rows, cols):
    return [pltpu.VMEM((rows, cols), BF16), pltpu.VMEM((_STAGE_SLOTS, _chunk_rows(rows, cols), cols), F32),
            pltpu.SemaphoreType.DMA((_STAGE_SLOTS,))]


def _fetch_bf16(w_hbm, idx, dst, stage, sem):
    rows = dst.shape[0]
    slots, rc = stage.shape[:2]

    def chunk(c):
        return pltpu.make_async_copy(w_hbm.at[idx, pl.ds(c * rc, rc), :], stage.at[c % slots], sem.at[c % slots])

    n = rows // rc
    for c in range(min(slots - 1, n)):
        chunk(c).start()
    for c in range(n):
        if c + slots - 1 < n:
            chunk(c + slots - 1).start()
        chunk(c).wait()
        dst[c * rc:(c + 1) * rc, :] = stage[c % slots].astype(BF16)


def _dot(a, b):
    return jnp.dot(a, b, preferred_element_type=F32)


def _rms(x, g):
    y = x * lax.rsqrt(jnp.mean(x * x, axis=-1, keepdims=True) + _EPS)
    return y * g


def _layernorm(x, g, b):
    mu = jnp.mean(x, axis=-1, keepdims=True)
    d = x - mu
    var = jnp.mean(d * d, axis=-1, keepdims=True)
    return d * lax.rsqrt(var + _EPS) * g + b


def _silu(x):
    return x * jax.nn.sigmoid(x)


def _softmax_rows(s):
    m = jnp.max(s, axis=-1, keepdims=True)
    e = jnp.exp(s - m)
    return e * (1.0 / jnp.sum(e, axis=-1, keepdims=True))


_DONE = object()


def _run_staggered(seqs):
    live = []
    pending = list(seqs)
    while live or pending:
        if pending:
            live.append(pending.pop(0))
        live = [s for s in live if next(s, _DONE) is not _DONE]


def _kv_out_copies(kbuf, vbuf, k_hbm, v_hbm, sem, layer, blk, slot):
    _, heads, nbk = kbuf.shape[:3]
    rows = pl.ds(blk * nbk, nbk)
    cps = []
    for hd in range(heads):
        cps.append(pltpu.make_async_copy(kbuf.at[slot, hd], k_hbm.at[layer, rows, :, hd, :], sem.at[slot, 0, hd]))
        cps.append(pltpu.make_async_copy(vbuf.at[slot, hd], v_hbm.at[layer, rows, :, hd, :], sem.at[slot, 1, hd]))
    return cps


def _kv_kernel(m_ref, wk_hbm, wv_hbm, k_hbm, v_hbm, kbuf, vbuf, sem, wk_ref, wv_ref, stage, wsem):
    nj = pl.num_programs(1)
    layer = pl.program_id(0)
    blk = pl.program_id(1)
    step = layer * nj + blk
    last = pl.num_programs(0) * nj - 1
    slot = lax.rem(step, 2)
    _, heads, nbk, mem, dh = kbuf.shape
    copies = functools.partial(_kv_out_copies, kbuf, vbuf, k_hbm, v_hbm, sem)

    @pl.when(blk == 0)
    def _():
        _fetch_bf16(wk_hbm, layer, wk_ref, stage, wsem)
        _fetch_bf16(wv_hbm, layer, wv_ref, stage, wsem)

    @pl.when(step >= 2)
    def _():
        prev = step - 2
        for cp in copies(lax.div(prev, nj), lax.rem(prev, nj), slot):
            cp.wait()

    m = m_ref[...].reshape(nbk * mem, m_ref.shape[-1]).astype(BF16)
    k = _dot(m, wk_ref[...])
    v = _dot(m, wv_ref[...])
    for hd in range(heads):
        for b in range(nbk):
            kbuf[slot, hd, b] = k[b * mem:(b + 1) * mem, hd * dh:(hd + 1) * dh]
            vbuf[slot, hd, b] = v[b * mem:(b + 1) * mem, hd * dh:(hd + 1) * dh]
    for cp in copies(layer, blk, slot):
        cp.start()

    @pl.when(step == last)
    def _():
        @pl.when(step >= 1)
        def _():
            prev = step - 1
            for cp in copies(lax.div(prev, nj), lax.rem(prev, nj), 1 - slot):
                cp.wait()
        for cp in copies(layer, blk, slot):
            cp.wait()


def _kv_proj(mem3, wk, wv, heads):
    nb, mem, d = mem3.shape
    depth = wk.shape[0]
    dh = d // heads
    nbk = _KV_BATCHES
    assert nb % nbk == 0 and mem % _SUBLANE == 0
    any_spec = pl.BlockSpec(memory_space=pl.ANY)
    buf = pltpu.VMEM((2, heads, nbk, mem, dh), F32)
    wk_scr, stage, wsem = _weight_scratch(d, d)
    return pl.pallas_call(
        _kv_kernel,
        grid=(depth, nb // nbk),
        in_specs=[pl.BlockSpec((nbk, mem, d), lambda l, j: (j, 0, 0)), any_spec, any_spec],
        out_specs=[any_spec, any_spec],
        out_shape=[jax.ShapeDtypeStruct((depth, nb, mem, heads, dh), F32)] * 2,
        scratch_shapes=[buf, buf, pltpu.SemaphoreType.DMA((2, 2, heads)), wk_scr, wk_scr, stage, wsem],
        compiler_params=_params("arbitrary", "arbitrary"),
        name="kv_proj",
    )(mem3, wk, wv)


def _ffn_kernel(xp_ref, xs_ref, g_ref, gf_ref, wg_hbm, wu_hbm, wd_hbm, op_ref, os_ref,
                wg_ref, wu_ref, up_stage, up_sem, wd_ref, dn_stage, dn_sem, *, layer, chunks, final_norm):
    @pl.when(pl.program_id(0) == 0)
    def _():
        _fetch_bf16(wg_hbm, layer, wg_ref, up_stage, up_sem)
        _fetch_bf16(wu_hbm, layer, wu_ref, up_stage, up_sem)
        _fetch_bf16(wd_hbm, layer, wd_ref, dn_stage, dn_sem)

    def swiglu(x_ref, o_ref):
        x = x_ref[...]
        h = _rms(x, g_ref[...]).astype(BF16)
        acc = x
        lo = 0
        for width in chunks:
            gate = _dot(h, wg_ref[:, lo:lo + width])
            up = _dot(h, wu_ref[:, lo:lo + width])
            act = (_silu(gate) * up).astype(BF16)
            acc = acc + _dot(act, wd_ref[lo:lo + width, :])
            lo += width
        if final_norm:
            acc = _rms(acc, gf_ref[...])
        o_ref[...] = acc

    is_sample_step = pl.program_id(0) == pl.num_programs(0) - 1

    @pl.when(jnp.logical_not(is_sample_step))
    def _():
        swiglu(xp_ref, op_ref)

    @pl.when(is_sample_step)
    def _():
        swiglu(xs_ref, os_ref)


def _ffn(xp2d, xs2d, g, wg, wu, wd, gf, layer, tm, final_norm):
    rows, d = xp2d.shape
    b = xs2d.shape[0]
    ff = wg.shape[-1]
    assert sum(_FF_CHUNKS) == ff and rows % tm == 0
    n_tiles = rows // tm
    kern = functools.partial(_ffn_kernel, layer=layer, chunks=_FF_CHUNKS, final_norm=final_norm)
    row_spec = pl.BlockSpec((tm, d), lambda i: (jnp.minimum(i, n_tiles - 1), 0))
    any_spec = pl.BlockSpec(memory_space=pl.ANY)
    ops, specs = zip(_sel(g, layer), _sel(gf, 0))
    wg_scr, up_stage, up_sem = _weight_scratch(d, ff)
    return pl.pallas_call(
        kern,
        grid=(n_tiles + 1,),
        in_specs=[row_spec, _const_spec((b, d)), *specs, any_spec, any_spec, any_spec],
        out_specs=[row_spec, _whole_spec((b, d))],
        out_shape=[jax.ShapeDtypeStruct((rows, d), F32), jax.ShapeDtypeStruct((b, d), F32)],
        scratch_shapes=[wg_scr, wg_scr, up_stage, up_sem, *_weight_scratch(ff, d)],
        compiler_params=_params("arbitrary"),
        name="ffn",
    )(xp2d, xs2d, *ops, wg, wu, wd)


def _pattn_kv_copies(k_hbm, v_hbm, kvbuf, sem, layer, n, slot):
    heads = kvbuf.shape[2]
    cps = []
    for j, src in enumerate((k_hbm, v_hbm)):
        for hd in range(heads):
            cps.append(pltpu.make_async_copy(src.at[layer, n, :, hd, :], kvbuf.at[slot, j, hd], sem.at[slot, j, hd]))
    return cps


def _sattn_copies(k_hbm, v_hbm, cbuf, sem, layer, blk, slot):
    _, _, heads, bb = cbuf.shape[:4]
    rows = pl.ds(blk * bb, bb)
    cps = []
    for j, src in enumerate((k_hbm, v_hbm)):
        for hd in range(heads):
            cps.append(pltpu.make_async_copy(src.at[layer, rows, :, hd, :], cbuf.at[slot, j, hd], sem.at[slot, j, hd]))
    return cps


def _attn_kernel(xp_ref, xs_ref, g_ref, pk_hbm, pv_hbm, ck_hbm, cv_hbm, wq_hbm, wo_hbm, op_ref, os_ref,
                 kvbuf, kv16, psem, q_scr, o_scr, cbuf, csem, wq_ref, wo_ref, stage, wsem, *, layer, parts):
    n = pl.program_id(0)
    i = pl.program_id(1)
    nt = pl.num_programs(1)
    step = n * nt + i
    last = pl.num_programs(0) * nt - 1
    heads, _, dh = kv16.shape[1:]
    bb = cbuf.shape[3]
    tm = xp_ref.shape[0]
    rows = tm // parts
    pslot = lax.rem(n, 2)
    cslot = lax.rem(step, 2)
    pcopies = functools.partial(_pattn_kv_copies, pk_hbm, pv_hbm, kvbuf, psem, layer)
    ccopies = functools.partial(_sattn_copies, ck_hbm, cv_hbm, cbuf, csem, layer)

    @pl.when(step == 0)
    def _():
        for cp in pcopies(0, 0) + ccopies(0, 0):
            cp.start()
        _fetch_bf16(wq_hbm, layer, wq_ref, stage, wsem)
        _fetch_bf16(wo_hbm, layer, wo_ref, stage, wsem)
        h = _rms(xs_ref[...], g_ref[...]).astype(BF16)
        q_scr[...] = _dot(h, wq_ref[...]) * (dh ** -0.5)

    @pl.when(step < last)
    def _():
        for cp in ccopies(step + 1, 1 - cslot):
            cp.start()

    @pl.when(i == 0)
    def _():
        @pl.when(n + 1 < pl.num_programs(0))
        def _():
            for cp in pcopies(n + 1, 1 - pslot):
                cp.start()

        for cp in pcopies(n, pslot):
            cp.wait()
        kv16[...] = kvbuf[pslot].astype(BF16)

    for cp in ccopies(step, cslot):
        cp.wait()

    def attend(r0):
        x = xp_ref[r0:r0 + rows, :]
        h = _rms(x, g_ref[...]).astype(BF16)
        yield
        q = _dot(h, wq_ref[...]).astype(BF16)
        yield
        outs = []
        for hd in range(heads):
            s = lax.dot_general(q[:, hd * dh:(hd + 1) * dh], kv16[0, hd], (((1,), (1,)), ((), ())),
                                preferred_element_type=F32)
            yield
            a = _softmax_rows(s * (dh ** -0.5)).astype(BF16)
            yield
            outs.append(_dot(a, kv16[1, hd]))
            yield
        o = jnp.concatenate(outs, axis=1).astype(BF16)
        op_ref[r0:r0 + rows, :] = x + _dot(o, wo_ref[...])
        yield

    def sample_rows():
        reps = dh // _LANE
        for j in range(bb):
            r = step * bb + j
            outs = []
            for hd in range(heads):
                q_row = q_scr[pl.ds(r, 1), hd * dh:(hd + 1) * dh]
                q_rep = jnp.broadcast_to(q_row, (_LANE, dh)).astype(BF16)
                kh = cbuf[cslot, 0, hd, j].astype(BF16)
                s = lax.dot_general(kh, q_rep, (((1,), (1,)), ((), ())), preferred_element_type=F32)
                e = jnp.exp(s - jnp.max(s, axis=0, keepdims=True))
                den = jnp.sum(e, axis=0, keepdims=True)
                num = jnp.sum(jnp.concatenate([e] * reps, axis=1) * cbuf[cslot, 1, hd, j], axis=0, keepdims=True)
                outs.append(num * (1.0 / jnp.concatenate([den] * reps, axis=1)))
                yield
            o_scr[pl.ds(r, 1), :] = jnp.concatenate(outs, axis=1)

    _run_staggered([attend(p * rows) for p in range(parts)] + [sample_rows()])

    @pl.when(step == last)
    def _():
        os_ref[...] = xs_ref[...] + _dot(o_scr[...].astype(BF16), wo_ref[...])


def _attn(xp, xs, g, wq, pk5, pv5, ck5, cv5, wo, layer, tm):
    nb, t, d = xp.shape
    b = xs.shape[0]
    _, _, mem_len, heads, dh = pk5.shape
    steps = nb * (t // tm)
    assert t % tm == 0 and tm % (_ATTN_PARTS * _SUBLANE) == 0 and b % steps == 0
    bb = b // steps
    x_spec = pl.BlockSpec((None, tm, d), lambda n, i: (n, i, 0))
    any_spec = pl.BlockSpec(memory_space=pl.ANY)
    g, g_spec = _sel(g, layer)
    wq_scr, stage, wsem = _weight_scratch(d, d)
    return pl.pallas_call(
        functools.partial(_attn_kernel, layer=layer, parts=_ATTN_PARTS),
        grid=(nb, t // tm),
        in_specs=[x_spec, _const_spec((b, d)), g_spec] + [any_spec] * 6,
        out_specs=[x_spec, _whole_spec((b, d))],
        out_shape=[jax.ShapeDtypeStruct((nb, t, d), F32), jax.ShapeDtypeStruct((b, d), F32)],
        scratch_shapes=[pltpu.VMEM((2, 2, heads, mem_len, dh), F32), pltpu.VMEM((2, heads, mem_len, dh), BF16),
                        pltpu.SemaphoreType.DMA((2, 2, heads)),
                        pltpu.VMEM((b, d), F32), pltpu.VMEM((b, d), F32),
                        pltpu.VMEM((2, 2, heads, bb, mem_len, dh), F32), pltpu.SemaphoreType.DMA((2, 2, heads)),
                        wq_scr, wq_scr, stage, wsem],
        compiler_params=_params("arbitrary", "arbitrary"),
        name="attn",
    )(xp, xs, g, pk5, pv5, ck5, cv5, wq, wo)


def _peven_kernel(x_ref, g_ref, cw_ref, cb_ref, lg_ref, lb_ref, ps_ref, win_hbm, wout_hbm, pw_hbm,
                  xo_ref, sa_ref, sb_ref, za_ext, zb_ext, ca_buf,
                  win_ref, win_stage, win_sem, wout_ref, wout_stage, wout_sem, pw_ref, pw_stage, pw_sem,
                  *, idx, tm, d_a, ha, hb):
    t = pl.program_id(1)
    ka = cw_ref.shape[0]
    nca = za_ext.shape[0]
    ncb = zb_ext.shape[0]
    pb = sb_ref.shape[0]

    @pl.when(jnp.logical_and(pl.program_id(0) == 0, t == 0))
    def _():
        _fetch_bf16(win_hbm, idx, win_ref, win_stage, win_sem)
        _fetch_bf16(wout_hbm, idx, wout_ref, wout_stage, wout_sem)
        _fetch_bf16(pw_hbm, idx, pw_ref, pw_stage, pw_sem)

    @pl.when(t == 0)
    def _():
        za_ext[:, 0:ha, :] = jnp.zeros((nca, ha, _LANE), F32)
        zb_ext[:, 0:hb, :] = jnp.zeros((ncb, hb, _LANE), F32)

    x = x_ref[...]
    h = _rms(x, g_ref[...]).astype(BF16)
    p = _dot(h, win_ref[...])
    za = p[:, :d_a] * jax.nn.sigmoid(p[:, d_a:2 * d_a])
    zb = p[:, 2 * d_a:]
    for c in range(nca):
        za_ext[c, ha:ha + tm, :] = za[:, c * _LANE:(c + 1) * _LANE]
    for c in range(ncb):
        zb_ext[c, hb:hb + tm, :] = zb[:, c * _LANE:(c + 1) * _LANE]

    base = ha - (ka - 1)
    for c in range(nca):
        ls = slice(c * _LANE, (c + 1) * _LANE)
        for r in range(tm // _CONV_ROWS):
            r0 = r * _CONV_ROWS
            acc = jnp.broadcast_to(cb_ref[:, ls], (_CONV_ROWS, _LANE))
            for k in range(ka):
                acc = acc + cw_ref[k:k + 1, ls] * za_ext[c, base + r0 + k:base + r0 + k + _CONV_ROWS, :]
            ca_buf[r0:r0 + _CONV_ROWS, ls] = acc
    ya = _silu(_layernorm(ca_buf[...], lg_ref[...], lb_ref[...]))

    pos = t * tm + lax.broadcasted_iota(jnp.int32, (tm, 1), 0)
    parts = []
    for gi, w in enumerate(_POOL_WINDOWS):
        zg = zb[:, gi * _LANE:(gi + 1) * _LANE]
        s = zg
        for j in range(1, w):
            s = s + zb_ext[gi, hb - j:hb - j + tm, :]
        inv_cnt = 1.0 / jnp.minimum(pos + 1, w).astype(F32)
        pooled = (s * inv_cnt - zg).astype(BF16)
        parts.append(_dot(pooled, pw_ref[gi * _LANE:(gi + 1) * _LANE, :]))
    yb = jnp.concatenate(parts, axis=1) * ps_ref[...]

    y = jnp.concatenate([ya, yb], axis=1).astype(BF16)
    xo_ref[...] = x + _dot(y, wout_ref[...])

    @pl.when(t == pl.num_programs(1) - 1)
    def _():
        for c in range(nca):
            sa_ref[:, c * _LANE:(c + 1) * _LANE] = za_ext[c, ha + tm - (ka - 1):ha + tm, :]
        for c in range(ncb):
            sb_ref[:, c * _LANE:(c + 1) * _LANE] = zb_ext[c, hb + tm - pb:hb + tm, :]

    za_ext[:, 0:ha, :] = za_ext[:, tm:tm + ha, :]
    zb_ext[:, 0:hb, :] = zb_ext[:, tm:tm + hb, :]


def _prompt_even(x, params, big, idx, tm):
    nb, t, d = x.shape
    ops, specs = zip(*params)
    win, wout, pw = big
    ka, d_a = ops[1].shape[-2:]
    d_b = ops[5].shape[-1]
    pb = max(_POOL_WINDOWS) - 1
    ha = _round_up(ka - 1, _SUBLANE)
    hb = _round_up(pb, _SUBLANE)
    assert t % tm == 0 and tm % _CONV_ROWS == 0 and tm >= max(ha, hb)
    assert d_a % _LANE == 0 and d_b == _LANE * len(_POOL_WINDOWS)
    x_spec = pl.BlockSpec((None, tm, d), lambda n, i: (n, i, 0))
    any_spec = pl.BlockSpec(memory_space=pl.ANY)
    kern = functools.partial(_peven_kernel, idx=idx, tm=tm, d_a=d_a, ha=ha, hb=hb)
    return pl.pallas_call(
        kern,
        grid=(nb, t // tm),
        in_specs=[x_spec, *specs, any_spec, any_spec, any_spec],
        out_specs=[x_spec,
                   pl.BlockSpec((None, ka - 1, d_a), lambda n, i: (n, 0, 0)),
                   pl.BlockSpec((None, pb, d_b), lambda n, i: (n, 0, 0))],
        out_shape=[jax.ShapeDtypeStruct((nb, t, d), F32),
                   jax.ShapeDtypeStruct((nb, ka - 1, d_a), F32),
                   jax.ShapeDtypeStruct((nb, pb, d_b), F32)],
        scratch_shapes=[pltpu.VMEM((d_a // _LANE, ha + tm, _LANE), F32),
                        pltpu.VMEM((d_b // _LANE, hb + tm, _LANE), F32),
                        pltpu.VMEM((tm, d_a), F32),
                        *_weight_scratch(*win.shape[1:]), *_weight_scratch(*wout.shape[1:]),
                        *_weight_scratch(*pw.shape[1:])],
        compiler_params=_params("arbitrary", "arbitrary"),
        name="prompt_even",
    )(x, *ops, win, wout, pw)


def _seven_kernel(x_ref, g_ref, cw_ref, cb_ref, lg_ref, lb_ref, ps_ref, ha_ref, hb_ref, win_hbm, wout_hbm, pw_hbm,
                  xo_ref, sa_ref, sb_ref,
                  win_ref, win_stage, win_sem, wout_ref, wout_stage, wout_sem, pw_ref, pw_stage, pw_sem, *, idx):
    _fetch_bf16(win_hbm, idx, win_ref, win_stage, win_sem)
    _fetch_bf16(wout_hbm, idx, wout_ref, wout_stage, wout_sem)
    _fetch_bf16(pw_hbm, idx, pw_ref, pw_stage, pw_sem)
    ka, d_a = cw_ref.shape
    pb = hb_ref.shape[0]
    d_b = hb_ref.shape[2]
    dg = d_b // len(_POOL_WINDOWS)
    x = x_ref[...]
    h = _rms(x, g_ref[...]).astype(BF16)
    p = _dot(h, win_ref[...])
    za = p[:, :d_a] * jax.nn.sigmoid(p[:, d_a:2 * d_a])
    zb = p[:, 2 * d_a:]

    acc = cb_ref[...] + cw_ref[ka - 1:ka, :] * za
    for k in range(ka - 1):
        acc = acc + cw_ref[k:k + 1, :] * ha_ref[k]
    ya = _silu(_layernorm(acc, lg_ref[...], lb_ref[...]))

    parts = []
    for gi, w in enumerate(_POOL_WINDOWS):
        ls = slice(gi * dg, (gi + 1) * dg)
        s = zb[:, ls]
        for j in range(1, w):
            s = s + hb_ref[pb - j, :, ls]
        inv_cnt = 1.0 / float(min(_PAST_LEN + 1, w))
        pooled = (s * inv_cnt - zb[:, ls]).astype(BF16)
        parts.append(_dot(pooled, pw_ref[gi * dg:(gi + 1) * dg, :]))
    yb = jnp.concatenate(parts, axis=1) * ps_ref[...]

    y = jnp.concatenate([ya, yb], axis=1).astype(BF16)
    xo_ref[...] = x + _dot(y, wout_ref[...])

    sa_ref[0:ka - 2] = ha_ref[1:ka - 1]
    sa_ref[ka - 2] = za
    sb_ref[0:pb - 1] = hb_ref[1:pb]
    sb_ref[pb - 1] = zb


def _sample_even(x2d, params, big, idx, hist_a, hist_b):
    b, d = x2d.shape
    ops, specs = zip(*params)
    win, wout, pw = big
    any_spec = pl.BlockSpec(memory_space=pl.ANY)
    return pl.pallas_call(
        functools.partial(_seven_kernel, idx=idx),
        grid=(1,),
        in_specs=[_const_spec((b, d)), *specs, _const_spec(hist_a.shape), _const_spec(hist_b.shape),
                  any_spec, any_spec, any_spec],
        out_specs=[_whole_spec((b, d)), _whole_spec(hist_a.shape), _whole_spec(hist_b.shape)],
        out_shape=[jax.ShapeDtypeStruct((b, d), F32),
                   jax.ShapeDtypeStruct(hist_a.shape, F32),
                   jax.ShapeDtypeStruct(hist_b.shape, F32)],
        scratch_shapes=[*_weight_scratch(*win.shape[1:]), *_weight_scratch(*wout.shape[1:]),
                        *_weight_scratch(*pw.shape[1:])],
        compiler_params=_params("arbitrary"),
        name="sample_even",
    )(x2d, *ops, hist_a, hist_b, win, wout, pw)


def _podd_kernel(x_ref, g_ref, lg_ref, lb_ref, ws_ref, bst_ref, cw_ref, win_hbm, wout_hbm,
                 xo_ref, sv_ref, sd_ref, g_ext, win_ref, win_stage, win_sem, wout_ref, wout_stage, wout_sem,
                 *, idx, tm, d_c, hd_rows):
    t = pl.program_id(1)
    kd, d_d = cw_ref.shape
    heads = ws_ref.shape[0]
    dh = d_c // heads
    ncd = g_ext.shape[0]

    @pl.when(jnp.logical_and(pl.program_id(0) == 0, t == 0))
    def _():
        _fetch_bf16(win_hbm, idx, win_ref, win_stage, win_sem)
        _fetch_bf16(wout_hbm, idx, wout_ref, wout_stage, wout_sem)

    @pl.when(t == 0)
    def _():
        g_ext[:, 0:hd_rows, :] = jnp.zeros((ncd, hd_rows, _LANE), F32)

    x = x_ref[...]
    h = _rms(x, g_ref[...]).astype(BF16)
    p = _dot(h, win_ref[...])
    u = p[:, :d_c]
    v = _layernorm(p[:, d_c:2 * d_c], lg_ref[...], lb_ref[...])
    o = 2 * d_c
    gb = p[:, o:o + d_d]
    gated = p[:, o + d_d:o + 2 * d_d] * p[:, o + 2 * d_d:]

    ri = lax.broadcasted_iota(jnp.int32, (_CHUNK, _CHUNK), 0)
    ci = lax.broadcasted_iota(jnp.int32, (_CHUNK, _CHUNK), 1)
    wm = [jnp.where(ci <= ri, ws_ref[hh], 0.0).astype(BF16) for hh in range(heads)]
    vb = v.astype(BF16)
    rows = []
    for c in range(tm // _CHUNK):
        rs = slice(c * _CHUNK, (c + 1) * _CHUNK)
        parts = [_dot(wm[hh], vb[rs, hh * dh:(hh + 1) * dh]) + bst_ref[:, hh:hh + 1] for hh in range(heads)]
        rows.append(jnp.concatenate(parts, axis=1))
    yc = u * jnp.concatenate(rows, axis=0)

    base = hd_rows - (kd - 1)
    cols = []
    for c in range(ncd):
        ls = slice(c * _LANE, (c + 1) * _LANE)
        g_ext[c, hd_rows:hd_rows + tm, :] = gated[:, ls]
        cd = cw_ref[0:1, ls] * g_ext[c, base:base + tm, :]
        for k in range(1, kd):
            cd = cd + cw_ref[k:k + 1, ls] * g_ext[c, base + k:base + k + tm, :]
        cols.append(cd)
    yd = gb * jnp.concatenate(cols, axis=1)

    y = jnp.concatenate([yc, yd], axis=1).astype(BF16)
    xo_ref[...] = x + _dot(y, wout_ref[...])

    @pl.when(t == pl.num_programs(1) - 1)
    def _():
        n_open = sv_ref.shape[0]
        sv_ref[...] = v[tm - n_open:, :]
        for c in range(ncd):
            sd_ref[:, c * _LANE:(c + 1) * _LANE] = g_ext[c, hd_rows + tm - (kd - 1):hd_rows + tm, :]

    g_ext[:, 0:hd_rows, :] = g_ext[:, tm:tm + hd_rows, :]


def _prompt_odd(x, params, big, idx, tm):
    nb, t, d = x.shape
    ops, specs = zip(*params)
    win, wout = big
    d_c = ops[1].shape[-1]
    kd, d_d = ops[5].shape[-2:]
    n_open = t - ((t - 1) // _CHUNK) * _CHUNK
    hd_rows = _round_up(kd - 1, _SUBLANE)
    assert t % tm == 0 and tm % _CHUNK == 0 and n_open % _SUBLANE == 0 and n_open <= tm
    x_spec = pl.BlockSpec((None, tm, d), lambda n, i: (n, i, 0))
    any_spec = pl.BlockSpec(memory_space=pl.ANY)
    kern = functools.partial(_podd_kernel, idx=idx, tm=tm, d_c=d_c, hd_rows=hd_rows)
    return pl.pallas_call(
        kern,
        grid=(nb, t // tm),
        in_specs=[x_spec, *specs, any_spec, any_spec],
        out_specs=[x_spec,
                   pl.BlockSpec((None, n_open, d_c), lambda n, i: (n, 0, 0)),
                   pl.BlockSpec((None, kd - 1, d_d), lambda n, i: (n, 0, 0))],
        out_shape=[jax.ShapeDtypeStruct((nb, t, d), F32),
                   jax.ShapeDtypeStruct((nb, n_open, d_c), F32),
                   jax.ShapeDtypeStruct((nb, kd - 1, d_d), F32)],
        scratch_shapes=[pltpu.VMEM((d_d // _LANE, hd_rows + tm, _LANE), F32),
                        *_weight_scratch(*win.shape[1:]), *_weight_scratch(*wout.shape[1:])],
        compiler_params=_params("arbitrary", "arbitrary"),
        name="prompt_odd",
    )(x, *ops, win, wout)


def _sodd_kernel(x_ref, g_ref, lg_ref, lb_ref, ws0_ref, bs0_ref, cw_ref, hd_ref, win_hbm, wout_hbm,
                 xo_ref, sv_ref, sd_ref, win_ref, win_stage, win_sem, wout_ref, wout_stage, wout_sem, *, idx):
    _fetch_bf16(win_hbm, idx, win_ref, win_stage, win_sem)
    _fetch_bf16(wout_hbm, idx, wout_ref, wout_stage, wout_sem)
    kd, d_d = cw_ref.shape
    d_c = lg_ref.shape[1]
    x = x_ref[...]
    h = _rms(x, g_ref[...]).astype(BF16)
    p = _dot(h, win_ref[...])
    u = p[:, :d_c]
    v = _layernorm(p[:, d_c:2 * d_c], lg_ref[...], lb_ref[...])
    o = 2 * d_c
    gb = p[:, o:o + d_d]
    gated = p[:, o + d_d:o + 2 * d_d] * p[:, o + 2 * d_d:]

    yc = u * (ws0_ref[...] * v + bs0_ref[...])

    cd = cw_ref[kd - 1:kd, :] * gated
    for k in range(kd - 1):
        cd = cd + cw_ref[k:k + 1, :] * hd_ref[:, k, :]
    yd = gb * cd

    y = jnp.concatenate([yc, yd], axis=1).astype(BF16)
    xo_ref[...] = x + _dot(y, wout_ref[...])

    sv_ref[...] = v
    for k in range(kd - 2):
        sd_ref[:, k, :] = hd_ref[:, k + 1, :]
    sd_ref[:, kd - 2, :] = gated


def _sample_odd(x2d, params, big, idx, ws0, bs0, hist_d):
    b, d = x2d.shape
    ops, specs = zip(*params)
    win, wout = big
    d_c = ops[1].shape[-1]
    ops = ops[:3] + (ws0, bs0) + ops[5:]
    specs = specs[:3] + (_const_spec(ws0.shape), _const_spec(bs0.shape)) + specs[5:]
    any_spec = pl.BlockSpec(memory_space=pl.ANY)
    return pl.pallas_call(
        functools.partial(_sodd_kernel, idx=idx),
        grid=(1,),
        in_specs=[_const_spec((b, d)), *specs, _const_spec(hist_d.shape), any_spec, any_spec],
        out_specs=[_whole_spec((b, d)), _whole_spec((b, d_c)), _whole_spec(hist_d.shape)],
        out_shape=[jax.ShapeDtypeStruct((b, d), F32),
                   jax.ShapeDtypeStruct((b, d_c), F32),
                   jax.ShapeDtypeStruct(hist_d.shape, F32)],
        scratch_shapes=[*_weight_scratch(*win.shape[1:]), *_weight_scratch(*wout.shape[1:])],
        compiler_params=_params("arbitrary"),
        name="sample_odd",
    )(x2d, *ops, hist_d, win, wout)


def kernel(x_prompt, x_sample, mem_prompt, state_convA, state_poolB, state_convD, cache_mem_k, cache_mem_v, norm_mix, norm_x, norm_ffn, norm_final, w_in_even, conv_a_w, conv_a_b, ln_a_g, ln_a_b, pool_b_w, pool_b_scale, w_out_even, w_in_odd, ln_c_g, ln_c_b, ws_c, bs_c, conv_d_w, w_out_odd, wq_x, wk_x, wv_x, wo_x, w_gate, w_up, w_down):
    nb, seq, d = x_prompt.shape
    db, dec_seq, _ = x_sample.shape
    assert dec_seq == 1
    depth = wq_x.shape[0]
    heads = cache_mem_k.shape[3]

    def unstack(parts):
        return parts[0][None] if len(parts) == 1 else jnp.stack(parts)

    g_fin = norm_final[None]
    pool_w2d = pool_b_w.reshape(pool_b_w.shape[0], -1, pool_b_w.shape[-1])

    p_k, p_v = _kv_proj(mem_prompt, wk_x, wv_x, heads)

    xp = x_prompt
    xs = x_sample.reshape(db, d)
    p_a, p_b, p_c, p_d = [], [], [], []
    s_a, s_b, s_c, s_d = [], [], [], []
    for layer in range(depth):
        i = layer // 2
        if layer % 2 == 0:
            params = [_sel(norm_mix, layer), _sel(conv_a_w, i), _sel(conv_a_b, i), _sel(ln_a_g, i), _sel(ln_a_b, i),
                      _sel(pool_b_scale, i)]
            big = (w_in_even, w_out_even, pool_w2d)
            xp, a, b = _prompt_even(xp, params, big, i, tm=_PROMPT_TILE)
            p_a.append(a)
            p_b.append(b)
            xs, a, b = _sample_even(xs, params, big, i, state_convA[i].transpose(1, 0, 2),
                                    state_poolB[i].transpose(1, 0, 2))
            s_a.append(a.transpose(1, 0, 2))
            s_b.append(b.transpose(1, 0, 2))
        else:
            params = [_sel(norm_mix, layer), _sel(ln_c_g, i), _sel(ln_c_b, i), _sel(ws_c, i),
                      _sel(bs_c.transpose(0, 2, 1), i), _sel(conv_d_w, i)]
            big = (w_in_odd, w_out_odd)
            xp, c, dd = _prompt_odd(xp, params, big, i, tm=_PROMPT_TILE)
            dc = c.shape[-1]
            p_c.append(c.reshape(nb, -1, _C_HEADS, dc // _C_HEADS))
            p_d.append(dd)
            ws0 = jnp.repeat(ws_c[i, :, 0, 0], dc // _C_HEADS)[None]
            bs0 = jnp.repeat(bs_c[i, :, 0], dc // _C_HEADS)[None]
            xs, c, dd = _sample_odd(xs, params, big, i, ws0, bs0, state_convD[i])
            s_c.append(c.reshape(db, 1, _C_HEADS, dc // _C_HEADS))
            s_d.append(dd)

        xp, xs = _attn(xp, xs, norm_x, wq_x, p_k, p_v, cache_mem_k, cache_mem_v, wo_x, layer, tm=_ATTN_TILE)

        xp2d, xs = _ffn(xp.reshape(nb * seq, d), xs, norm_ffn, w_gate, w_up, w_down, g_fin, layer, tm=_PROMPT_TILE,
                        final_norm=layer == depth - 1)
        xp = xp2d.reshape(nb, seq, d)

    return (xp, xs.reshape(db, 1, d), unstack(p_a), unstack(p_b), unstack(p_c), unstack(p_d), p_k, p_v,
            unstack(s_a), unstack(s_b), unstack(s_c), unstack(s_d))
```

```python
import functools

import jax
import jax.numpy as jnp
from jax import lax
from jax.experimental import pallas as pl
from jax.experimental.pallas import tpu as pltpu

F32 = jnp.float32
BF16 = jnp.bfloat16

_EPS = 1e-6
_X_HEADS = 4
_C_HEADS = 4
_CHUNK = 128
_POOL_WINDOWS = (2, 4, 8, 16)
_PAST_LEN = 16384

_SUBLANE = 8
_LANE = 128
_VMEM_LIMIT = 56 * 1024 * 1024
_STAGE_BYTES = 1024 * 1024
_STAGE_SLOTS = 4

_PROMPT_TILE = 1024
_ATTN_TILE = 1024
_ATTN_PARTS = 2
_FFN_TILE = 512
_KV_BATCHES = 2
_FF_CHUNKS = (768, 768, 768, 512)
_CONV_ROWS = 16


def _round_up(n, m):
    return -(-n // m) * m


def _params(*sem):
    return pltpu.CompilerParams(dimension_semantics=sem, vmem_limit_bytes=_VMEM_LIMIT)


def _const_spec(shape):
    nd = len(shape)
    return pl.BlockSpec(shape, lambda *_: (0,) * nd, pipeline_mode=pl.Buffered(1))


def _whole_spec(shape):
    nd = len(shape)
    return pl.BlockSpec(shape, lambda *_: (0,) * nd)


def _sel(stacked, idx):
    if stacked.ndim == 2:
        stacked = stacked[:, None, :]
    tail = stacked.shape[1:]
    spec = pl.BlockSpec((None,) + tail, lambda *_: (idx,) + (0,) * len(tail), pipeline_mode=pl.Buffered(1))
    return stacked, spec


def _chunk_rows(rows, cols):
    cands = [r for r in range(2 * _SUBLANE, rows + 1, 2 * _SUBLANE)
             if rows % r == 0 and r * cols * 4 <= _STAGE_BYTES]
    return max(cands)


def _weight_scratch(rows, cols):
    return [pltpu.VMEM((rows, cols), BF16), pltpu.VMEM((_STAGE_SLOTS, _chunk_rows(rows, cols), cols), F32),
            pltpu.SemaphoreType.DMA((_STAGE_SLOTS,))]


def _fetch_bf16(*jobs):
    rings = {}
    for w_hbm, idx, dst, stage, sem in jobs:
        slots, rc = stage.shape[:2]
        ring = rings.setdefault(id(stage), [])
        for r0 in range(0, dst.shape[0], rc):
            slot = len(ring) % slots
            copy = pltpu.make_async_copy(w_hbm.at[idx, pl.ds(r0, rc), :], stage.at[slot], sem.at[slot])
            ring.append((copy, dst, r0, rc, stage, slot))
    rings = list(rings.values())
    ahead = [ring[0][4].shape[0] - 1 for ring in rings]
    for ring, n in zip(rings, ahead):
        for copy, *_ in ring[:n]:
            copy.start()
    for k in range(max(len(ring) for ring in rings)):
        for ring, n in zip(rings, ahead):
            if k >= len(ring):
                continue
            if k + n < len(ring):
                ring[k + n][0].start()
            copy, dst, r0, rc, stage, slot = ring[k]
            copy.wait()
            dst[r0:r0 + rc, :] = stage[slot].astype(BF16)


def _dot(a, b):
    return jnp.dot(a, b, preferred_element_type=F32)


def _rms(x, g):
    y = x * lax.rsqrt(jnp.mean(x * x, axis=-1, keepdims=True) + _EPS)
    return y * g


def _layernorm(x, g, b):
    mu = jnp.mean(x, axis=-1, keepdims=True)
    d = x - mu
    var = jnp.mean(d * d, axis=-1, keepdims=True)
    return d * lax.rsqrt(var + _EPS) * g + b


def _silu(x):
    return x * jax.nn.sigmoid(x)


def _softmax_rows(s):
    m = jnp.max(s, axis=-1, keepdims=True)
    e = jnp.exp(s - m)
    return e * (1.0 / jnp.sum(e, axis=-1, keepdims=True))


_DONE = object()


def _run_staggered(seqs):
    live = []
    pending = list(seqs)
    while live or pending:
        if pending:
            live.append(pending.pop(0))
        live = [s for s in live if next(s, _DONE) is not _DONE]


def _kv_out_copies(kbuf, vbuf, k_hbm, v_hbm, sem, layer, blk, slot):
    _, heads, nbk = kbuf.shape[:3]
    rows = pl.ds(blk * nbk, nbk)
    cps = []
    for hd in range(heads):
        cps.append(pltpu.make_async_copy(kbuf.at[slot, hd], k_hbm.at[layer, rows, :, hd, :], sem.at[slot, 0, hd]))
        cps.append(pltpu.make_async_copy(vbuf.at[slot, hd], v_hbm.at[layer, rows, :, hd, :], sem.at[slot, 1, hd]))
    return cps


def _kv_kernel(m_ref, wk_hbm, wv_hbm, k_hbm, v_hbm, kbuf, vbuf, sem, wk_ref, wv_ref, stage, wsem):
    nj = pl.num_programs(1)
    layer = pl.program_id(0)
    blk = pl.program_id(1)
    step = layer * nj + blk
    last = pl.num_programs(0) * nj - 1
    slot = lax.rem(step, 2)
    _, heads, nbk, mem, dh = kbuf.shape
    copies = functools.partial(_kv_out_copies, kbuf, vbuf, k_hbm, v_hbm, sem)

    @pl.when(blk == 0)
    def _():
        _fetch_bf16((wk_hbm, layer, wk_ref, stage, wsem), (wv_hbm, layer, wv_ref, stage, wsem))

    @pl.when(step >= 2)
    def _():
        prev = step - 2
        for cp in copies(lax.div(prev, nj), lax.rem(prev, nj), slot):
            cp.wait()

    m = m_ref[...].reshape(nbk * mem, m_ref.shape[-1]).astype(BF16)
    k = _dot(m, wk_ref[...])
    v = _dot(m, wv_ref[...])
    for hd in range(heads):
        for b in range(nbk):
            kbuf[slot, hd, b] = k[b * mem:(b + 1) * mem, hd * dh:(hd + 1) * dh]
            vbuf[slot, hd, b] = v[b * mem:(b + 1) * mem, hd * dh:(hd + 1) * dh]
    for cp in copies(layer, blk, slot):
        cp.start()

    @pl.when(step == last)
    def _():
        @pl.when(step >= 1)
        def _():
            prev = step - 1
            for cp in copies(lax.div(prev, nj), lax.rem(prev, nj), 1 - slot):
                cp.wait()
        for cp in copies(layer, blk, slot):
            cp.wait()


def _kv_proj(mem3, wk, wv, heads):
    nb, mem, d = mem3.shape
    depth = wk.shape[0]
    dh = d // heads
    nbk = _KV_BATCHES
    assert nb % nbk == 0 and mem % _SUBLANE == 0
    any_spec = pl.BlockSpec(memory_space=pl.ANY)
    buf = pltpu.VMEM((2, heads, nbk, mem, dh), F32)
    wk_scr, stage, wsem = _weight_scratch(d, d)
    return pl.pallas_call(
        _kv_kernel,
        grid=(depth, nb // nbk),
        in_specs=[pl.BlockSpec((nbk, mem, d), lambda l, j: (j, 0, 0)), any_spec, any_spec],
        out_specs=[any_spec, any_spec],
        out_shape=[jax.ShapeDtypeStruct((depth, nb, mem, heads, dh), F32)] * 2,
        scratch_shapes=[buf, buf, pltpu.SemaphoreType.DMA((2, 2, heads)), wk_scr, wk_scr, stage, wsem],
        compiler_params=_params("arbitrary", "arbitrary"),
        name="kv_proj",
    )(mem3, wk, wv)


def _sattn_copies(k_hbm, v_hbm, cbuf, sem, layer, row0, slot):
    _, _, heads, hb = cbuf.shape[:4]
    rows = pl.ds(row0, hb)
    cps = []
    for j, src in enumerate((k_hbm, v_hbm)):
        for hd in range(heads):
            cps.append(pltpu.make_async_copy(src.at[layer, rows, :, hd, :], cbuf.at[slot, j, hd], sem.at[slot, j, hd]))
    return cps


def _sample_attn_rows(qs_ref, o_scr, cbuf, slot, row0):
    _, _, heads, hb, _, dh = cbuf.shape
    for j in range(hb):
        outs = []
        for hd in range(heads):
            q_row = qs_ref[pl.ds(row0 + j, 1), hd * dh:(hd + 1) * dh]
            s = jnp.sum(cbuf[slot, 0, hd, j] * q_row, axis=1, keepdims=True)
            e = jnp.exp(s - jnp.max(s, axis=0, keepdims=True))
            den = jnp.sum(e, axis=0, keepdims=True)
            num = jnp.sum(e * cbuf[slot, 1, hd, j], axis=0, keepdims=True)
            outs.append(num * (1.0 / den))
            yield
        o_scr[pl.ds(row0 + j, 1), :] = jnp.concatenate(outs, axis=1)


def _ffn_kernel(xp_ref, xs_ref, qs_ref, g_ref, gf_ref, ck_hbm, cv_hbm, wg_hbm, wu_hbm, wd_hbm, wo_hbm,
                op_ref, os_ref, wg_ref, wu_ref, up_stage, up_sem, wd_ref, dn_stage, dn_sem, wo_ref,
                o_scr, cbuf, csem, *, layer, chunks, final_norm):
    step = pl.program_id(0)
    n_tiles = pl.num_programs(0) - 1
    slots, hb = cbuf.shape[0], cbuf.shape[3]
    per_step = slots * hb
    ccopies = functools.partial(_sattn_copies, ck_hbm, cv_hbm, cbuf, csem, layer)

    @pl.when(step == 0)
    def _():
        for sl in range(slots):
            for cp in ccopies(sl * hb, sl):
                cp.start()
        _fetch_bf16((wg_hbm, layer, wg_ref, up_stage, up_sem), (wu_hbm, layer, wu_ref, up_stage, up_sem),
                    (wd_hbm, layer, wd_ref, dn_stage, dn_sem), (wo_hbm, layer, wo_ref, dn_stage, dn_sem))

    def swiglu(x, o_ref):
        h = _rms(x, g_ref[...]).astype(BF16)
        yield
        acc = x
        lo = 0
        for width in chunks:
            gate = _dot(h, wg_ref[:, lo:lo + width])
            up = _dot(h, wu_ref[:, lo:lo + width])
            yield
            act = (_silu(gate) * up).astype(BF16)
            yield
            acc = acc + _dot(act, wd_ref[lo:lo + width, :])
            yield
            lo += width
        if final_norm:
            acc = _rms(acc, gf_ref[...])
        o_ref[...] = acc

    def sample_rows():
        nxt = jnp.minimum(step + 1, n_tiles - 1) * per_step
        for sl in range(slots):
            row0 = step * per_step + sl * hb
            for cp in ccopies(row0, sl):
                cp.wait()
            yield from _sample_attn_rows(qs_ref, o_scr, cbuf, sl, row0)
            for cp in ccopies(nxt + sl * hb, sl):
                cp.start()
            yield

    @pl.when(step < n_tiles)
    def _():
        _run_staggered([swiglu(xp_ref[...], op_ref), sample_rows()])

    @pl.when(step == n_tiles)
    def _():
        for sl in range(slots):
            for cp in ccopies((n_tiles - 1) * per_step + sl * hb, sl):
                cp.wait()
        x = xs_ref[...] + _dot(o_scr[...].astype(BF16), wo_ref[...])
        _run_staggered([swiglu(x, os_ref)])


def _ffn(xp2d, xs2d, qs2d, ck5, cv5, wo, g, wg, wu, wd, gf, layer, tm, final_norm):
    rows, d = xp2d.shape
    b = xs2d.shape[0]
    ff = wg.shape[-1]
    _, _, mem_len, heads, dh = ck5.shape
    n_tiles = rows // tm
    slots = 2
    assert sum(_FF_CHUNKS) == ff and rows % tm == 0 and b % (n_tiles * slots) == 0
    hb = b // (n_tiles * slots)
    kern = functools.partial(_ffn_kernel, layer=layer, chunks=_FF_CHUNKS, final_norm=final_norm)
    row_spec = pl.BlockSpec((tm, d), lambda i: (jnp.minimum(i, n_tiles - 1), 0))
    any_spec = pl.BlockSpec(memory_space=pl.ANY)
    ops, specs = zip(_sel(g, layer), _sel(gf, 0))
    wg_scr, up_stage, up_sem = _weight_scratch(d, ff)
    wd_scr, dn_stage, dn_sem = _weight_scratch(ff, d)
    assert dn_stage.shape[1:] == _weight_scratch(d, d)[1].shape[1:]
    return pl.pallas_call(
        kern,
        grid=(n_tiles + 1,),
        in_specs=[row_spec, _const_spec((b, d)), _const_spec((b, d)), *specs] + [any_spec] * 6,
        out_specs=[row_spec, _whole_spec((b, d))],
        out_shape=[jax.ShapeDtypeStruct((rows, d), F32), jax.ShapeDtypeStruct((b, d), F32)],
        scratch_shapes=[wg_scr, wg_scr, up_stage, up_sem, wd_scr, dn_stage, dn_sem, pltpu.VMEM((d, d), BF16),
                        pltpu.VMEM((b, d), F32), pltpu.VMEM((slots, 2, heads, hb, mem_len, dh), F32),
                        pltpu.SemaphoreType.DMA((slots, 2, heads))],
        compiler_params=_params("arbitrary"),
        name="ffn",
    )(xp2d, xs2d, qs2d, *ops, ck5, cv5, wg, wu, wd, wo)


def _pattn_kv_copies(k_hbm, v_hbm, kvbuf, sem, layer, n, slot):
    heads = kvbuf.shape[2]
    cps = []
    for j, src in enumerate((k_hbm, v_hbm)):
        for hd in range(heads):
            cps.append(pltpu.make_async_copy(src.at[layer, n, :, hd, :], kvbuf.at[slot, j, hd], sem.at[slot, j, hd]))
    return cps


def _attn_kernel(xp_ref, xs_ref, g_ref, pk_hbm, pv_hbm, wq_hbm, wo_hbm, op_ref, qs_ref,
                 kvbuf, kv16, psem, wq_ref, wo_ref, stage, wsem, *, layer, parts):
    n = pl.program_id(0)
    i = pl.program_id(1)
    heads, _, dh = kv16.shape[1:]
    tm = xp_ref.shape[0]
    rows = tm // parts
    pslot = lax.rem(n, 2)
    pcopies = functools.partial(_pattn_kv_copies, pk_hbm, pv_hbm, kvbuf, psem, layer)

    @pl.when(jnp.logical_and(n == 0, i == 0))
    def _():
        for cp in pcopies(0, 0):
            cp.start()
        _fetch_bf16((wq_hbm, layer, wq_ref, stage, wsem), (wo_hbm, layer, wo_ref, stage, wsem))
        h = _rms(xs_ref[...], g_ref[...]).astype(BF16)
        qs_ref[...] = _dot(h, wq_ref[...]) * (dh ** -0.5)

    @pl.when(i == 0)
    def _():
        @pl.when(n + 1 < pl.num_programs(0))
        def _():
            for cp in pcopies(n + 1, 1 - pslot):
                cp.start()

        for cp in pcopies(n, pslot):
            cp.wait()
        kv16[...] = kvbuf[pslot].astype(BF16)

    def attend(r0):
        x = xp_ref[r0:r0 + rows, :]
        h = _rms(x, g_ref[...]).astype(BF16)
        yield
        q = _dot(h, wq_ref[...]).astype(BF16)
        yield
        outs = []
        for hd in range(heads):
            s = lax.dot_general(q[:, hd * dh:(hd + 1) * dh], kv16[0, hd], (((1,), (1,)), ((), ())),
                                preferred_element_type=F32)
            yield
            a = _softmax_rows(s * (dh ** -0.5)).astype(BF16)
            yield
            outs.append(_dot(a, kv16[1, hd]))
            yield
        o = jnp.concatenate(outs, axis=1).astype(BF16)
        op_ref[r0:r0 + rows, :] = x + _dot(o, wo_ref[...])
        yield

    _run_staggered([attend(p * rows) for p in range(parts)])


def _attn(xp, xs, g, wq, pk5, pv5, wo, layer, tm):
    nb, t, d = xp.shape
    b = xs.shape[0]
    _, _, mem_len, heads, dh = pk5.shape
    assert t % tm == 0 and tm % (_ATTN_PARTS * _SUBLANE) == 0
    x_spec = pl.BlockSpec((None, tm, d), lambda n, i: (n, i, 0))
    any_spec = pl.BlockSpec(memory_space=pl.ANY)
    g, g_spec = _sel(g, layer)
    wq_scr, stage, wsem = _weight_scratch(d, d)
    return pl.pallas_call(
        functools.partial(_attn_kernel, layer=layer, parts=_ATTN_PARTS),
        grid=(nb, t // tm),
        in_specs=[x_spec, _const_spec((b, d)), g_spec] + [any_spec] * 4,
        out_specs=[x_spec, _whole_spec((b, d))],
        out_shape=[jax.ShapeDtypeStruct((nb, t, d), F32), jax.ShapeDtypeStruct((b, d), F32)],
        scratch_shapes=[pltpu.VMEM((2, 2, heads, mem_len, dh), F32), pltpu.VMEM((2, heads, mem_len, dh), BF16),
                        pltpu.SemaphoreType.DMA((2, 2, heads)), wq_scr, wq_scr, stage, wsem],
        compiler_params=_params("arbitrary", "arbitrary"),
        name="attn",
    )(xp, xs, g, pk5, pv5, wq, wo)


def _peven_kernel(x_ref, g_ref, cw_ref, cb_ref, lg_ref, lb_ref, ps_ref, win_hbm, wout_hbm, pw_hbm,
                  xo_ref, sa_ref, sb_ref, za_ext, zb_ext, ca_buf,
                  win_ref, win_stage, win_sem, wout_ref, wout_stage, wout_sem, pw_ref, pw_stage, pw_sem,
                  *, idx, tm, d_a, ha, hb):
    t = pl.program_id(1)
    ka = cw_ref.shape[0]
    nca = za_ext.shape[0]
    ncb = zb_ext.shape[0]
    pb = sb_ref.shape[0]

    @pl.when(jnp.logical_and(pl.program_id(0) == 0, t == 0))
    def _():
        _fetch_bf16((win_hbm, idx, win_ref, win_stage, win_sem), (wout_hbm, idx, wout_ref, wout_stage, wout_sem),
                    (pw_hbm, idx, pw_ref, pw_stage, pw_sem))

    @pl.when(t == 0)
    def _():
        za_ext[:, 0:ha, :] = jnp.zeros((nca, ha, _LANE), F32)
        zb_ext[:, 0:hb, :] = jnp.zeros((ncb, hb, _LANE), F32)

    x = x_ref[...]
    h = _rms(x, g_ref[...]).astype(BF16)
    p = _dot(h, win_ref[...])
    za = p[:, :d_a] * jax.nn.sigmoid(p[:, d_a:2 * d_a])
    zb = p[:, 2 * d_a:]
    for c in range(nca):
        za_ext[c, ha:ha + tm, :] = za[:, c * _LANE:(c + 1) * _LANE]
    for c in range(ncb):
        zb_ext[c, hb:hb + tm, :] = zb[:, c * _LANE:(c + 1) * _LANE]

    base = ha - (ka - 1)
    for c in range(nca):
        ls = slice(c * _LANE, (c + 1) * _LANE)
        for r in range(tm // _CONV_ROWS):
            r0 = r * _CONV_ROWS
            acc = jnp.broadcast_to(cb_ref[:, ls], (_CONV_ROWS, _LANE))
            for k in range(ka):
                acc = acc + cw_ref[k:k + 1, ls] * za_ext[c, base + r0 + k:base + r0 + k + _CONV_ROWS, :]
            ca_buf[r0:r0 + _CONV_ROWS, ls] = acc
    ya = _silu(_layernorm(ca_buf[...], lg_ref[...], lb_ref[...]))

    pos = t * tm + lax.broadcasted_iota(jnp.int32, (tm, 1), 0)
    parts = []
    for gi, w in enumerate(_POOL_WINDOWS):
        zg = zb[:, gi * _LANE:(gi + 1) * _LANE]
        s = zg
        for j in range(1, w):
            s = s + zb_ext[gi, hb - j:hb - j + tm, :]
        inv_cnt = 1.0 / jnp.minimum(pos + 1, w).astype(F32)
        pooled = (s * inv_cnt - zg).astype(BF16)
        parts.append(_dot(pooled, pw_ref[gi * _LANE:(gi + 1) * _LANE, :]))
    yb = jnp.concatenate(parts, axis=1) * ps_ref[...]

    y = jnp.concatenate([ya, yb], axis=1).astype(BF16)
    xo_ref[...] = x + _dot(y, wout_ref[...])

    @pl.when(t == pl.num_programs(1) - 1)
    def _():
        for c in range(nca):
            sa_ref[:, c * _LANE:(c + 1) * _LANE] = za_ext[c, ha + tm - (ka - 1):ha + tm, :]
        for c in range(ncb):
            sb_ref[:, c * _LANE:(c + 1) * _LANE] = zb_ext[c, hb + tm - pb:hb + tm, :]

    za_ext[:, 0:ha, :] = za_ext[:, tm:tm + ha, :]
    zb_ext[:, 0:hb, :] = zb_ext[:, tm:tm + hb, :]


def _prompt_even(x, params, big, idx, tm):
    nb, t, d = x.shape
    ops, specs = zip(*params)
    win, wout, pw = big
    ka, d_a = ops[1].shape[-2:]
    d_b = ops[5].shape[-1]
    pb = max(_POOL_WINDOWS) - 1
    ha = _round_up(ka - 1, _SUBLANE)
    hb = _round_up(pb, _SUBLANE)
    assert t % tm == 0 and tm % _CONV_ROWS == 0 and tm >= max(ha, hb)
    assert d_a % _LANE == 0 and d_b == _LANE * len(_POOL_WINDOWS)
    x_spec = pl.BlockSpec((None, tm, d), lambda n, i: (n, i, 0))
    any_spec = pl.BlockSpec(memory_space=pl.ANY)
    kern = functools.partial(_peven_kernel, idx=idx, tm=tm, d_a=d_a, ha=ha, hb=hb)
    return pl.pallas_call(
        kern,
        grid=(nb, t // tm),
        in_specs=[x_spec, *specs, any_spec, any_spec, any_spec],
        out_specs=[x_spec,
                   pl.BlockSpec((None, ka - 1, d_a), lambda n, i: (n, 0, 0)),
                   pl.BlockSpec((None, pb, d_b), lambda n, i: (n, 0, 0))],
        out_shape=[jax.ShapeDtypeStruct((nb, t, d), F32),
                   jax.ShapeDtypeStruct((nb, ka - 1, d_a), F32),
                   jax.ShapeDtypeStruct((nb, pb, d_b), F32)],
        scratch_shapes=[pltpu.VMEM((d_a // _LANE, ha + tm, _LANE), F32),
                        pltpu.VMEM((d_b // _LANE, hb + tm, _LANE), F32),
                        pltpu.VMEM((tm, d_a), F32),
                        *_weight_scratch(*win.shape[1:]), *_weight_scratch(*wout.shape[1:]),
                        *_weight_scratch(*pw.shape[1:])],
        compiler_params=_params("arbitrary", "arbitrary"),
        name="prompt_even",
    )(x, *ops, win, wout, pw)


def _seven_kernel(x_ref, g_ref, cw_ref, cb_ref, lg_ref, lb_ref, ps_ref, ha_ref, hb_ref, win_hbm, wout_hbm, pw_hbm,
                  xo_ref, sa_ref, sb_ref,
                  win_ref, win_stage, win_sem, wout_ref, wout_stage, wout_sem, pw_ref, pw_stage, pw_sem, *, idx):
    _fetch_bf16((win_hbm, idx, win_ref, win_stage, win_sem), (wout_hbm, idx, wout_ref, wout_stage, wout_sem),
                (pw_hbm, idx, pw_ref, pw_stage, pw_sem))
    ka, d_a = cw_ref.shape
    pb = hb_ref.shape[0]
    d_b = hb_ref.shape[2]
    dg = d_b // len(_POOL_WINDOWS)
    x = x_ref[...]
    h = _rms(x, g_ref[...]).astype(BF16)
    p = _dot(h, win_ref[...])
    za = p[:, :d_a] * jax.nn.sigmoid(p[:, d_a:2 * d_a])
    zb = p[:, 2 * d_a:]

    acc = cb_ref[...] + cw_ref[ka - 1:ka, :] * za
    for k in range(ka - 1):
        acc = acc + cw_ref[k:k + 1, :] * ha_ref[k]
    ya = _silu(_layernorm(acc, lg_ref[...], lb_ref[...]))

    parts = []
    for gi, w in enumerate(_POOL_WINDOWS):
        ls = slice(gi * dg, (gi + 1) * dg)
        s = zb[:, ls]
        for j in range(1, w):
            s = s + hb_ref[pb - j, :, ls]
        inv_cnt = 1.0 / float(min(_PAST_LEN + 1, w))
        pooled = (s * inv_cnt - zb[:, ls]).astype(BF16)
        parts.append(_dot(pooled, pw_ref[gi * dg:(gi + 1) * dg, :]))
    yb = jnp.concatenate(parts, axis=1) * ps_ref[...]

    y = jnp.concatenate([ya, yb], axis=1).astype(BF16)
    xo_ref[...] = x + _dot(y, wout_ref[...])

    sa_ref[0:ka - 2] = ha_ref[1:ka - 1]
    sa_ref[ka - 2] = za
    sb_ref[0:pb - 1] = hb_ref[1:pb]
    sb_ref[pb - 1] = zb


def _sample_even(x2d, params, big, idx, hist_a, hist_b):
    b, d = x2d.shape
    ops, specs = zip(*params)
    win, wout, pw = big
    any_spec = pl.BlockSpec(memory_space=pl.ANY)
    return pl.pallas_call(
        functools.partial(_seven_kernel, idx=idx),
        grid=(1,),
        in_specs=[_const_spec((b, d)), *specs, _const_spec(hist_a.shape), _const_spec(hist_b.shape),
                  any_spec, any_spec, any_spec],
        out_specs=[_whole_spec((b, d)), _whole_spec(hist_a.shape), _whole_spec(hist_b.shape)],
        out_shape=[jax.ShapeDtypeStruct((b, d), F32),
                   jax.ShapeDtypeStruct(hist_a.shape, F32),
                   jax.ShapeDtypeStruct(hist_b.shape, F32)],
        scratch_shapes=[*_weight_scratch(*win.shape[1:]), *_weight_scratch(*wout.shape[1:]),
                        *_weight_scratch(*pw.shape[1:])],
        compiler_params=_params("arbitrary"),
        name="sample_even",
    )(x2d, *ops, hist_a, hist_b, win, wout, pw)


def _podd_kernel(x_ref, g_ref, lg_ref, lb_ref, ws_ref, bst_ref, cw_ref, win_hbm, wout_hbm,
                 xo_ref, sv_ref, sd_ref, g_ext, win_ref, win_stage, win_sem, wout_ref, wout_stage, wout_sem,
                 *, idx, tm, d_c, hd_rows):
    t = pl.program_id(1)
    kd, d_d = cw_ref.shape
    heads = ws_ref.shape[0]
    dh = d_c // heads
    ncd = g_ext.shape[0]

    @pl.when(jnp.logical_and(pl.program_id(0) == 0, t == 0))
    def _():
        _fetch_bf16((win_hbm, idx, win_ref, win_stage, win_sem), (wout_hbm, idx, wout_ref, wout_stage, wout_sem))

    @pl.when(t == 0)
    def _():
        g_ext[:, 0:hd_rows, :] = jnp.zeros((ncd, hd_rows, _LANE), F32)

    x = x_ref[...]
    h = _rms(x, g_ref[...]).astype(BF16)
    p = _dot(h, win_ref[...])
    u = p[:, :d_c]
    v = _layernorm(p[:, d_c:2 * d_c], lg_ref[...], lb_ref[...])
    o = 2 * d_c
    gb = p[:, o:o + d_d]
    gated = p[:, o + d_d:o + 2 * d_d] * p[:, o + 2 * d_d:]

    ri = lax.broadcasted_iota(jnp.int32, (_CHUNK, _CHUNK), 0)
    ci = lax.broadcasted_iota(jnp.int32, (_CHUNK, _CHUNK), 1)
    wm = [jnp.where(ci <= ri, ws_ref[hh], 0.0).astype(BF16) for hh in range(heads)]
    vb = v.astype(BF16)
    rows = []
    for c in range(tm // _CHUNK):
        rs = slice(c * _CHUNK, (c + 1) * _CHUNK)
        parts = [_dot(wm[hh], vb[rs, hh * dh:(hh + 1) * dh]) + bst_ref[:, hh:hh + 1] for hh in range(heads)]
        rows.append(jnp.concatenate(parts, axis=1))
    yc = u * jnp.concatenate(rows, axis=0)

    base = hd_rows - (kd - 1)
    cols = []
    for c in range(ncd):
        ls = slice(c * _LANE, (c + 1) * _LANE)
        g_ext[c, hd_rows:hd_rows + tm, :] = gated[:, ls]
        cd = cw_ref[0:1, ls] * g_ext[c, base:base + tm, :]
        for k in range(1, kd):
            cd = cd + cw_ref[k:k + 1, ls] * g_ext[c, base + k:base + k + tm, :]
        cols.append(cd)
    yd = gb * jnp.concatenate(cols, axis=1)

    y = jnp.concatenate([yc, yd], axis=1).astype(BF16)
    xo_ref[...] = x + _dot(y, wout_ref[...])

    @pl.when(t == pl.num_programs(1) - 1)
    def _():
        n_open = sv_ref.shape[0]
        sv_ref[...] = v[tm - n_open:, :]
        for c in range(ncd):
            sd_ref[:, c * _LANE:(c + 1) * _LANE] = g_ext[c, hd_rows + tm - (kd - 1):hd_rows + tm, :]

    g_ext[:, 0:hd_rows, :] = g_ext[:, tm:tm + hd_rows, :]


def _prompt_odd(x, params, big, idx, tm):
    nb, t, d = x.shape
    ops, specs = zip(*params)
    win, wout = big
    d_c = ops[1].shape[-1]
    kd, d_d = ops[5].shape[-2:]
    n_open = t - ((t - 1) // _CHUNK) * _CHUNK
    hd_rows = _round_up(kd - 1, _SUBLANE)
    assert t % tm == 0 and tm % _CHUNK == 0 and n_open % _SUBLANE == 0 and n_open <= tm
    x_spec = pl.BlockSpec((None, tm, d), lambda n, i: (n, i, 0))
    any_spec = pl.BlockSpec(memory_space=pl.ANY)
    kern = functools.partial(_podd_kernel, idx=idx, tm=tm, d_c=d_c, hd_rows=hd_rows)
    return pl.pallas_call(
        kern,
        grid=(nb, t // tm),
        in_specs=[x_spec, *specs, any_spec, any_spec],
        out_specs=[x_spec,
                   pl.BlockSpec((None, n_open, d_c), lambda n, i: (n, 0, 0)),
                   pl.BlockSpec((None, kd - 1, d_d), lambda n, i: (n, 0, 0))],
        out_shape=[jax.ShapeDtypeStruct((nb, t, d), F32),
                   jax.ShapeDtypeStruct((nb, n_open, d_c), F32),
                   jax.ShapeDtypeStruct((nb, kd - 1, d_d), F32)],
        scratch_shapes=[pltpu.VMEM((d_d // _LANE, hd_rows + tm, _LANE), F32),
                        *_weight_scratch(*win.shape[1:]), *_weight_scratch(*wout.shape[1:])],
        compiler_params=_params("arbitrary", "arbitrary"),
        name="prompt_odd",
    )(x, *ops, win, wout)


def _sodd_kernel(x_ref, g_ref, lg_ref, lb_ref, ws0_ref, bs0_ref, cw_ref, hd_ref, win_hbm, wout_hbm,
                 xo_ref, sv_ref, sd_ref, win_ref, win_stage, win_sem, wout_ref, wout_stage, wout_sem, *, idx):
    _fetch_bf16((win_hbm, idx, win_ref, win_stage, win_sem), (wout_hbm, idx, wout_ref, wout_stage, wout_sem))
    kd, d_d = cw_ref.shape
    d_c = lg_ref.shape[1]
    x = x_ref[...]
    h = _rms(x, g_ref[...]).astype(BF16)
    p = _dot(h, win_ref[...])
    u = p[:, :d_c]
    v = _layernorm(p[:, d_c:2 * d_c], lg_ref[...], lb_ref[...])
    o = 2 * d_c
    gb = p[:, o:o + d_d]
    gated = p[:, o + d_d:o + 2 * d_d] * p[:, o + 2 * d_d:]

    yc = u * (ws0_ref[...] * v + bs0_ref[...])

    cd = cw_ref[kd - 1:kd, :] * gated
    for k in range(kd - 1):
        cd = cd + cw_ref[k:k + 1, :] * hd_ref[:, k, :]
    yd = gb * cd

    y = jnp.concatenate([yc, yd], axis=1).astype(BF16)
    xo_ref[...] = x + _dot(y, wout_ref[...])

    sv_ref[...] = v
    for k in range(kd - 2):
        sd_ref[:, k, :] = hd_ref[:, k + 1, :]
    sd_ref[:, kd - 2, :] = gated


def _sample_odd(x2d, params, big, idx, ws0, bs0, hist_d):
    b, d = x2d.shape
    ops, specs = zip(*params)
    win, wout = big
    d_c = ops[1].shape[-1]
    ops = ops[:3] + (ws0, bs0) + ops[5:]
    specs = specs[:3] + (_const_spec(ws0.shape), _const_spec(bs0.shape)) + specs[5:]
    any_spec = pl.BlockSpec(memory_space=pl.ANY)
    return pl.pallas_call(
        functools.partial(_sodd_kernel, idx=idx),
        grid=(1,),
        in_specs=[_const_spec((b, d)), *specs, _const_spec(hist_d.shape), any_spec, any_spec],
        out_specs=[_whole_spec((b, d)), _whole_spec((b, d_c)), _whole_spec(hist_d.shape)],
        out_shape=[jax.ShapeDtypeStruct((b, d), F32),
                   jax.ShapeDtypeStruct((b, d_c), F32),
                   jax.ShapeDtypeStruct(hist_d.shape, F32)],
        scratch_shapes=[*_weight_scratch(*win.shape[1:]), *_weight_scratch(*wout.shape[1:])],
        compiler_params=_params("arbitrary"),
        name="sample_odd",
    )(x2d, *ops, hist_d, win, wout)


def kernel(x_prompt, x_sample, mem_prompt, state_convA, state_poolB, state_convD, cache_mem_k, cache_mem_v, norm_mix, norm_x, norm_ffn, norm_final, w_in_even, conv_a_w, conv_a_b, ln_a_g, ln_a_b, pool_b_w, pool_b_scale, w_out_even, w_in_odd, ln_c_g, ln_c_b, ws_c, bs_c, conv_d_w, w_out_odd, wq_x, wk_x, wv_x, wo_x, w_gate, w_up, w_down):
    nb, seq, d = x_prompt.shape
    db, dec_seq, _ = x_sample.shape
    assert dec_seq == 1
    depth = wq_x.shape[0]
    heads = cache_mem_k.shape[3]

    def unstack(parts):
        return parts[0][None] if len(parts) == 1 else jnp.stack(parts)

    g_fin = norm_final[None]
    pool_w2d = pool_b_w.reshape(pool_b_w.shape[0], -1, pool_b_w.shape[-1])

    p_k, p_v = _kv_proj(mem_prompt, wk_x, wv_x, heads)

    xp = x_prompt
    xs = x_sample.reshape(db, d)
    p_a, p_b, p_c, p_d = [], [], [], []
    s_a, s_b, s_c, s_d = [], [], [], []
    for layer in range(depth):
        i = layer // 2
        if layer % 2 == 0:
            params = [_sel(norm_mix, layer), _sel(conv_a_w, i), _sel(conv_a_b, i), _sel(ln_a_g, i), _sel(ln_a_b, i),
                      _sel(pool_b_scale, i)]
            big = (w_in_even, w_out_even, pool_w2d)
            xp, a, b = _prompt_even(xp, params, big, i, tm=_PROMPT_TILE)
            p_a.append(a)
            p_b.append(b)
            xs, a, b = _sample_even(xs, params, big, i, state_convA[i].transpose(1, 0, 2),
                                    state_poolB[i].transpose(1, 0, 2))
            s_a.append(a.transpose(1, 0, 2))
            s_b.append(b.transpose(1, 0, 2))
        else:
            params = [_sel(norm_mix, layer), _sel(ln_c_g, i), _sel(ln_c_b, i), _sel(ws_c, i),
                      _sel(bs_c.transpose(0, 2, 1), i), _sel(conv_d_w, i)]
            big = (w_in_odd, w_out_odd)
            xp, c, dd = _prompt_odd(xp, params, big, i, tm=_PROMPT_TILE)
            dc = c.shape[-1]
            p_c.append(c.reshape(nb, -1, _C_HEADS, dc // _C_HEADS))
            p_d.append(dd)
            ws0 = jnp.repeat(ws_c[i, :, 0, 0], dc // _C_HEADS)[None]
            bs0 = jnp.repeat(bs_c[i, :, 0], dc // _C_HEADS)[None]
            xs, c, dd = _sample_odd(xs, params, big, i, ws0, bs0, state_convD[i])
            s_c.append(c.reshape(db, 1, _C_HEADS, dc // _C_HEADS))
            s_d.append(dd)

        xp, qs = _attn(xp, xs, norm_x, wq_x, p_k, p_v, wo_x, layer, tm=_ATTN_TILE)

        xp2d, xs = _ffn(xp.reshape(nb * seq, d), xs, qs, cache_mem_k, cache_mem_v, wo_x, norm_ffn, w_gate, w_up,
                        w_down, g_fin, layer, tm=_FFN_TILE, final_norm=layer == depth - 1)
        xp = xp2d.reshape(nb, seq, d)

    return (xp, xs.reshape(db, 1, d), unstack(p_a), unstack(p_b), unstack(p_c), unstack(p_d), p_k, p_v,
            unstack(s_a), unstack(s_b), unstack(s_c), unstack(s_d))
```

```python
import functools

import jax
import jax.numpy as jnp
from jax import lax
from jax.experimental import pallas as pl
from jax.experimental.pallas import tpu as pltpu

F32 = jnp.float32
BF16 = jnp.bfloat16

_EPS = 1e-6
_X_HEADS = 4
_C_HEADS = 4
_CHUNK = 128
_POOL_WINDOWS = (2, 4, 8, 16)
_PAST_LEN = 16384

_SUBLANE = 8
_LANE = 128
_VMEM_LIMIT = 56 * 1024 * 1024
_STAGE_BYTES = 512 * 1024
_STAGE_SLOTS = 4

_PROMPT_TILE = 1024
_ATTN_TILE = 1024
_ATTN_PARTS = 2
_FFN_TILE = 512
_KV_BATCHES = 2
_FF_CHUNKS = (768, 768, 768, 512)
_CONV_ROWS = 16


def _round_up(n, m):
    return -(-n // m) * m


def _params(*sem):
    return pltpu.CompilerParams(dimension_semantics=sem, vmem_limit_bytes=_VMEM_LIMIT)


def _const_spec(shape):
    nd = len(shape)
    return pl.BlockSpec(shape, lambda *_: (0,) * nd, pipeline_mode=pl.Buffered(1))


def _whole_spec(shape):
    nd = len(shape)
    return pl.BlockSpec(shape, lambda *_: (0,) * nd)


def _sel(stacked, idx):
    if stacked.ndim == 2:
        stacked = stacked[:, None, :]
    tail = stacked.shape[1:]
    spec = pl.BlockSpec((None,) + tail, lambda *_: (idx,) + (0,) * len(tail), pipeline_mode=pl.Buffered(1))
    return stacked, spec


def _chunk_rows(rows, cols):
    cands = [r for r in range(2 * _SUBLANE, rows + 1, 2 * _SUBLANE)
             if rows % r == 0 and r * cols * 4 <= _STAGE_BYTES]
    return max(cands)


def _weight_scratch(rows, cols):
    return [pltpu.VMEM((rows, cols), BF16), pltpu.VMEM((_STAGE_SLOTS, _chunk_rows(rows, cols), cols), F32),
            pltpu.SemaphoreType.DMA((_STAGE_SLOTS,))]


def _fetch_bf16(*jobs):
    rings = {}
    for w_hbm, idx, dst, stage, sem in jobs:
        slots, rc = stage.shape[:2]
        ring = rings.setdefault(id(stage), [])
        for r0 in range(0, dst.shape[0], rc):
            slot = len(ring) % slots
            copy = pltpu.make_async_copy(w_hbm.at[idx, pl.ds(r0, rc), :], stage.at[slot], sem.at[slot])
            ring.append((copy, dst, r0, rc, stage, slot))
    rings = list(rings.values())
    ahead = [ring[0][4].shape[0] - 1 for ring in rings]
    for ring, n in zip(rings, ahead):
        for copy, *_ in ring[:n]:
            copy.start()
    for k in range(max(len(ring) for ring in rings)):
        for ring, n in zip(rings, ahead):
            if k >= len(ring):
                continue
            if k + n < len(ring):
                ring[k + n][0].start()
            copy, dst, r0, rc, stage, slot = ring[k]
            copy.wait()
            dst[r0:r0 + rc, :] = stage[slot].astype(BF16)


def _dot(a, b):
    return jnp.dot(a, b, preferred_element_type=F32)


def _rms(x, g):
    y = x * lax.rsqrt(jnp.mean(x * x, axis=-1, keepdims=True) + _EPS)
    return y * g


def _layernorm(x, g, b):
    mu = jnp.mean(x, axis=-1, keepdims=True)
    d = x - mu
    var = jnp.mean(d * d, axis=-1, keepdims=True)
    return d * lax.rsqrt(var + _EPS) * g + b


def _silu(x):
    return x * jax.nn.sigmoid(x)


def _softmax_rows(s):
    m = jnp.max(s, axis=-1, keepdims=True)
    e = jnp.exp(s - m)
    return e * (1.0 / jnp.sum(e, axis=-1, keepdims=True))


_DONE = object()


def _run_staggered(seqs):
    live = []
    pending = list(seqs)
    while live or pending:
        if pending:
            live.append(pending.pop(0))
        live = [s for s in live if next(s, _DONE) is not _DONE]


def _kv_out_copies(kbuf, vbuf, k_hbm, v_hbm, sem, layer, blk, slot):
    _, heads, nbk = kbuf.shape[:3]
    rows = pl.ds(blk * nbk, nbk)
    cps = []
    for hd in range(heads):
        cps.append(pltpu.make_async_copy(kbuf.at[slot, hd], k_hbm.at[layer, rows, :, hd, :], sem.at[slot, 0, hd]))
        cps.append(pltpu.make_async_copy(vbuf.at[slot, hd], v_hbm.at[layer, rows, :, hd, :], sem.at[slot, 1, hd]))
    return cps


def _kv_kernel(m_ref, wk_hbm, wv_hbm, k_hbm, v_hbm, kbuf, vbuf, sem, wk_ref, wv_ref, stage, wsem):
    nj = pl.num_programs(1)
    layer = pl.program_id(0)
    blk = pl.program_id(1)
    step = layer * nj + blk
    last = pl.num_programs(0) * nj - 1
    slot = lax.rem(step, 2)
    _, heads, nbk, mem, dh = kbuf.shape
    copies = functools.partial(_kv_out_copies, kbuf, vbuf, k_hbm, v_hbm, sem)

    @pl.when(blk == 0)
    def _():
        _fetch_bf16((wk_hbm, layer, wk_ref, stage, wsem), (wv_hbm, layer, wv_ref, stage, wsem))

    @pl.when(step >= 2)
    def _():
        prev = step - 2
        for cp in copies(lax.div(prev, nj), lax.rem(prev, nj), slot):
            cp.wait()

    m = m_ref[...].reshape(nbk * mem, m_ref.shape[-1]).astype(BF16)
    k = _dot(m, wk_ref[...])
    v = _dot(m, wv_ref[...])
    for hd in range(heads):
        for b in range(nbk):
            kbuf[slot, hd, b] = k[b * mem:(b + 1) * mem, hd * dh:(hd + 1) * dh]
            vbuf[slot, hd, b] = v[b * mem:(b + 1) * mem, hd * dh:(hd + 1) * dh]
    for cp in copies(layer, blk, slot):
        cp.start()

    @pl.when(step == last)
    def _():
        @pl.when(step >= 1)
        def _():
            prev = step - 1
            for cp in copies(lax.div(prev, nj), lax.rem(prev, nj), 1 - slot):
                cp.wait()
        for cp in copies(layer, blk, slot):
            cp.wait()


def _kv_proj(mem3, wk, wv, heads):
    nb, mem, d = mem3.shape
    depth = wk.shape[0]
    dh = d // heads
    nbk = _KV_BATCHES
    assert nb % nbk == 0 and mem % _SUBLANE == 0
    any_spec = pl.BlockSpec(memory_space=pl.ANY)
    buf = pltpu.VMEM((2, heads, nbk, mem, dh), F32)
    wk_scr, stage, wsem = _weight_scratch(d, d)
    return pl.pallas_call(
        _kv_kernel,
        grid=(depth, nb // nbk),
        in_specs=[pl.BlockSpec((nbk, mem, d), lambda l, j: (j, 0, 0)), any_spec, any_spec],
        out_specs=[any_spec, any_spec],
        out_shape=[jax.ShapeDtypeStruct((depth, nb, mem, heads, dh), F32)] * 2,
        scratch_shapes=[buf, buf, pltpu.SemaphoreType.DMA((2, 2, heads)), wk_scr, wk_scr, stage, wsem],
        compiler_params=_params("arbitrary", "arbitrary"),
        name="kv_proj",
    )(mem3, wk, wv)


def _sattn_copies(k_hbm, v_hbm, cbuf, sem, layer, row0, slot):
    _, _, heads, hb = cbuf.shape[:4]
    rows = pl.ds(row0, hb)
    cps = []
    for j, src in enumerate((k_hbm, v_hbm)):
        for hd in range(heads):
            cps.append(pltpu.make_async_copy(src.at[layer, rows, :, hd, :], cbuf.at[slot, j, hd], sem.at[slot, j, hd]))
    return cps


def _sample_attn_rows(qs_ref, o_scr, cbuf, slot, row0):
    _, _, heads, hb, _, dh = cbuf.shape
    for j in range(hb):
        outs = []
        for hd in range(heads):
            q_row = qs_ref[pl.ds(row0 + j, 1), hd * dh:(hd + 1) * dh]
            s = jnp.sum(cbuf[slot, 0, hd, j] * q_row, axis=1, keepdims=True)
            e = jnp.exp(s - jnp.max(s, axis=0, keepdims=True))
            den = jnp.sum(e, axis=0, keepdims=True)
            num = jnp.sum(e * cbuf[slot, 1, hd, j], axis=0, keepdims=True)
            outs.append(num * (1.0 / den))
            yield
        o_scr[pl.ds(row0 + j, 1), :] = jnp.concatenate(outs, axis=1)


def _ffn_kernel(xp_ref, xs_ref, qs_ref, g_ref, gf_ref, ck_hbm, cv_hbm, wg_hbm, wu_hbm, wd_hbm, wo_hbm,
                op_ref, os_ref, wg_ref, wu_ref, up_stage, up_sem, wd_ref, dn_stage, dn_sem, wo_ref,
                o_scr, cbuf, csem, *, layer, chunks, final_norm):
    step = pl.program_id(0)
    n_tiles = pl.num_programs(0) - 1
    slots, hb = cbuf.shape[0], cbuf.shape[3]
    cslot = lax.rem(step, slots)
    ccopies = functools.partial(_sattn_copies, ck_hbm, cv_hbm, cbuf, csem, layer)

    @pl.when(step == 0)
    def _():
        for sl in range(slots):
            for cp in ccopies(sl * hb, sl):
                cp.start()
        _fetch_bf16((wg_hbm, layer, wg_ref, up_stage, up_sem), (wu_hbm, layer, wu_ref, up_stage, up_sem),
                    (wd_hbm, layer, wd_ref, dn_stage, dn_sem), (wo_hbm, layer, wo_ref, dn_stage, dn_sem))

    def swiglu(x, o_ref):
        h = _rms(x, g_ref[...]).astype(BF16)
        yield
        acc = x
        lo = 0
        for width in chunks:
            gate = _dot(h, wg_ref[:, lo:lo + width])
            up = _dot(h, wu_ref[:, lo:lo + width])
            yield
            act = (_silu(gate) * up).astype(BF16)
            yield
            acc = acc + _dot(act, wd_ref[lo:lo + width, :])
            yield
            lo += width
        if final_norm:
            acc = _rms(acc, gf_ref[...])
        o_ref[...] = acc

    def sample_rows():
        yield from _sample_attn_rows(qs_ref, o_scr, cbuf, cslot, step * hb)

    @pl.when(step < n_tiles)
    def _():
        for cp in ccopies(step * hb, cslot):
            cp.wait()
        _run_staggered([swiglu(xp_ref[...], op_ref), sample_rows()])
        for cp in ccopies(jnp.minimum(step + 2, n_tiles - 1) * hb, cslot):
            cp.start()

    @pl.when(step == n_tiles)
    def _():
        for sl in range(slots):
            for cp in ccopies((n_tiles - 1) * hb, sl):
                cp.wait()
        x = xs_ref[...] + _dot(o_scr[...].astype(BF16), wo_ref[...])
        _run_staggered([swiglu(x, os_ref)])


def _ffn(xp2d, xs2d, qs2d, ck5, cv5, wo, g, wg, wu, wd, gf, layer, tm, final_norm):
    rows, d = xp2d.shape
    b = xs2d.shape[0]
    ff = wg.shape[-1]
    _, _, mem_len, heads, dh = ck5.shape
    n_tiles = rows // tm
    slots = 2
    assert sum(_FF_CHUNKS) == ff and rows % tm == 0 and b % n_tiles == 0
    hb = b // n_tiles
    kern = functools.partial(_ffn_kernel, layer=layer, chunks=_FF_CHUNKS, final_norm=final_norm)
    row_spec = pl.BlockSpec((tm, d), lambda i: (jnp.minimum(i, n_tiles - 1), 0))
    any_spec = pl.BlockSpec(memory_space=pl.ANY)
    ops, specs = zip(_sel(g, layer), _sel(gf, 0))
    wg_scr, up_stage, up_sem = _weight_scratch(d, ff)
    wd_scr, dn_stage, dn_sem = _weight_scratch(ff, d)
    assert dn_stage.shape[1:] == _weight_scratch(d, d)[1].shape[1:]
    return pl.pallas_call(
        kern,
        grid=(n_tiles + 1,),
        in_specs=[row_spec, _const_spec((b, d)), _const_spec((b, d)), *specs] + [any_spec] * 6,
        out_specs=[row_spec, _whole_spec((b, d))],
        out_shape=[jax.ShapeDtypeStruct((rows, d), F32), jax.ShapeDtypeStruct((b, d), F32)],
        scratch_shapes=[wg_scr, wg_scr, up_stage, up_sem, wd_scr, dn_stage, dn_sem, pltpu.VMEM((d, d), BF16),
                        pltpu.VMEM((b, d), F32), pltpu.VMEM((slots, 2, heads, hb, mem_len, dh), F32),
                        pltpu.SemaphoreType.DMA((slots, 2, heads))],
        compiler_params=_params("arbitrary"),
        name="ffn",
    )(xp2d, xs2d, qs2d, *ops, ck5, cv5, wg, wu, wd, wo)


def _pattn_kv_copies(k_hbm, v_hbm, kvbuf, sem, layer, n, slot):
    heads = kvbuf.shape[2]
    cps = []
    for j, src in enumerate((k_hbm, v_hbm)):
        for hd in range(heads):
            cps.append(pltpu.make_async_copy(src.at[layer, n, :, hd, :], kvbuf.at[slot, j, hd], sem.at[slot, j, hd]))
    return cps


def _attn_kernel(xp_ref, xs_ref, g_ref, pk_hbm, pv_hbm, wq_hbm, wo_hbm, op_ref, qs_ref,
                 kvbuf, kv16, psem, wq_ref, wo_ref, stage, wsem, *, layer, parts):
    n = pl.program_id(0)
    i = pl.program_id(1)
    heads, _, dh = kv16.shape[1:]
    tm = xp_ref.shape[0]
    rows = tm // parts
    pslot = lax.rem(n, 2)
    pcopies = functools.partial(_pattn_kv_copies, pk_hbm, pv_hbm, kvbuf, psem, layer)

    @pl.when(jnp.logical_and(n == 0, i == 0))
    def _():
        for cp in pcopies(0, 0):
            cp.start()
        _fetch_bf16((wq_hbm, layer, wq_ref, stage, wsem), (wo_hbm, layer, wo_ref, stage, wsem))
        h = _rms(xs_ref[...], g_ref[...]).astype(BF16)
        qs_ref[...] = _dot(h, wq_ref[...]) * (dh ** -0.5)

    @pl.when(i == 0)
    def _():
        @pl.when(n + 1 < pl.num_programs(0))
        def _():
            for cp in pcopies(n + 1, 1 - pslot):
                cp.start()

        for cp in pcopies(n, pslot):
            cp.wait()
        kv16[...] = kvbuf[pslot].astype(BF16)

    def attend(r0):
        x = xp_ref[r0:r0 + rows, :]
        h = _rms(x, g_ref[...]).astype(BF16)
        yield
        q = _dot(h, wq_ref[...]).astype(BF16)
        yield
        outs = []
        for hd in range(heads):
            s = lax.dot_general(q[:, hd * dh:(hd + 1) * dh], kv16[0, hd], (((1,), (1,)), ((), ())),
                                preferred_element_type=F32)
            yield
            a = _softmax_rows(s * (dh ** -0.5)).astype(BF16)
            yield
            outs.append(_dot(a, kv16[1, hd]))
            yield
        o = jnp.concatenate(outs, axis=1).astype(BF16)
        op_ref[r0:r0 + rows, :] = x + _dot(o, wo_ref[...])
        yield

    _run_staggered([attend(p * rows) for p in range(parts)])


def _attn(xp, xs, g, wq, pk5, pv5, wo, layer, tm):
    nb, t, d = xp.shape
    b = xs.shape[0]
    _, _, mem_len, heads, dh = pk5.shape
    assert t % tm == 0 and tm % (_ATTN_PARTS * _SUBLANE) == 0
    x_spec = pl.BlockSpec((None, tm, d), lambda n, i: (n, i, 0))
    any_spec = pl.BlockSpec(memory_space=pl.ANY)
    g, g_spec = _sel(g, layer)
    wq_scr, stage, wsem = _weight_scratch(d, d)
    return pl.pallas_call(
        functools.partial(_attn_kernel, layer=layer, parts=_ATTN_PARTS),
        grid=(nb, t // tm),
        in_specs=[x_spec, _const_spec((b, d)), g_spec] + [any_spec] * 4,
        out_specs=[x_spec, _whole_spec((b, d))],
        out_shape=[jax.ShapeDtypeStruct((nb, t, d), F32), jax.ShapeDtypeStruct((b, d), F32)],
        scratch_shapes=[pltpu.VMEM((2, 2, heads, mem_len, dh), F32), pltpu.VMEM((2, heads, mem_len, dh), BF16),
                        pltpu.SemaphoreType.DMA((2, 2, heads)), wq_scr, wq_scr, stage, wsem],
        compiler_params=_params("arbitrary", "arbitrary"),
        name="attn",
    )(xp, xs, g, pk5, pv5, wq, wo)


def _peven_kernel(x_ref, g_ref, cw_ref, cb_ref, lg_ref, lb_ref, ps_ref, win_hbm, wout_hbm, pw_hbm,
                  xo_ref, sa_ref, sb_ref, za_ext, zb_ext, ca_buf,
                  win_ref, win_stage, win_sem, wout_ref, wout_stage, wout_sem, pw_ref, pw_stage, pw_sem,
                  *, idx, tm, d_a, ha, hb):
    t = pl.program_id(1)
    ka = cw_ref.shape[0]
    nca = za_ext.shape[0]
    ncb = zb_ext.shape[0]
    pb = sb_ref.shape[0]

    @pl.when(jnp.logical_and(pl.program_id(0) == 0, t == 0))
    def _():
        _fetch_bf16((win_hbm, idx, win_ref, win_stage, win_sem), (wout_hbm, idx, wout_ref, wout_stage, wout_sem),
                    (pw_hbm, idx, pw_ref, pw_stage, pw_sem))

    @pl.when(t == 0)
    def _():
        za_ext[:, 0:ha, :] = jnp.zeros((nca, ha, _LANE), F32)
        zb_ext[:, 0:hb, :] = jnp.zeros((ncb, hb, _LANE), F32)

    x = x_ref[...]
    h = _rms(x, g_ref[...]).astype(BF16)
    p = _dot(h, win_ref[...])
    za = p[:, :d_a] * jax.nn.sigmoid(p[:, d_a:2 * d_a])
    zb = p[:, 2 * d_a:]
    for c in range(nca):
        za_ext[c, ha:ha + tm, :] = za[:, c * _LANE:(c + 1) * _LANE]
    for c in range(ncb):
        zb_ext[c, hb:hb + tm, :] = zb[:, c * _LANE:(c + 1) * _LANE]

    base = ha - (ka - 1)
    for c in range(nca):
        ls = slice(c * _LANE, (c + 1) * _LANE)
        for r in range(tm // _CONV_ROWS):
            r0 = r * _CONV_ROWS
            acc = jnp.broadcast_to(cb_ref[:, ls], (_CONV_ROWS, _LANE))
            for k in range(ka):
                acc = acc + cw_ref[k:k + 1, ls] * za_ext[c, base + r0 + k:base + r0 + k + _CONV_ROWS, :]
            ca_buf[r0:r0 + _CONV_ROWS, ls] = acc
    ya = _silu(_layernorm(ca_buf[...], lg_ref[...], lb_ref[...]))

    pos = t * tm + lax.broadcasted_iota(jnp.int32, (tm, 1), 0)
    parts = []
    for gi, w in enumerate(_POOL_WINDOWS):
        zg = zb[:, gi * _LANE:(gi + 1) * _LANE]
        s = zg
        for j in range(1, w):
            s = s + zb_ext[gi, hb - j:hb - j + tm, :]
        inv_cnt = 1.0 / jnp.minimum(pos + 1, w).astype(F32)
        pooled = (s * inv_cnt - zg).astype(BF16)
        parts.append(_dot(pooled, pw_ref[gi * _LANE:(gi + 1) * _LANE, :]))
    yb = jnp.concatenate(parts, axis=1) * ps_ref[...]

    y = jnp.concatenate([ya, yb], axis=1).astype(BF16)
    xo_ref[...] = x + _dot(y, wout_ref[...])

    @pl.when(t == pl.num_programs(1) - 1)
    def _():
        for c in range(nca):
            sa_ref[:, c * _LANE:(c + 1) * _LANE] = za_ext[c, ha + tm - (ka - 1):ha + tm, :]
        for c in range(ncb):
            sb_ref[:, c * _LANE:(c + 1) * _LANE] = zb_ext[c, hb + tm - pb:hb + tm, :]

    za_ext[:, 0:ha, :] = za_ext[:, tm:tm + ha, :]
    zb_ext[:, 0:hb, :] = zb_ext[:, tm:tm + hb, :]


def _prompt_even(x, params, big, idx, tm):
    nb, t, d = x.shape
    ops, specs = zip(*params)
    win, wout, pw = big
    ka, d_a = ops[1].shape[-2:]
    d_b = ops[5].shape[-1]
    pb = max(_POOL_WINDOWS) - 1
    ha = _round_up(ka - 1, _SUBLANE)
    hb = _round_up(pb, _SUBLANE)
    assert t % tm == 0 and tm % _CONV_ROWS == 0 and tm >= max(ha, hb)
    assert d_a % _LANE == 0 and d_b == _LANE * len(_POOL_WINDOWS)
    x_spec = pl.BlockSpec((None, tm, d), lambda n, i: (n, i, 0))
    any_spec = pl.BlockSpec(memory_space=pl.ANY)
    kern = functools.partial(_peven_kernel, idx=idx, tm=tm, d_a=d_a, ha=ha, hb=hb)
    return pl.pallas_call(
        kern,
        grid=(nb, t // tm),
        in_specs=[x_spec, *specs, any_spec, any_spec, any_spec],
        out_specs=[x_spec,
                   pl.BlockSpec((None, ka - 1, d_a), lambda n, i: (n, 0, 0)),
                   pl.BlockSpec((None, pb, d_b), lambda n, i: (n, 0, 0))],
        out_shape=[jax.ShapeDtypeStruct((nb, t, d), F32),
                   jax.ShapeDtypeStruct((nb, ka - 1, d_a), F32),
                   jax.ShapeDtypeStruct((nb, pb, d_b), F32)],
        scratch_shapes=[pltpu.VMEM((d_a // _LANE, ha + tm, _LANE), F32),
                        pltpu.VMEM((d_b // _LANE, hb + tm, _LANE), F32),
                        pltpu.VMEM((tm, d_a), F32),
                        *_weight_scratch(*win.shape[1:]), *_weight_scratch(*wout.shape[1:]),
                        *_weight_scratch(*pw.shape[1:])],
        compiler_params=_params("arbitrary", "arbitrary"),
        name="prompt_even",
    )(x, *ops, win, wout, pw)


def _seven_kernel(x_ref, g_ref, cw_ref, cb_ref, lg_ref, lb_ref, ps_ref, ha_ref, hb_ref, win_hbm, wout_hbm, pw_hbm,
                  xo_ref, sa_ref, sb_ref,
                  win_ref, win_stage, win_sem, wout_ref, wout_stage, wout_sem, pw_ref, pw_stage, pw_sem, *, idx):
    _fetch_bf16((win_hbm, idx, win_ref, win_stage, win_sem), (wout_hbm, idx, wout_ref, wout_stage, wout_sem),
                (pw_hbm, idx, pw_ref, pw_stage, pw_sem))
    ka, d_a = cw_ref.shape
    pb = hb_ref.shape[0]
    d_b = hb_ref.shape[2]
    dg = d_b // len(_POOL_WINDOWS)
    x = x_ref[...]
    h = _rms(x, g_ref[...]).astype(BF16)
    p = _dot(h, win_ref[...])
    za = p[:, :d_a] * jax.nn.sigmoid(p[:, d_a:2 * d_a])
    zb = p[:, 2 * d_a:]

    acc = cb_ref[...] + cw_ref[ka - 1:ka, :] * za
    for k in range(ka - 1):
        acc = acc + cw_ref[k:k + 1, :] * ha_ref[k]
    ya = _silu(_layernorm(acc, lg_ref[...], lb_ref[...]))

    parts = []
    for gi, w in enumerate(_POOL_WINDOWS):
        ls = slice(gi * dg, (gi + 1) * dg)
        s = zb[:, ls]
        for j in range(1, w):
            s = s + hb_ref[pb - j, :, ls]
        inv_cnt = 1.0 / float(min(_PAST_LEN + 1, w))
        pooled = (s * inv_cnt - zb[:, ls]).astype(BF16)
        parts.append(_dot(pooled, pw_ref[gi * dg:(gi + 1) * dg, :]))
    yb = jnp.concatenate(parts, axis=1) * ps_ref[...]

    y = jnp.concatenate([ya, yb], axis=1).astype(BF16)
    xo_ref[...] = x + _dot(y, wout_ref[...])

    sa_ref[0:ka - 2] = ha_ref[1:ka - 1]
    sa_ref[ka - 2] = za
    sb_ref[0:pb - 1] = hb_ref[1:pb]
    sb_ref[pb - 1] = zb


def _sample_even(x2d, params, big, idx, hist_a, hist_b):
    b, d = x2d.shape
    ops, specs = zip(*params)
    win, wout, pw = big
    any_spec = pl.BlockSpec(memory_space=pl.ANY)
    return pl.pallas_call(
        functools.partial(_seven_kernel, idx=idx),
        grid=(1,),
        in_specs=[_const_spec((b, d)), *specs, _const_spec(hist_a.shape), _const_spec(hist_b.shape),
                  any_spec, any_spec, any_spec],
        out_specs=[_whole_spec((b, d)), _whole_spec(hist_a.shape), _whole_spec(hist_b.shape)],
        out_shape=[jax.ShapeDtypeStruct((b, d), F32),
                   jax.ShapeDtypeStruct(hist_a.shape, F32),
                   jax.ShapeDtypeStruct(hist_b.shape, F32)],
        scratch_shapes=[*_weight_scratch(*win.shape[1:]), *_weight_scratch(*wout.shape[1:]),
                        *_weight_scratch(*pw.shape[1:])],
        compiler_params=_params("arbitrary"),
        name="sample_even",
    )(x2d, *ops, hist_a, hist_b, win, wout, pw)


def _podd_kernel(x_ref, g_ref, lg_ref, lb_ref, ws_ref, bst_ref, cw_ref, win_hbm, wout_hbm,
                 xo_ref, sv_ref, sd_ref, g_ext, win_ref, win_stage, win_sem, wout_ref, wout_stage, wout_sem,
                 *, idx, tm, d_c, hd_rows):
    t = pl.program_id(1)
    kd, d_d = cw_ref.shape
    heads = ws_ref.shape[0]
    dh = d_c // heads
    ncd = g_ext.shape[0]

    @pl.when(jnp.logical_and(pl.program_id(0) == 0, t == 0))
    def _():
        _fetch_bf16((win_hbm, idx, win_ref, win_stage, win_sem), (wout_hbm, idx, wout_ref, wout_stage, wout_sem))

    @pl.when(t == 0)
    def _():
        g_ext[:, 0:hd_rows, :] = jnp.zeros((ncd, hd_rows, _LANE), F32)

    x = x_ref[...]
    h = _rms(x, g_ref[...]).astype(BF16)
    p = _dot(h, win_ref[...])
    u = p[:, :d_c]
    v = _layernorm(p[:, d_c:2 * d_c], lg_ref[...], lb_ref[...])
    o = 2 * d_c
    gb = p[:, o:o + d_d]
    gated = p[:, o + d_d:o + 2 * d_d] * p[:, o + 2 * d_d:]

    ri = lax.broadcasted_iota(jnp.int32, (_CHUNK, _CHUNK), 0)
    ci = lax.broadcasted_iota(jnp.int32, (_CHUNK, _CHUNK), 1)
    wm = [jnp.where(ci <= ri, ws_ref[hh], 0.0).astype(BF16) for hh in range(heads)]
    vb = v.astype(BF16)
    rows = []
    for c in range(tm // _CHUNK):
        rs = slice(c * _CHUNK, (c + 1) * _CHUNK)
        parts = [_dot(wm[hh], vb[rs, hh * dh:(hh + 1) * dh]) + bst_ref[:, hh:hh + 1] for hh in range(heads)]
        rows.append(jnp.concatenate(parts, axis=1))
    yc = u * jnp.concatenate(rows, axis=0)

    base = hd_rows - (kd - 1)
    cols = []
    for c in range(ncd):
        ls = slice(c * _LANE, (c + 1) * _LANE)
        g_ext[c, hd_rows:hd_rows + tm, :] = gated[:, ls]
        cd = cw_ref[0:1, ls] * g_ext[c, base:base + tm, :]
        for k in range(1, kd):
            cd = cd + cw_ref[k:k + 1, ls] * g_ext[c, base + k:base + k + tm, :]
        cols.append(cd)
    yd = gb * jnp.concatenate(cols, axis=1)

    y = jnp.concatenate([yc, yd], axis=1).astype(BF16)
    xo_ref[...] = x + _dot(y, wout_ref[...])

    @pl.when(t == pl.num_programs(1) - 1)
    def _():
        n_open = sv_ref.shape[0]
        sv_ref[...] = v[tm - n_open:, :]
        for c in range(ncd):
            sd_ref[:, c * _LANE:(c + 1) * _LANE] = g_ext[c, hd_rows + tm - (kd - 1):hd_rows + tm, :]

    g_ext[:, 0:hd_rows, :] = g_ext[:, tm:tm + hd_rows, :]


def _prompt_odd(x, params, big, idx, tm):
    nb, t, d = x.shape
    ops, specs = zip(*params)
    win, wout = big
    d_c = ops[1].shape[-1]
    kd, d_d = ops[5].shape[-2:]
    n_open = t - ((t - 1) // _CHUNK) * _CHUNK
    hd_rows = _round_up(kd - 1, _SUBLANE)
    assert t % tm == 0 and tm % _CHUNK == 0 and n_open % _SUBLANE == 0 and n_open <= tm
    x_spec = pl.BlockSpec((None, tm, d), lambda n, i: (n, i, 0))
    any_spec = pl.BlockSpec(memory_space=pl.ANY)
    kern = functools.partial(_podd_kernel, idx=idx, tm=tm, d_c=d_c, hd_rows=hd_rows)
    return pl.pallas_call(
        kern,
        grid=(nb, t // tm),
        in_specs=[x_spec, *specs, any_spec, any_spec],
        out_specs=[x_spec,
                   pl.BlockSpec((None, n_open, d_c), lambda n, i: (n, 0, 0)),
                   pl.BlockSpec((None, kd - 1, d_d), lambda n, i: (n, 0, 0))],
        out_shape=[jax.ShapeDtypeStruct((nb, t, d), F32),
                   jax.ShapeDtypeStruct((nb, n_open, d_c), F32),
                   jax.ShapeDtypeStruct((nb, kd - 1, d_d), F32)],
        scratch_shapes=[pltpu.VMEM((d_d // _LANE, hd_rows + tm, _LANE), F32),
                        *_weight_scratch(*win.shape[1:]), *_weight_scratch(*wout.shape[1:])],
        compiler_params=_params("arbitrary", "arbitrary"),
        name="prompt_odd",
    )(x, *ops, win, wout)


def _sodd_kernel(x_ref, g_ref, lg_ref, lb_ref, ws0_ref, bs0_ref, cw_ref, hd_ref, win_hbm, wout_hbm,
                 xo_ref, sv_ref, sd_ref, win_ref, win_stage, win_sem, wout_ref, wout_stage, wout_sem, *, idx):
    _fetch_bf16((win_hbm, idx, win_ref, win_stage, win_sem), (wout_hbm, idx, wout_ref, wout_stage, wout_sem))
    kd, d_d = cw_ref.shape
    d_c = lg_ref.shape[1]
    x = x_ref[...]
    h = _rms(x, g_ref[...]).astype(BF16)
    p = _dot(h, win_ref[...])
    u = p[:, :d_c]
    v = _layernorm(p[:, d_c:2 * d_c], lg_ref[...], lb_ref[...])
    o = 2 * d_c
    gb = p[:, o:o + d_d]
    gated = p[:, o + d_d:o + 2 * d_d] * p[:, o + 2 * d_d:]

    yc = u * (ws0_ref[...] * v + bs0_ref[...])

    cd = cw_ref[kd - 1:kd, :] * gated
    for k in range(kd - 1):
        cd = cd + cw_ref[k:k + 1, :] * hd_ref[:, k, :]
    yd = gb * cd

    y = jnp.concatenate([yc, yd], axis=1).astype(BF16)
    xo_ref[...] = x + _dot(y, wout_ref[...])

    sv_ref[...] = v
    for k in range(kd - 2):
        sd_ref[:, k, :] = hd_ref[:, k + 1, :]
    sd_ref[:, kd - 2, :] = gated


def _sample_odd(x2d, params, big, idx, ws0, bs0, hist_d):
    b, d = x2d.shape
    ops, specs = zip(*params)
    win, wout = big
    d_c = ops[1].shape[-1]
    ops = ops[:3] + (ws0, bs0) + ops[5:]
    specs = specs[:3] + (_const_spec(ws0.shape), _const_spec(bs0.shape)) + specs[5:]
    any_spec = pl.BlockSpec(memory_space=pl.ANY)
    return pl.pallas_call(
        functools.partial(_sodd_kernel, idx=idx),
        grid=(1,),
        in_specs=[_const_spec((b, d)), *specs, _const_spec(hist_d.shape), any_spec, any_spec],
        out_specs=[_whole_spec((b, d)), _whole_spec((b, d_c)), _whole_spec(hist_d.shape)],
        out_shape=[jax.ShapeDtypeStruct((b, d), F32),
                   jax.ShapeDtypeStruct((b, d_c), F32),
                   jax.ShapeDtypeStruct(hist_d.shape, F32)],
        scratch_shapes=[*_weight_scratch(*win.shape[1:]), *_weight_scratch(*wout.shape[1:])],
        compiler_params=_params("arbitrary"),
        name="sample_odd",
    )(x2d, *ops, hist_d, win, wout)


def kernel(x_prompt, x_sample, mem_prompt, state_convA, state_poolB, state_convD, cache_mem_k, cache_mem_v, norm_mix, norm_x, norm_ffn, norm_final, w_in_even, conv_a_w, conv_a_b, ln_a_g, ln_a_b, pool_b_w, pool_b_scale, w_out_even, w_in_odd, ln_c_g, ln_c_b, ws_c, bs_c, conv_d_w, w_out_odd, wq_x, wk_x, wv_x, wo_x, w_gate, w_up, w_down):
    nb, seq, d = x_prompt.shape
    db, dec_seq, _ = x_sample.shape
    assert dec_seq == 1
    depth = wq_x.shape[0]
    heads = cache_mem_k.shape[3]

    def unstack(parts):
        return parts[0][None] if len(parts) == 1 else jnp.stack(parts)

    g_fin = norm_final[None]
    pool_w2d = pool_b_w.reshape(pool_b_w.shape[0], -1, pool_b_w.shape[-1])

    p_k, p_v = _kv_proj(mem_prompt, wk_x, wv_x, heads)

    xp = x_prompt
    xs = x_sample.reshape(db, d)
    p_a, p_b, p_c, p_d = [], [], [], []
    s_a, s_b, s_c, s_d = [], [], [], []
    for layer in range(depth):
        i = layer // 2
        if layer % 2 == 0:
            params = [_sel(norm_mix, layer), _sel(conv_a_w, i), _sel(conv_a_b, i), _sel(ln_a_g, i), _sel(ln_a_b, i),
                      _sel(pool_b_scale, i)]
            big = (w_in_even, w_out_even, pool_w2d)
            xp, a, b = _prompt_even(xp, params, big, i, tm=_PROMPT_TILE)
            p_a.append(a)
            p_b.append(b)
            xs, a, b = _sample_even(xs, params, big, i, state_convA[i].transpose(1, 0, 2),
                                    state_poolB[i].transpose(1, 0, 2))
            s_a.append(a.transpose(1, 0, 2))
            s_b.append(b.transpose(1, 0, 2))
        else:
            params = [_sel(norm_mix, layer), _sel(ln_c_g, i), _sel(ln_c_b, i), _sel(ws_c, i),
                      _sel(bs_c.transpose(0, 2, 1), i), _sel(conv_d_w, i)]
            big = (w_in_odd, w_out_odd)
            xp, c, dd = _prompt_odd(xp, params, big, i, tm=_PROMPT_TILE)
            dc = c.shape[-1]
            p_c.append(c.reshape(nb, -1, _C_HEADS, dc // _C_HEADS))
            p_d.append(dd)
            ws0 = jnp.repeat(ws_c[i, :, 0, 0], dc // _C_HEADS)[None]
            bs0 = jnp.repeat(bs_c[i, :, 0], dc // _C_HEADS)[None]
            xs, c, dd = _sample_odd(xs, params, big, i, ws0, bs0, state_convD[i])
            s_c.append(c.reshape(db, 1, _C_HEADS, dc // _C_HEADS))
            s_d.append(dd)

        xp, qs = _attn(xp, xs, norm_x, wq_x, p_k, p_v, wo_x, layer, tm=_ATTN_TILE)

        xp2d, xs = _ffn(xp.reshape(nb * seq, d), xs, qs, cache_mem_k, cache_mem_v, wo_x, norm_ffn, w_gate, w_up,
                        w_down, g_fin, layer, tm=_FFN_TILE, final_norm=layer == depth - 1)
        xp = xp2d.reshape(nb, seq, d)

    return (xp, xs.reshape(db, 1, d), unstack(p_a), unstack(p_b), unstack(p_c), unstack(p_d), p_k, p_v,
            unstack(s_a), unstack(s_b), unstack(s_c), unstack(s_d))
```

```python
import functools

import jax
import jax.numpy as jnp
from jax import lax
from jax.experimental import pallas as pl
from jax.experimental.pallas import tpu as pltpu

F32 = jnp.float32
BF16 = jnp.bfloat16

_EPS = 1e-6
_X_HEADS = 4
_C_HEADS = 4
_CHUNK = 128
_POOL_WINDOWS = (2, 4, 8, 16)
_PAST_LEN = 16384

_SUBLANE = 8
_LANE = 128
_VMEM_LIMIT = 56 * 1024 * 1024
_STAGE_BYTES = 512 * 1024
_STAGE_SLOTS = 8

_PROMPT_TILE = 1024
_ATTN_TILE = 1024
_ATTN_PARTS = 2
_FFN_TILE = 512
_KV_BATCHES = 2
_FF_CHUNKS = (768, 768, 768, 512)
_CONV_ROWS = 16


def _round_up(n, m):
    return -(-n // m) * m


def _params(*sem):
    return pltpu.CompilerParams(dimension_semantics=sem, vmem_limit_bytes=_VMEM_LIMIT)


def _const_spec(shape):
    nd = len(shape)
    return pl.BlockSpec(shape, lambda *_: (0,) * nd, pipeline_mode=pl.Buffered(1))


def _whole_spec(shape):
    nd = len(shape)
    return pl.BlockSpec(shape, lambda *_: (0,) * nd)


def _sel(stacked, idx):
    if stacked.ndim == 2:
        stacked = stacked[:, None, :]
    tail = stacked.shape[1:]
    spec = pl.BlockSpec((None,) + tail, lambda *_: (idx,) + (0,) * len(tail), pipeline_mode=pl.Buffered(1))
    return stacked, spec


def _chunk_rows(rows, cols):
    cands = [r for r in range(2 * _SUBLANE, rows + 1, 2 * _SUBLANE)
             if rows % r == 0 and r * cols * 4 <= _STAGE_BYTES]
    return max(cands)


def _weight_scratch(rows, cols):
    return [pltpu.VMEM((rows, cols), BF16), pltpu.VMEM((_STAGE_SLOTS, _chunk_rows(rows, cols), cols), F32),
            pltpu.SemaphoreType.DMA((_STAGE_SLOTS,))]


def _fetch_bf16(*jobs):
    rings = {}
    for w_hbm, idx, dst, stage, sem in jobs:
        slots, rc = stage.shape[:2]
        ring = rings.setdefault(id(stage), [])
        for r0 in range(0, dst.shape[0], rc):
            slot = len(ring) % slots
            copy = pltpu.make_async_copy(w_hbm.at[idx, pl.ds(r0, rc), :], stage.at[slot], sem.at[slot])
            ring.append((copy, dst, r0, rc, stage, slot))
    rings = list(rings.values())
    ahead = [ring[0][4].shape[0] - 1 for ring in rings]
    for ring, n in zip(rings, ahead):
        for copy, *_ in ring[:n]:
            copy.start()
    for k in range(max(len(ring) for ring in rings)):
        for ring, n in zip(rings, ahead):
            if k >= len(ring):
                continue
            if k + n < len(ring):
                ring[k + n][0].start()
            copy, dst, r0, rc, stage, slot = ring[k]
            copy.wait()
            dst[r0:r0 + rc, :] = stage[slot].astype(BF16)


def _dot(a, b):
    return jnp.dot(a, b, preferred_element_type=F32)


def _rms(x, g):
    y = x * lax.rsqrt(jnp.mean(x * x, axis=-1, keepdims=True) + _EPS)
    return y * g


def _layernorm(x, g, b):
    mu = jnp.mean(x, axis=-1, keepdims=True)
    d = x - mu
    var = jnp.mean(d * d, axis=-1, keepdims=True)
    return d * lax.rsqrt(var + _EPS) * g + b


def _silu(x):
    return x * jax.nn.sigmoid(x)


def _softmax_rows(s):
    m = jnp.max(s, axis=-1, keepdims=True)
    e = jnp.exp(s - m)
    return e * (1.0 / jnp.sum(e, axis=-1, keepdims=True))


_DONE = object()


def _run_staggered(seqs):
    live = []
    pending = list(seqs)
    while live or pending:
        if pending:
            live.append(pending.pop(0))
        live = [s for s in live if next(s, _DONE) is not _DONE]


def _kv_out_copies(kbuf, vbuf, k_hbm, v_hbm, sem, layer, blk, slot):
    _, heads, nbk = kbuf.shape[:3]
    rows = pl.ds(blk * nbk, nbk)
    cps = []
    for hd in range(heads):
        cps.append(pltpu.make_async_copy(kbuf.at[slot, hd], k_hbm.at[layer, rows, :, hd, :], sem.at[slot, 0, hd]))
        cps.append(pltpu.make_async_copy(vbuf.at[slot, hd], v_hbm.at[layer, rows, :, hd, :], sem.at[slot, 1, hd]))
    return cps


def _kv_kernel(m_ref, wk_hbm, wv_hbm, k_hbm, v_hbm, kbuf, vbuf, sem, wk_ref, wv_ref, stage, wsem):
    nj = pl.num_programs(1)
    layer = pl.program_id(0)
    blk = pl.program_id(1)
    step = layer * nj + blk
    last = pl.num_programs(0) * nj - 1
    slot = lax.rem(step, 2)
    _, heads, nbk, mem, dh = kbuf.shape
    copies = functools.partial(_kv_out_copies, kbuf, vbuf, k_hbm, v_hbm, sem)

    @pl.when(blk == 0)
    def _():
        _fetch_bf16((wk_hbm, layer, wk_ref, stage, wsem), (wv_hbm, layer, wv_ref, stage, wsem))

    @pl.when(step >= 2)
    def _():
        prev = step - 2
        for cp in copies(lax.div(prev, nj), lax.rem(prev, nj), slot):
            cp.wait()

    m = m_ref[...].reshape(nbk * mem, m_ref.shape[-1]).astype(BF16)
    k = _dot(m, wk_ref[...])
    v = _dot(m, wv_ref[...])
    for hd in range(heads):
        for b in range(nbk):
            kbuf[slot, hd, b] = k[b * mem:(b + 1) * mem, hd * dh:(hd + 1) * dh]
            vbuf[slot, hd, b] = v[b * mem:(b + 1) * mem, hd * dh:(hd + 1) * dh]
    for cp in copies(layer, blk, slot):
        cp.start()

    @pl.when(step == last)
    def _():
        @pl.when(step >= 1)
        def _():
            prev = step - 1
            for cp in copies(lax.div(prev, nj), lax.rem(prev, nj), 1 - slot):
                cp.wait()
        for cp in copies(layer, blk, slot):
            cp.wait()


def _kv_proj(mem3, wk, wv, heads):
    nb, mem, d = mem3.shape
    depth = wk.shape[0]
    dh = d // heads
    nbk = _KV_BATCHES
    assert nb % nbk == 0 and mem % _SUBLANE == 0
    any_spec = pl.BlockSpec(memory_space=pl.ANY)
    buf = pltpu.VMEM((2, heads, nbk, mem, dh), F32)
    wk_scr, stage, wsem = _weight_scratch(d, d)
    return pl.pallas_call(
        _kv_kernel,
        grid=(depth, nb // nbk),
        in_specs=[pl.BlockSpec((nbk, mem, d), lambda l, j: (j, 0, 0)), any_spec, any_spec],
        out_specs=[any_spec, any_spec],
        out_shape=[jax.ShapeDtypeStruct((depth, nb, mem, heads, dh), F32)] * 2,
        scratch_shapes=[buf, buf, pltpu.SemaphoreType.DMA((2, 2, heads)), wk_scr, wk_scr, stage, wsem],
        compiler_params=_params("arbitrary", "arbitrary"),
        name="kv_proj",
    )(mem3, wk, wv)


def _sattn_copies(k_hbm, v_hbm, cbuf, sem, layer, row0, slot):
    _, _, heads, hb = cbuf.shape[:4]
    rows = pl.ds(row0, hb)
    cps = []
    for j, src in enumerate((k_hbm, v_hbm)):
        for hd in range(heads):
            cps.append(pltpu.make_async_copy(src.at[layer, rows, :, hd, :], cbuf.at[slot, j, hd], sem.at[slot, j, hd]))
    return cps


def _sample_attn_rows(qs_ref, o_scr, cbuf, slot, row0):
    _, _, heads, hb, _, dh = cbuf.shape
    for j in range(hb):
        outs = []
        for hd in range(heads):
            q_row = qs_ref[pl.ds(row0 + j, 1), hd * dh:(hd + 1) * dh]
            s = jnp.sum(cbuf[slot, 0, hd, j] * q_row, axis=1, keepdims=True)
            e = jnp.exp(s - jnp.max(s, axis=0, keepdims=True))
            den = jnp.sum(e, axis=0, keepdims=True)
            num = jnp.sum(e * cbuf[slot, 1, hd, j], axis=0, keepdims=True)
            outs.append(num * (1.0 / den))
            yield
        o_scr[pl.ds(row0 + j, 1), :] = jnp.concatenate(outs, axis=1)


def _ffn_kernel(xp_ref, xs_ref, qs_ref, g_ref, gf_ref, ck_hbm, cv_hbm, wg_hbm, wu_hbm, wd_hbm, wo_hbm,
                op_ref, os_ref, wg_ref, wu_ref, up_stage, up_sem, wd_ref, dn_stage, dn_sem, wo_ref,
                o_scr, cbuf, csem, *, layer, chunks, final_norm):
    step = pl.program_id(0)
    n_tiles = pl.num_programs(0) - 1
    slots, hb = cbuf.shape[0], cbuf.shape[3]
    per_step = slots * hb
    ccopies = functools.partial(_sattn_copies, ck_hbm, cv_hbm, cbuf, csem, layer)

    @pl.when(step == 0)
    def _():
        for sl in range(slots):
            for cp in ccopies(sl * hb, sl):
                cp.start()
        _fetch_bf16((wg_hbm, layer, wg_ref, up_stage, up_sem), (wu_hbm, layer, wu_ref, up_stage, up_sem),
                    (wd_hbm, layer, wd_ref, dn_stage, dn_sem), (wo_hbm, layer, wo_ref, dn_stage, dn_sem))

    def swiglu(x, o_ref):
        h = _rms(x, g_ref[...]).astype(BF16)
        yield
        acc = x
        lo = 0
        for width in chunks:
            gate = _dot(h, wg_ref[:, lo:lo + width])
            up = _dot(h, wu_ref[:, lo:lo + width])
            yield
            act = (_silu(gate) * up).astype(BF16)
            yield
            acc = acc + _dot(act, wd_ref[lo:lo + width, :])
            yield
            lo += width
        if final_norm:
            acc = _rms(acc, gf_ref[...])
        o_ref[...] = acc

    def sample_rows():
        nxt = jnp.minimum(step + 1, n_tiles - 1) * per_step
        for sl in range(slots):
            row0 = step * per_step + sl * hb
            for cp in ccopies(row0, sl):
                cp.wait()
            yield from _sample_attn_rows(qs_ref, o_scr, cbuf, sl, row0)
            for cp in ccopies(nxt + sl * hb, sl):
                cp.start()
            yield

    @pl.when(step < n_tiles)
    def _():
        _run_staggered([swiglu(xp_ref[...], op_ref), sample_rows()])

    @pl.when(step == n_tiles)
    def _():
        for sl in range(slots):
            for cp in ccopies((n_tiles - 1) * per_step + sl * hb, sl):
                cp.wait()
        x = xs_ref[...] + _dot(o_scr[...].astype(BF16), wo_ref[...])
        _run_staggered([swiglu(x, os_ref)])


def _ffn(xp2d, xs2d, qs2d, ck5, cv5, wo, g, wg, wu, wd, gf, layer, tm, final_norm):
    rows, d = xp2d.shape
    b = xs2d.shape[0]
    ff = wg.shape[-1]
    _, _, mem_len, heads, dh = ck5.shape
    n_tiles = rows // tm
    slots = 2
    assert sum(_FF_CHUNKS) == ff and rows % tm == 0 and b % (n_tiles * slots) == 0
    hb = b // (n_tiles * slots)
    kern = functools.partial(_ffn_kernel, layer=layer, chunks=_FF_CHUNKS, final_norm=final_norm)
    row_spec = pl.BlockSpec((tm, d), lambda i: (jnp.minimum(i, n_tiles - 1), 0))
    any_spec = pl.BlockSpec(memory_space=pl.ANY)
    ops, specs = zip(_sel(g, layer), _sel(gf, 0))
    wg_scr, up_stage, up_sem = _weight_scratch(d, ff)
    wd_scr, dn_stage, dn_sem = _weight_scratch(ff, d)
    assert dn_stage.shape[1:] == _weight_scratch(d, d)[1].shape[1:]
    return pl.pallas_call(
        kern,
        grid=(n_tiles + 1,),
        in_specs=[row_spec, _const_spec((b, d)), _const_spec((b, d)), *specs] + [any_spec] * 6,
        out_specs=[row_spec, _whole_spec((b, d))],
        out_shape=[jax.ShapeDtypeStruct((rows, d), F32), jax.ShapeDtypeStruct((b, d), F32)],
        scratch_shapes=[wg_scr, wg_scr, up_stage, up_sem, wd_scr, dn_stage, dn_sem, pltpu.VMEM((d, d), BF16),
                        pltpu.VMEM((b, d), F32), pltpu.VMEM((slots, 2, heads, hb, mem_len, dh), F32),
                        pltpu.SemaphoreType.DMA((slots, 2, heads))],
        compiler_params=_params("arbitrary"),
        name="ffn",
    )(xp2d, xs2d, qs2d, *ops, ck5, cv5, wg, wu, wd, wo)


def _pattn_kv_copies(k_hbm, v_hbm, kvbuf, sem, layer, n, slot):
    heads = kvbuf.shape[2]
    cps = []
    for j, src in enumerate((k_hbm, v_hbm)):
        for hd in range(heads):
            cps.append(pltpu.make_async_copy(src.at[layer, n, :, hd, :], kvbuf.at[slot, j, hd], sem.at[slot, j, hd]))
    return cps


def _attn_kernel(xp_ref, xs_ref, g_ref, pk_hbm, pv_hbm, wq_hbm, wo_hbm, op_ref, qs_ref,
                 kvbuf, kv16, psem, wq_ref, wo_ref, stage, wsem, *, layer, parts):
    n = pl.program_id(0)
    i = pl.program_id(1)
    heads, _, dh = kv16.shape[1:]
    tm = xp_ref.shape[0]
    rows = tm // parts
    pslot = lax.rem(n, 2)
    pcopies = functools.partial(_pattn_kv_copies, pk_hbm, pv_hbm, kvbuf, psem, layer)

    @pl.when(jnp.logical_and(n == 0, i == 0))
    def _():
        for cp in pcopies(0, 0):
            cp.start()
        _fetch_bf16((wq_hbm, layer, wq_ref, stage, wsem), (wo_hbm, layer, wo_ref, stage, wsem))
        h = _rms(xs_ref[...], g_ref[...]).astype(BF16)
        qs_ref[...] = _dot(h, wq_ref[...]) * (dh ** -0.5)

    @pl.when(i == 0)
    def _():
        @pl.when(n + 1 < pl.num_programs(0))
        def _():
            for cp in pcopies(n + 1, 1 - pslot):
                cp.start()

        for cp in pcopies(n, pslot):
            cp.wait()
        kv16[...] = kvbuf[pslot].astype(BF16)

    def attend(r0):
        x = xp_ref[r0:r0 + rows, :]
        h = _rms(x, g_ref[...]).astype(BF16)
        yield
        q = _dot(h, wq_ref[...]).astype(BF16)
        yield
        outs = []
        for hd in range(heads):
            s = lax.dot_general(q[:, hd * dh:(hd + 1) * dh], kv16[0, hd], (((1,), (1,)), ((), ())),
                                preferred_element_type=F32)
            yield
            a = _softmax_rows(s * (dh ** -0.5)).astype(BF16)
            yield
            outs.append(_dot(a, kv16[1, hd]))
            yield
        o = jnp.concatenate(outs, axis=1).astype(BF16)
        op_ref[r0:r0 + rows, :] = x + _dot(o, wo_ref[...])
        yield

    _run_staggered([attend(p * rows) for p in range(parts)])


def _attn(xp, xs, g, wq, pk5, pv5, wo, layer, tm):
    nb, t, d = xp.shape
    b = xs.shape[0]
    _, _, mem_len, heads, dh = pk5.shape
    assert t % tm == 0 and tm % (_ATTN_PARTS * _SUBLANE) == 0
    x_spec = pl.BlockSpec((None, tm, d), lambda n, i: (n, i, 0))
    any_spec = pl.BlockSpec(memory_space=pl.ANY)
    g, g_spec = _sel(g, layer)
    wq_scr, stage, wsem = _weight_scratch(d, d)
    return pl.pallas_call(
        functools.partial(_attn_kernel, layer=layer, parts=_ATTN_PARTS),
        grid=(nb, t // tm),
        in_specs=[x_spec, _const_spec((b, d)), g_spec] + [any_spec] * 4,
        out_specs=[x_spec, _whole_spec((b, d))],
        out_shape=[jax.ShapeDtypeStruct((nb, t, d), F32), jax.ShapeDtypeStruct((b, d), F32)],
        scratch_shapes=[pltpu.VMEM((2, 2, heads, mem_len, dh), F32), pltpu.VMEM((2, heads, mem_len, dh), BF16),
                        pltpu.SemaphoreType.DMA((2, 2, heads)), wq_scr, wq_scr, stage, wsem],
        compiler_params=_params("arbitrary", "arbitrary"),
        name="attn",
    )(xp, xs, g, pk5, pv5, wq, wo)


def _peven_kernel(x_ref, g_ref, cw_ref, cb_ref, lg_ref, lb_ref, ps_ref, win_hbm, wout_hbm, pw_hbm,
                  xo_ref, sa_ref, sb_ref, za_ext, zb_ext, ca_buf,
                  win_ref, win_stage, win_sem, wout_ref, wout_stage, wout_sem, pw_ref, pw_stage, pw_sem,
                  *, idx, tm, d_a, ha, hb):
    t = pl.program_id(1)
    ka = cw_ref.shape[0]
    nca = za_ext.shape[0]
    ncb = zb_ext.shape[0]
    pb = sb_ref.shape[0]

    @pl.when(jnp.logical_and(pl.program_id(0) == 0, t == 0))
    def _():
        _fetch_bf16((win_hbm, idx, win_ref, win_stage, win_sem), (wout_hbm, idx, wout_ref, wout_stage, wout_sem),
                    (pw_hbm, idx, pw_ref, pw_stage, pw_sem))

    @pl.when(t == 0)
    def _():
        za_ext[:, 0:ha, :] = jnp.zeros((nca, ha, _LANE), F32)
        zb_ext[:, 0:hb, :] = jnp.zeros((ncb, hb, _LANE), F32)

    x = x_ref[...]
    h = _rms(x, g_ref[...]).astype(BF16)
    p = _dot(h, win_ref[...])
    za = p[:, :d_a] * jax.nn.sigmoid(p[:, d_a:2 * d_a])
    zb = p[:, 2 * d_a:]
    for c in range(nca):
        za_ext[c, ha:ha + tm, :] = za[:, c * _LANE:(c + 1) * _LANE]
    for c in range(ncb):
        zb_ext[c, hb:hb + tm, :] = zb[:, c * _LANE:(c + 1) * _LANE]

    base = ha - (ka - 1)
    for c in range(nca):
        ls = slice(c * _LANE, (c + 1) * _LANE)
        for r in range(tm // _CONV_ROWS):
            r0 = r * _CONV_ROWS
            acc = jnp.broadcast_to(cb_ref[:, ls], (_CONV_ROWS, _LANE))
            for k in range(ka):
                acc = acc + cw_ref[k:k + 1, ls] * za_ext[c, base + r0 + k:base + r0 + k + _CONV_ROWS, :]
            ca_buf[r0:r0 + _CONV_ROWS, ls] = acc
    ya = _silu(_layernorm(ca_buf[...], lg_ref[...], lb_ref[...]))

    pos = t * tm + lax.broadcasted_iota(jnp.int32, (tm, 1), 0)
    parts = []
    for gi, w in enumerate(_POOL_WINDOWS):
        zg = zb[:, gi * _LANE:(gi + 1) * _LANE]
        s = zg
        for j in range(1, w):
            s = s + zb_ext[gi, hb - j:hb - j + tm, :]
        inv_cnt = 1.0 / jnp.minimum(pos + 1, w).astype(F32)
        pooled = (s * inv_cnt - zg).astype(BF16)
        parts.append(_dot(pooled, pw_ref[gi * _LANE:(gi + 1) * _LANE, :]))
    yb = jnp.concatenate(parts, axis=1) * ps_ref[...]

    y = jnp.concatenate([ya, yb], axis=1).astype(BF16)
    xo_ref[...] = x + _dot(y, wout_ref[...])

    @pl.when(t == pl.num_programs(1) - 1)
    def _():
        for c in range(nca):
            sa_ref[:, c * _LANE:(c + 1) * _LANE] = za_ext[c, ha + tm - (ka - 1):ha + tm, :]
        for c in range(ncb):
            sb_ref[:, c * _LANE:(c + 1) * _LANE] = zb_ext[c, hb + tm - pb:hb + tm, :]

    za_ext[:, 0:ha, :] = za_ext[:, tm:tm + ha, :]
    zb_ext[:, 0:hb, :] = zb_ext[:, tm:tm + hb, :]


def _prompt_even(x, params, big, idx, tm):
    nb, t, d = x.shape
    ops, specs = zip(*params)
    win, wout, pw = big
    ka, d_a = ops[1].shape[-2:]
    d_b = ops[5].shape[-1]
    pb = max(_POOL_WINDOWS) - 1
    ha = _round_up(ka - 1, _SUBLANE)
    hb = _round_up(pb, _SUBLANE)
    assert t % tm == 0 and tm % _CONV_ROWS == 0 and tm >= max(ha, hb)
    assert d_a % _LANE == 0 and d_b == _LANE * len(_POOL_WINDOWS)
    x_spec = pl.BlockSpec((None, tm, d), lambda n, i: (n, i, 0))
    any_spec = pl.BlockSpec(memory_space=pl.ANY)
    kern = functools.partial(_peven_kernel, idx=idx, tm=tm, d_a=d_a, ha=ha, hb=hb)
    return pl.pallas_call(
        kern,
        grid=(nb, t // tm),
        in_specs=[x_spec, *specs, any_spec, any_spec, any_spec],
        out_specs=[x_spec,
                   pl.BlockSpec((None, ka - 1, d_a), lambda n, i: (n, 0, 0)),
                   pl.BlockSpec((None, pb, d_b), lambda n, i: (n, 0, 0))],
        out_shape=[jax.ShapeDtypeStruct((nb, t, d), F32),
                   jax.ShapeDtypeStruct((nb, ka - 1, d_a), F32),
                   jax.ShapeDtypeStruct((nb, pb, d_b), F32)],
        scratch_shapes=[pltpu.VMEM((d_a // _LANE, ha + tm, _LANE), F32),
                        pltpu.VMEM((d_b // _LANE, hb + tm, _LANE), F32),
                        pltpu.VMEM((tm, d_a), F32),
                        *_weight_scratch(*win.shape[1:]), *_weight_scratch(*wout.shape[1:]),
                        *_weight_scratch(*pw.shape[1:])],
        compiler_params=_params("arbitrary", "arbitrary"),
        name="prompt_even",
    )(x, *ops, win, wout, pw)


def _seven_kernel(x_ref, g_ref, cw_ref, cb_ref, lg_ref, lb_ref, ps_ref, ha_ref, hb_ref, win_hbm, wout_hbm, pw_hbm,
                  xo_ref, sa_ref, sb_ref,
                  win_ref, win_stage, win_sem, wout_ref, wout_stage, wout_sem, pw_ref, pw_stage, pw_sem, *, idx):
    _fetch_bf16((win_hbm, idx, win_ref, win_stage, win_sem), (wout_hbm, idx, wout_ref, wout_stage, wout_sem),
                (pw_hbm, idx, pw_ref, pw_stage, pw_sem))
    ka, d_a = cw_ref.shape
    pb = hb_ref.shape[0]
    d_b = hb_ref.shape[2]
    dg = d_b // len(_POOL_WINDOWS)
    x = x_ref[...]
    h = _rms(x, g_ref[...]).astype(BF16)
    p = _dot(h, win_ref[...])
    za = p[:, :d_a] * jax.nn.sigmoid(p[:, d_a:2 * d_a])
    zb = p[:, 2 * d_a:]

    acc = cb_ref[...] + cw_ref[ka - 1:ka, :] * za
    for k in range(ka - 1):
        acc = acc + cw_ref[k:k + 1, :] * ha_ref[k]
    ya = _silu(_layernorm(acc, lg_ref[...], lb_ref[...]))

    parts = []
    for gi, w in enumerate(_POOL_WINDOWS):
        ls = slice(gi * dg, (gi + 1) * dg)
        s = zb[:, ls]
        for j in range(1, w):
            s = s + hb_ref[pb - j, :, ls]
        inv_cnt = 1.0 / float(min(_PAST_LEN + 1, w))
        pooled = (s * inv_cnt - zb[:, ls]).astype(BF16)
        parts.append(_dot(pooled, pw_ref[gi * dg:(gi + 1) * dg, :]))
    yb = jnp.concatenate(parts, axis=1) * ps_ref[...]

    y = jnp.concatenate([ya, yb], axis=1).astype(BF16)
    xo_ref[...] = x + _dot(y, wout_ref[...])

    sa_ref[0:ka - 2] = ha_ref[1:ka - 1]
    sa_ref[ka - 2] = za
    sb_ref[0:pb - 1] = hb_ref[1:pb]
    sb_ref[pb - 1] = zb


def _sample_even(x2d, params, big, idx, hist_a, hist_b):
    b, d = x2d.shape
    ops, specs = zip(*params)
    win, wout, pw = big
    any_spec = pl.BlockSpec(memory_space=pl.ANY)
    return pl.pallas_call(
        functools.partial(_seven_kernel, idx=idx),
        grid=(1,),
        in_specs=[_const_spec((b, d)), *specs, _const_spec(hist_a.shape), _const_spec(hist_b.shape),
                  any_spec, any_spec, any_spec],
        out_specs=[_whole_spec((b, d)), _whole_spec(hist_a.shape), _whole_spec(hist_b.shape)],
        out_shape=[jax.ShapeDtypeStruct((b, d), F32),
                   jax.ShapeDtypeStruct(hist_a.shape, F32),
                   jax.ShapeDtypeStruct(hist_b.shape, F32)],
        scratch_shapes=[*_weight_scratch(*win.shape[1:]), *_weight_scratch(*wout.shape[1:]),
                        *_weight_scratch(*pw.shape[1:])],
        compiler_params=_params("arbitrary"),
        name="sample_even",
    )(x2d, *ops, hist_a, hist_b, win, wout, pw)


def _podd_kernel(x_ref, g_ref, lg_ref, lb_ref, ws_ref, bst_ref, cw_ref, win_hbm, wout_hbm,
                 xo_ref, sv_ref, sd_ref, g_ext, win_ref, win_stage, win_sem, wout_ref, wout_stage, wout_sem,
                 *, idx, tm, d_c, hd_rows):
    t = pl.program_id(1)
    kd, d_d = cw_ref.shape
    heads = ws_ref.shape[0]
    dh = d_c // heads
    ncd = g_ext.shape[0]

    @pl.when(jnp.logical_and(pl.program_id(0) == 0, t == 0))
    def _():
        _fetch_bf16((win_hbm, idx, win_ref, win_stage, win_sem), (wout_hbm, idx, wout_ref, wout_stage, wout_sem))

    @pl.when(t == 0)
    def _():
        g_ext[:, 0:hd_rows, :] = jnp.zeros((ncd, hd_rows, _LANE), F32)

    x = x_ref[...]
    h = _rms(x, g_ref[...]).astype(BF16)
    p = _dot(h, win_ref[...])
    u = p[:, :d_c]
    v = _layernorm(p[:, d_c:2 * d_c], lg_ref[...], lb_ref[...])
    o = 2 * d_c
    gb = p[:, o:o + d_d]
    gated = p[:, o + d_d:o + 2 * d_d] * p[:, o + 2 * d_d:]

    ri = lax.broadcasted_iota(jnp.int32, (_CHUNK, _CHUNK), 0)
    ci = lax.broadcasted_iota(jnp.int32, (_CHUNK, _CHUNK), 1)
    wm = [jnp.where(ci <= ri, ws_ref[hh], 0.0).astype(BF16) for hh in range(heads)]
    vb = v.astype(BF16)
    rows = []
    for c in range(tm // _CHUNK):
        rs = slice(c * _CHUNK, (c + 1) * _CHUNK)
        parts = [_dot(wm[hh], vb[rs, hh * dh:(hh + 1) * dh]) + bst_ref[:, hh:hh + 1] for hh in range(heads)]
        rows.append(jnp.concatenate(parts, axis=1))
    yc = u * jnp.concatenate(rows, axis=0)

    base = hd_rows - (kd - 1)
    cols = []
    for c in range(ncd):
        ls = slice(c * _LANE, (c + 1) * _LANE)
        g_ext[c, hd_rows:hd_rows + tm, :] = gated[:, ls]
        cd = cw_ref[0:1, ls] * g_ext[c, base:base + tm, :]
        for k in range(1, kd):
            cd = cd + cw_ref[k:k + 1, ls] * g_ext[c, base + k:base + k + tm, :]
        cols.append(cd)
    yd = gb * jnp.concatenate(cols, axis=1)

    y = jnp.concatenate([yc, yd], axis=1).astype(BF16)
    xo_ref[...] = x + _dot(y, wout_ref[...])

    @pl.when(t == pl.num_programs(1) - 1)
    def _():
        n_open = sv_ref.shape[0]
        sv_ref[...] = v[tm - n_open:, :]
        for c in range(ncd):
            sd_ref[:, c * _LANE:(c + 1) * _LANE] = g_ext[c, hd_rows + tm - (kd - 1):hd_rows + tm, :]

    g_ext[:, 0:hd_rows, :] = g_ext[:, tm:tm + hd_rows, :]


def _prompt_odd(x, params, big, idx, tm):
    nb, t, d = x.shape
    ops, specs = zip(*params)
    win, wout = big
    d_c = ops[1].shape[-1]
    kd, d_d = ops[5].shape[-2:]
    n_open = t - ((t - 1) // _CHUNK) * _CHUNK
    hd_rows = _round_up(kd - 1, _SUBLANE)
    assert t % tm == 0 and tm % _CHUNK == 0 and n_open % _SUBLANE == 0 and n_open <= tm
    x_spec = pl.BlockSpec((None, tm, d), lambda n, i: (n, i, 0))
    any_spec = pl.BlockSpec(memory_space=pl.ANY)
    kern = functools.partial(_podd_kernel, idx=idx, tm=tm, d_c=d_c, hd_rows=hd_rows)
    return pl.pallas_call(
        kern,
        grid=(nb, t // tm),
        in_specs=[x_spec, *specs, any_spec, any_spec],
        out_specs=[x_spec,
                   pl.BlockSpec((None, n_open, d_c), lambda n, i: (n, 0, 0)),
                   pl.BlockSpec((None, kd - 1, d_d), lambda n, i: (n, 0, 0))],
        out_shape=[jax.ShapeDtypeStruct((nb, t, d), F32),
                   jax.ShapeDtypeStruct((nb, n_open, d_c), F32),
                   jax.ShapeDtypeStruct((nb, kd - 1, d_d), F32)],
        scratch_shapes=[pltpu.VMEM((d_d // _LANE, hd_rows + tm, _LANE), F32),
                        *_weight_scratch(*win.shape[1:]), *_weight_scratch(*wout.shape[1:])],
        compiler_params=_params("arbitrary", "arbitrary"),
        name="prompt_odd",
    )(x, *ops, win, wout)


def _sodd_kernel(x_ref, g_ref, lg_ref, lb_ref, ws0_ref, bs0_ref, cw_ref, hd_ref, win_hbm, wout_hbm,
                 xo_ref, sv_ref, sd_ref, win_ref, win_stage, win_sem, wout_ref, wout_stage, wout_sem, *, idx):
    _fetch_bf16((win_hbm, idx, win_ref, win_stage, win_sem), (wout_hbm, idx, wout_ref, wout_stage, wout_sem))
    kd, d_d = cw_ref.shape
    d_c = lg_ref.shape[1]
    x = x_ref[...]
    h = _rms(x, g_ref[...]).astype(BF16)
    p = _dot(h, win_ref[...])
    u = p[:, :d_c]
    v = _layernorm(p[:, d_c:2 * d_c], lg_ref[...], lb_ref[...])
    o = 2 * d_c
    gb = p[:, o:o + d_d]
    gated = p[:, o + d_d:o + 2 * d_d] * p[:, o + 2 * d_d:]

    yc = u * (ws0_ref[...] * v + bs0_ref[...])

    cd = cw_ref[kd - 1:kd, :] * gated
    for k in range(kd - 1):
        cd = cd + cw_ref[k:k + 1, :] * hd_ref[:, k, :]
    yd = gb * cd

    y = jnp.concatenate([yc, yd], axis=1).astype(BF16)
    xo_ref[...] = x + _dot(y, wout_ref[...])

    sv_ref[...] = v
    for k in range(kd - 2):
        sd_ref[:, k, :] = hd_ref[:, k + 1, :]
    sd_ref[:, kd - 2, :] = gated


def _sample_odd(x2d, params, big, idx, ws0, bs0, hist_d):
    b, d = x2d.shape
    ops, specs = zip(*params)
    win, wout = big
    d_c = ops[1].shape[-1]
    ops = ops[:3] + (ws0, bs0) + ops[5:]
    specs = specs[:3] + (_const_spec(ws0.shape), _const_spec(bs0.shape)) + specs[5:]
    any_spec = pl.BlockSpec(memory_space=pl.ANY)
    return pl.pallas_call(
        functools.partial(_sodd_kernel, idx=idx),
        grid=(1,),
        in_specs=[_const_spec((b, d)), *specs, _const_spec(hist_d.shape), any_spec, any_spec],
        out_specs=[_whole_spec((b, d)), _whole_spec((b, d_c)), _whole_spec(hist_d.shape)],
        out_shape=[jax.ShapeDtypeStruct((b, d), F32),
                   jax.ShapeDtypeStruct((b, d_c), F32),
                   jax.ShapeDtypeStruct(hist_d.shape, F32)],
        scratch_shapes=[*_weight_scratch(*win.shape[1:]), *_weight_scratch(*wout.shape[1:])],
        compiler_params=_params("arbitrary"),
        name="sample_odd",
    )(x2d, *ops, hist_d, win, wout)


def kernel(x_prompt, x_sample, mem_prompt, state_convA, state_poolB, state_convD, cache_mem_k, cache_mem_v, norm_mix, norm_x, norm_ffn, norm_final, w_in_even, conv_a_w, conv_a_b, ln_a_g, ln_a_b, pool_b_w, pool_b_scale, w_out_even, w_in_odd, ln_c_g, ln_c_b, ws_c, bs_c, conv_d_w, w_out_odd, wq_x, wk_x, wv_x, wo_x, w_gate, w_up, w_down):
    nb, seq, d = x_prompt.shape
    db, dec_seq, _ = x_sample.shape
    assert dec_seq == 1
    depth = wq_x.shape[0]
    heads = cache_mem_k.shape[3]

    def unstack(parts):
        return parts[0][None] if len(parts) == 1 else jnp.stack(parts)

    g_fin = norm_final[None]
    pool_w2d = pool_b_w.reshape(pool_b_w.shape[0], -1, pool_b_w.shape[-1])

    p_k, p_v = _kv_proj(mem_prompt, wk_x, wv_x, heads)

    xp = x_prompt
    xs = x_sample.reshape(db, d)
    p_a, p_b, p_c, p_d = [], [], [], []
    s_a, s_b, s_c, s_d = [], [], [], []
    for layer in range(depth):
        i = layer // 2
        if layer % 2 == 0:
            params = [_sel(norm_mix, layer), _sel(conv_a_w, i), _sel(conv_a_b, i), _sel(ln_a_g, i), _sel(ln_a_b, i),
                      _sel(pool_b_scale, i)]
            big = (w_in_even, w_out_even, pool_w2d)
            xp, a, b = _prompt_even(xp, params, big, i, tm=_PROMPT_TILE)
            p_a.append(a)
            p_b.append(b)
            xs, a, b = _sample_even(xs, params, big, i, state_convA[i].transpose(1, 0, 2),
                                    state_poolB[i].transpose(1, 0, 2))
            s_a.append(a.transpose(1, 0, 2))
            s_b.append(b.transpose(1, 0, 2))
        else:
            params = [_sel(norm_mix, layer), _sel(ln_c_g, i), _sel(ln_c_b, i), _sel(ws_c, i),
                      _sel(bs_c.transpose(0, 2, 1), i), _sel(conv_d_w, i)]
            big = (w_in_odd, w_out_odd)
            xp, c, dd = _prompt_odd(xp, params, big, i, tm=_PROMPT_TILE)
            dc = c.shape[-1]
            p_c.append(c.reshape(nb, -1, _C_HEADS, dc // _C_HEADS))
            p_d.append(dd)
            ws0 = jnp.repeat(ws_c[i, :, 0, 0], dc // _C_HEADS)[None]
            bs0 = jnp.repeat(bs_c[i, :, 0], dc // _C_HEADS)[None]
            xs, c, dd = _sample_odd(xs, params, big, i, ws0, bs0, state_convD[i])
            s_c.append(c.reshape(db, 1, _C_HEADS, dc // _C_HEADS))
            s_d.append(dd)

        xp, qs = _attn(xp, xs, norm_x, wq_x, p_k, p_v, wo_x, layer, tm=_ATTN_TILE)

        xp2d, xs = _ffn(xp.reshape(nb * seq, d), xs, qs, cache_mem_k, cache_mem_v, wo_x, norm_ffn, w_gate, w_up,
                        w_down, g_fin, layer, tm=_FFN_TILE, final_norm=layer == depth - 1)
        xp = xp2d.reshape(nb, seq, d)

    return (xp, xs.reshape(db, 1, d), unstack(p_a), unstack(p_b), unstack(p_c), unstack(p_d), p_k, p_v,
            unstack(s_a), unstack(s_b), unstack(s_c), unstack(s_d))
```

```python
import functools

import jax
import jax.numpy as jnp
from jax import lax
from jax.experimental import pallas as pl
from jax.experimental.pallas import tpu as pltpu

F32 = jnp.float32
BF16 = jnp.bfloat16

_EPS = 1e-6
_X_HEADS = 4
_C_HEADS = 4
_CHUNK = 128
_POOL_WINDOWS = (2, 4, 8, 16)
_PAST_LEN = 16384

_SUBLANE = 8
_LANE = 128
_VMEM_LIMIT = 56 * 1024 * 1024
_STAGE_BYTES = 1024 * 1024
_STAGE_SLOTS = 4

_PROMPT_TILE = 1024
_ATTN_TILE = 1024
_ATTN_PARTS = 2
_FFN_TILE = 512
_KV_BATCHES = 2
_FF_CHUNKS = (768, 768, 768, 512)
_CONV_ROWS = 16


def _round_up(n, m):
    return -(-n // m) * m


def _params(*sem):
    return pltpu.CompilerParams(dimension_semantics=sem, vmem_limit_bytes=_VMEM_LIMIT)


def _const_spec(shape):
    nd = len(shape)
    return pl.BlockSpec(shape, lambda *_: (0,) * nd, pipeline_mode=pl.Buffered(1))


def _whole_spec(shape):
    nd = len(shape)
    return pl.BlockSpec(shape, lambda *_: (0,) * nd)


def _sel(stacked, idx):
    if stacked.ndim == 2:
        stacked = stacked[:, None, :]
    tail = stacked.shape[1:]
    spec = pl.BlockSpec((None,) + tail, lambda *_: (idx,) + (0,) * len(tail), pipeline_mode=pl.Buffered(1))
    return stacked, spec


def _chunk_rows(rows, cols):
    cands = [r for r in range(2 * _SUBLANE, rows + 1, 2 * _SUBLANE)
             if rows % r == 0 and r * cols * 4 <= _STAGE_BYTES]
    return max(cands)


def _weight_scratch(rows, cols):
    return [pltpu.VMEM((rows, cols), BF16), pltpu.VMEM((_STAGE_SLOTS, _chunk_rows(rows, cols), cols), F32),
            pltpu.SemaphoreType.DMA((_STAGE_SLOTS,))]


def _fetch_bf16(*jobs):
    rings = {}
    for w_hbm, idx, dst, stage, sem in jobs:
        slots, rc = stage.shape[:2]
        ring = rings.setdefault(id(stage), [])
        for r0 in range(0, dst.shape[0], rc):
            slot = len(ring) % slots
            copy = pltpu.make_async_copy(w_hbm.at[idx, pl.ds(r0, rc), :], stage.at[slot], sem.at[slot])
            ring.append((copy, dst, r0, rc, stage, slot))
    rings = list(rings.values())
    ahead = [ring[0][4].shape[0] - 1 for ring in rings]
    for ring, n in zip(rings, ahead):
        for copy, *_ in ring[:n]:
            copy.start()
    for k in range(max(len(ring) for ring in rings)):
        for ring, n in zip(rings, ahead):
            if k >= len(ring):
                continue
            if k + n < len(ring):
                ring[k + n][0].start()
            copy, dst, r0, rc, stage, slot = ring[k]
            copy.wait()
            dst[r0:r0 + rc, :] = stage[slot].astype(BF16)


def _dot(a, b):
    return jnp.dot(a, b, preferred_element_type=F32)


def _rms(x, g):
    y = x * lax.rsqrt(jnp.mean(x * x, axis=-1, keepdims=True) + _EPS)
    return y * g


def _layernorm(x, g, b):
    mu = jnp.mean(x, axis=-1, keepdims=True)
    d = x - mu
    var = jnp.mean(d * d, axis=-1, keepdims=True)
    return d * lax.rsqrt(var + _EPS) * g + b


def _silu(x):
    return x * jax.nn.sigmoid(x)


def _softmax_rows(s):
    m = jnp.max(s, axis=-1, keepdims=True)
    e = jnp.exp(s - m)
    return e * (1.0 / jnp.sum(e, axis=-1, keepdims=True))


_DONE = object()


def _run_staggered(seqs):
    live = []
    pending = list(seqs)
    while live or pending:
        if pending:
            live.append(pending.pop(0))
        live = [s for s in live if next(s, _DONE) is not _DONE]


def _kv_out_copies(kbuf, vbuf, k_hbm, v_hbm, sem, layer, blk, slot):
    _, heads, nbk = kbuf.shape[:3]
    rows = pl.ds(blk * nbk, nbk)
    cps = []
    for hd in range(heads):
        cps.append(pltpu.make_async_copy(kbuf.at[slot, hd], k_hbm.at[layer, rows, :, hd, :], sem.at[slot, 0, hd]))
        cps.append(pltpu.make_async_copy(vbuf.at[slot, hd], v_hbm.at[layer, rows, :, hd, :], sem.at[slot, 1, hd]))
    return cps


def _kv_kernel(m_ref, wk_hbm, wv_hbm, k_hbm, v_hbm, kbuf, vbuf, sem, wk_ref, wv_ref, stage, wsem):
    nj = pl.num_programs(1)
    layer = pl.program_id(0)
    blk = pl.program_id(1)
    step = layer * nj + blk
    last = pl.num_programs(0) * nj - 1
    slot = lax.rem(step, 2)
    _, heads, nbk, mem, dh = kbuf.shape
    copies = functools.partial(_kv_out_copies, kbuf, vbuf, k_hbm, v_hbm, sem)

    @pl.when(blk == 0)
    def _():
        _fetch_bf16((wk_hbm, layer, wk_ref, stage, wsem), (wv_hbm, layer, wv_ref, stage, wsem))

    @pl.when(step >= 2)
    def _():
        prev = step - 2
        for cp in copies(lax.div(prev, nj), lax.rem(prev, nj), slot):
            cp.wait()

    m = m_ref[...].reshape(nbk * mem, m_ref.shape[-1]).astype(BF16)
    k = _dot(m, wk_ref[...])
    v = _dot(m, wv_ref[...])
    for hd in range(heads):
        for b in range(nbk):
            kbuf[slot, hd, b] = k[b * mem:(b + 1) * mem, hd * dh:(hd + 1) * dh]
            vbuf[slot, hd, b] = v[b * mem:(b + 1) * mem, hd * dh:(hd + 1) * dh]
    for cp in copies(layer, blk, slot):
        cp.start()

    @pl.when(step == last)
    def _():
        @pl.when(step >= 1)
        def _():
            prev = step - 1
            for cp in copies(lax.div(prev, nj), lax.rem(prev, nj), 1 - slot):
                cp.wait()
        for cp in copies(layer, blk, slot):
            cp.wait()


def _kv_proj(mem3, wk, wv, heads):
    nb, mem, d = mem3.shape
    depth = wk.shape[0]
    dh = d // heads
    nbk = _KV_BATCHES
    assert nb % nbk == 0 and mem % _SUBLANE == 0
    any_spec = pl.BlockSpec(memory_space=pl.ANY)
    buf = pltpu.VMEM((2, heads, nbk, mem, dh), F32)
    wk_scr, stage, wsem = _weight_scratch(d, d)
    return pl.pallas_call(
        _kv_kernel,
        grid=(depth, nb // nbk),
        in_specs=[pl.BlockSpec((nbk, mem, d), lambda l, j: (j, 0, 0)), any_spec, any_spec],
        out_specs=[any_spec, any_spec],
        out_shape=[jax.ShapeDtypeStruct((depth, nb, mem, heads, dh), F32)] * 2,
        scratch_shapes=[buf, buf, pltpu.SemaphoreType.DMA((2, 2, heads)), wk_scr, wk_scr, stage, wsem],
        compiler_params=_params("arbitrary", "arbitrary"),
        name="kv_proj",
    )(mem3, wk, wv)


def _sattn_copies(k_hbm, v_hbm, cbuf, sem, layer, row0, slot):
    _, _, heads, hb = cbuf.shape[:4]
    rows = pl.ds(row0, hb)
    cps = []
    for j, src in enumerate((k_hbm, v_hbm)):
        for hd in range(heads):
            cps.append(pltpu.make_async_copy(src.at[layer, rows, :, hd, :], cbuf.at[slot, j, hd], sem.at[slot, j, hd]))
    return cps


def _sample_attn_rows(qs_ref, o_scr, cbuf, slot, row0):
    _, _, heads, hb, _, dh = cbuf.shape
    for j in range(hb):
        outs = []
        for hd in range(heads):
            q_row = qs_ref[pl.ds(row0 + j, 1), hd * dh:(hd + 1) * dh]
            s = jnp.sum(cbuf[slot, 0, hd, j] * q_row, axis=1, keepdims=True)
            e = jnp.exp(s - jnp.max(s, axis=0, keepdims=True))
            den = jnp.sum(e, axis=0, keepdims=True)
            num = jnp.sum(e * cbuf[slot, 1, hd, j], axis=0, keepdims=True)
            outs.append(num * (1.0 / den))
            yield
        o_scr[pl.ds(row0 + j, 1), :] = jnp.concatenate(outs, axis=1)


def _ffn_kernel(xp_ref, xs_ref, qs_ref, g_ref, gf_ref, ck_hbm, cv_hbm, wg_hbm, wu_hbm, wd_hbm, wo_hbm,
                op_ref, os_ref, wg_ref, wu_ref, up_stage, up_sem, wd_ref, dn_stage, dn_sem, wo_ref,
                o_scr, cbuf, csem, *, layer, chunks, final_norm):
    step = pl.program_id(0)
    n_tiles = pl.num_programs(0) - 1
    slots, hb = cbuf.shape[0], cbuf.shape[3]
    per_step = slots * hb
    ccopies = functools.partial(_sattn_copies, ck_hbm, cv_hbm, cbuf, csem, layer)

    @pl.when(step == 0)
    def _():
        for sl in range(slots):
            for cp in ccopies(sl * hb, sl):
                cp.start()
        _fetch_bf16((wg_hbm, layer, wg_ref, up_stage, up_sem), (wu_hbm, layer, wu_ref, up_stage, up_sem),
                    (wd_hbm, layer, wd_ref, dn_stage, dn_sem), (wo_hbm, layer, wo_ref, dn_stage, dn_sem))

    def swiglu(x, o_ref):
        h = _rms(x, g_ref[...]).astype(BF16)
        yield
        acc = x
        lo = 0
        for width in chunks:
            gate = _dot(h, wg_ref[:, lo:lo + width])
            up = _dot(h, wu_ref[:, lo:lo + width])
            yield
            act = (_silu(gate) * up).astype(BF16)
            yield
            acc = acc + _dot(act, wd_ref[lo:lo + width, :])
            yield
            lo += width
        if final_norm:
            acc = _rms(acc, gf_ref[...])
        o_ref[...] = acc

    def sample_rows(sl):
        nxt = jnp.minimum(step + 1, n_tiles - 1) * per_step
        row0 = step * per_step + sl * hb
        yield from _sample_attn_rows(qs_ref, o_scr, cbuf, sl, row0)
        for cp in ccopies(nxt + sl * hb, sl):
            cp.start()
        yield

    @pl.when(step < n_tiles)
    def _():
        for sl in range(slots):
            for cp in ccopies(step * per_step + sl * hb, sl):
                cp.wait()
        _run_staggered([swiglu(xp_ref[...], op_ref)] + [sample_rows(sl) for sl in range(slots)])

    @pl.when(step == n_tiles)
    def _():
        for sl in range(slots):
            for cp in ccopies((n_tiles - 1) * per_step + sl * hb, sl):
                cp.wait()
        x = xs_ref[...] + _dot(o_scr[...].astype(BF16), wo_ref[...])
        _run_staggered([swiglu(x, os_ref)])


def _ffn(xp2d, xs2d, qs2d, ck5, cv5, wo, g, wg, wu, wd, gf, layer, tm, final_norm):
    rows, d = xp2d.shape
    b = xs2d.shape[0]
    ff = wg.shape[-1]
    _, _, mem_len, heads, dh = ck5.shape
    n_tiles = rows // tm
    slots = 2
    assert sum(_FF_CHUNKS) == ff and rows % tm == 0 and b % (n_tiles * slots) == 0
    hb = b // (n_tiles * slots)
    kern = functools.partial(_ffn_kernel, layer=layer, chunks=_FF_CHUNKS, final_norm=final_norm)
    row_spec = pl.BlockSpec((tm, d), lambda i: (jnp.minimum(i, n_tiles - 1), 0))
    any_spec = pl.BlockSpec(memory_space=pl.ANY)
    ops, specs = zip(_sel(g, layer), _sel(gf, 0))
    wg_scr, up_stage, up_sem = _weight_scratch(d, ff)
    wd_scr, dn_stage, dn_sem = _weight_scratch(ff, d)
    assert dn_stage.shape[1:] == _weight_scratch(d, d)[1].shape[1:]
    return pl.pallas_call(
        kern,
        grid=(n_tiles + 1,),
        in_specs=[row_spec, _const_spec((b, d)), _const_spec((b, d)), *specs] + [any_spec] * 6,
        out_specs=[row_spec, _whole_spec((b, d))],
        out_shape=[jax.ShapeDtypeStruct((rows, d), F32), jax.ShapeDtypeStruct((b, d), F32)],
        scratch_shapes=[wg_scr, wg_scr, up_stage, up_sem, wd_scr, dn_stage, dn_sem, pltpu.VMEM((d, d), BF16),
                        pltpu.VMEM((b, d), F32), pltpu.VMEM((slots, 2, heads, hb, mem_len, dh), F32),
                        pltpu.SemaphoreType.DMA((slots, 2, heads))],
        compiler_params=_params("arbitrary"),
        name="ffn",
    )(xp2d, xs2d, qs2d, *ops, ck5, cv5, wg, wu, wd, wo)


def _pattn_kv_copies(k_hbm, v_hbm, kvbuf, sem, layer, n, slot):
    heads = kvbuf.shape[2]
    cps = []
    for j, src in enumerate((k_hbm, v_hbm)):
        for hd in range(heads):
            cps.append(pltpu.make_async_copy(src.at[layer, n, :, hd, :], kvbuf.at[slot, j, hd], sem.at[slot, j, hd]))
    return cps


def _attn_kernel(xp_ref, xs_ref, g_ref, pk_hbm, pv_hbm, wq_hbm, wo_hbm, op_ref, qs_ref,
                 kvbuf, kv16, psem, wq_ref, wo_ref, stage, wsem, *, layer, parts):
    n = pl.program_id(0)
    i = pl.program_id(1)
    heads, _, dh = kv16.shape[1:]
    tm = xp_ref.shape[0]
    rows = tm // parts
    pslot = lax.rem(n, 2)
    pcopies = functools.partial(_pattn_kv_copies, pk_hbm, pv_hbm, kvbuf, psem, layer)

    @pl.when(jnp.logical_and(n == 0, i == 0))
    def _():
        for cp in pcopies(0, 0):
            cp.start()
        _fetch_bf16((wq_hbm, layer, wq_ref, stage, wsem), (wo_hbm, layer, wo_ref, stage, wsem))
        h = _rms(xs_ref[...], g_ref[...]).astype(BF16)
        qs_ref[...] = _dot(h, wq_ref[...]) * (dh ** -0.5)

    @pl.when(i == 0)
    def _():
        @pl.when(n + 1 < pl.num_programs(0))
        def _():
            for cp in pcopies(n + 1, 1 - pslot):
                cp.start()

        for cp in pcopies(n, pslot):
            cp.wait()
        kv16[...] = kvbuf[pslot].astype(BF16)

    def attend(r0):
        x = xp_ref[r0:r0 + rows, :]
        h = _rms(x, g_ref[...]).astype(BF16)
        yield
        q = _dot(h, wq_ref[...]).astype(BF16)
        yield
        outs = []
        for hd in range(heads):
            s = lax.dot_general(q[:, hd * dh:(hd + 1) * dh], kv16[0, hd], (((1,), (1,)), ((), ())),
                                preferred_element_type=F32)
            yield
            a = _softmax_rows(s * (dh ** -0.5)).astype(BF16)
            yield
            outs.append(_dot(a, kv16[1, hd]))
            yield
        o = jnp.concatenate(outs, axis=1).astype(BF16)
        op_ref[r0:r0 + rows, :] = x + _dot(o, wo_ref[...])
        yield

    _run_staggered([attend(p * rows) for p in range(parts)])


def _attn(xp, xs, g, wq, pk5, pv5, wo, layer, tm):
    nb, t, d = xp.shape
    b = xs.shape[0]
    _, _, mem_len, heads, dh = pk5.shape
    assert t % tm == 0 and tm % (_ATTN_PARTS * _SUBLANE) == 0
    x_spec = pl.BlockSpec((None, tm, d), lambda n, i: (n, i, 0))
    any_spec = pl.BlockSpec(memory_space=pl.ANY)
    g, g_spec = _sel(g, layer)
    wq_scr, stage, wsem = _weight_scratch(d, d)
    return pl.pallas_call(
        functools.partial(_attn_kernel, layer=layer, parts=_ATTN_PARTS),
        grid=(nb, t // tm),
        in_specs=[x_spec, _const_spec((b, d)), g_spec] + [any_spec] * 4,
        out_specs=[x_spec, _whole_spec((b, d))],
        out_shape=[jax.ShapeDtypeStruct((nb, t, d), F32), jax.ShapeDtypeStruct((b, d), F32)],
        scratch_shapes=[pltpu.VMEM((2, 2, heads, mem_len, dh), F32), pltpu.VMEM((2, heads, mem_len, dh), BF16),
                        pltpu.SemaphoreType.DMA((2, 2, heads)), wq_scr, wq_scr, stage, wsem],
        compiler_params=_params("arbitrary", "arbitrary"),
        name="attn",
    )(xp, xs, g, pk5, pv5, wq, wo)


def _peven_kernel(x_ref, g_ref, cw_ref, cb_ref, lg_ref, lb_ref, ps_ref, win_hbm, wout_hbm, pw_hbm,
                  xo_ref, sa_ref, sb_ref, za_ext, zb_ext, ca_buf,
                  win_ref, win_stage, win_sem, wout_ref, wout_stage, wout_sem, pw_ref, pw_stage, pw_sem,
                  *, idx, tm, d_a, ha, hb):
    t = pl.program_id(1)
    ka = cw_ref.shape[0]
    nca = za_ext.shape[0]
    ncb = zb_ext.shape[0]
    pb = sb_ref.shape[0]

    @pl.when(jnp.logical_and(pl.program_id(0) == 0, t == 0))
    def _():
        _fetch_bf16((win_hbm, idx, win_ref, win_stage, win_sem), (wout_hbm, idx, wout_ref, wout_stage, wout_sem),
                    (pw_hbm, idx, pw_ref, pw_stage, pw_sem))

    @pl.when(t == 0)
    def _():
        za_ext[:, 0:ha, :] = jnp.zeros((nca, ha, _LANE), F32)
        zb_ext[:, 0:hb, :] = jnp.zeros((ncb, hb, _LANE), F32)

    x = x_ref[...]
    h = _rms(x, g_ref[...]).astype(BF16)
    p = _dot(h, win_ref[...])
    za = p[:, :d_a] * jax.nn.sigmoid(p[:, d_a:2 * d_a])
    zb = p[:, 2 * d_a:]
    for c in range(nca):
        za_ext[c, ha:ha + tm, :] = za[:, c * _LANE:(c + 1) * _LANE]
    for c in range(ncb):
        zb_ext[c, hb:hb + tm, :] = zb[:, c * _LANE:(c + 1) * _LANE]

    base = ha - (ka - 1)
    for c in range(nca):
        ls = slice(c * _LANE, (c + 1) * _LANE)
        for r in range(tm // _CONV_ROWS):
            r0 = r * _CONV_ROWS
            acc = jnp.broadcast_to(cb_ref[:, ls], (_CONV_ROWS, _LANE))
            for k in range(ka):
                acc = acc + cw_ref[k:k + 1, ls] * za_ext[c, base + r0 + k:base + r0 + k + _CONV_ROWS, :]
            ca_buf[r0:r0 + _CONV_ROWS, ls] = acc
    ya = _silu(_layernorm(ca_buf[...], lg_ref[...], lb_ref[...]))

    pos = t * tm + lax.broadcasted_iota(jnp.int32, (tm, 1), 0)
    parts = []
    for gi, w in enumerate(_POOL_WINDOWS):
        zg = zb[:, gi * _LANE:(gi + 1) * _LANE]
        s = zg
        for j in range(1, w):
            s = s + zb_ext[gi, hb - j:hb - j + tm, :]
        inv_cnt = 1.0 / jnp.minimum(pos + 1, w).astype(F32)
        pooled = (s * inv_cnt - zg).astype(BF16)
        parts.append(_dot(pooled, pw_ref[gi * _LANE:(gi + 1) * _LANE, :]))
    yb = jnp.concatenate(parts, axis=1) * ps_ref[...]

    y = jnp.concatenate([ya, yb], axis=1).astype(BF16)
    xo_ref[...] = x + _dot(y, wout_ref[...])

    @pl.when(t == pl.num_programs(1) - 1)
    def _():
        for c in range(nca):
            sa_ref[:, c * _LANE:(c + 1) * _LANE] = za_ext[c, ha + tm - (ka - 1):ha + tm, :]
        for c in range(ncb):
            sb_ref[:, c * _LANE:(c + 1) * _LANE] = zb_ext[c, hb + tm - pb:hb + tm, :]

    za_ext[:, 0:ha, :] = za_ext[:, tm:tm + ha, :]
    zb_ext[:, 0:hb, :] = zb_ext[:, tm:tm + hb, :]


def _prompt_even(x, params, big, idx, tm):
    nb, t, d = x.shape
    ops, specs = zip(*params)
    win, wout, pw = big
    ka, d_a = ops[1].shape[-2:]
    d_b = ops[5].shape[-1]
    pb = max(_POOL_WINDOWS) - 1
    ha = _round_up(ka - 1, _SUBLANE)
    hb = _round_up(pb, _SUBLANE)
    assert t % tm == 0 and tm % _CONV_ROWS == 0 and tm >= max(ha, hb)
    assert d_a % _LANE == 0 and d_b == _LANE * len(_POOL_WINDOWS)
    x_spec = pl.BlockSpec((None, tm, d), lambda n, i: (n, i, 0))
    any_spec = pl.BlockSpec(memory_space=pl.ANY)
    kern = functools.partial(_peven_kernel, idx=idx, tm=tm, d_a=d_a, ha=ha, hb=hb)
    return pl.pallas_call(
        kern,
        grid=(nb, t // tm),
        in_specs=[x_spec, *specs, any_spec, any_spec, any_spec],
        out_specs=[x_spec,
                   pl.BlockSpec((None, ka - 1, d_a), lambda n, i: (n, 0, 0)),
                   pl.BlockSpec((None, pb, d_b), lambda n, i: (n, 0, 0))],
        out_shape=[jax.ShapeDtypeStruct((nb, t, d), F32),
                   jax.ShapeDtypeStruct((nb, ka - 1, d_a), F32),
                   jax.ShapeDtypeStruct((nb, pb, d_b), F32)],
        scratch_shapes=[pltpu.VMEM((d_a // _LANE, ha + tm, _LANE), F32),
                        pltpu.VMEM((d_b // _LANE, hb + tm, _LANE), F32),
                        pltpu.VMEM((tm, d_a), F32),
                        *_weight_scratch(*win.shape[1:]), *_weight_scratch(*wout.shape[1:]),
                        *_weight_scratch(*pw.shape[1:])],
        compiler_params=_params("arbitrary", "arbitrary"),
        name="prompt_even",
    )(x, *ops, win, wout, pw)


def _seven_kernel(x_ref, g_ref, cw_ref, cb_ref, lg_ref, lb_ref, ps_ref, ha_ref, hb_ref, win_hbm, wout_hbm, pw_hbm,
                  xo_ref, sa_ref, sb_ref,
                  win_ref, win_stage, win_sem, wout_ref, wout_stage, wout_sem, pw_ref, pw_stage, pw_sem, *, idx):
    _fetch_bf16((win_hbm, idx, win_ref, win_stage, win_sem), (wout_hbm, idx, wout_ref, wout_stage, wout_sem),
                (pw_hbm, idx, pw_ref, pw_stage, pw_sem))
    ka, d_a = cw_ref.shape
    pb = hb_ref.shape[0]
    d_b = hb_ref.shape[2]
    dg = d_b // len(_POOL_WINDOWS)
    x = x_ref[...]
    h = _rms(x, g_ref[...]).astype(BF16)
    p = _dot(h, win_ref[...])
    za = p[:, :d_a] * jax.nn.sigmoid(p[:, d_a:2 * d_a])
    zb = p[:, 2 * d_a:]

    acc = cb_ref[...] + cw_ref[ka - 1:ka, :] * za
    for k in range(ka - 1):
        acc = acc + cw_ref[k:k + 1, :] * ha_ref[k]
    ya = _silu(_layernorm(acc, lg_ref[...], lb_ref[...]))

    parts = []
    for gi, w in enumerate(_POOL_WINDOWS):
        ls = slice(gi * dg, (gi + 1) * dg)
        s = zb[:, ls]
        for j in range(1, w):
            s = s + hb_ref[pb - j, :, ls]
        inv_cnt = 1.0 / float(min(_PAST_LEN + 1, w))
        pooled = (s * inv_cnt - zb[:, ls]).astype(BF16)
        parts.append(_dot(pooled, pw_ref[gi * dg:(gi + 1) * dg, :]))
    yb = jnp.concatenate(parts, axis=1) * ps_ref[...]

    y = jnp.concatenate([ya, yb], axis=1).astype(BF16)
    xo_ref[...] = x + _dot(y, wout_ref[...])

    sa_ref[0:ka - 2] = ha_ref[1:ka - 1]
    sa_ref[ka - 2] = za
    sb_ref[0:pb - 1] = hb_ref[1:pb]
    sb_ref[pb - 1] = zb


def _sample_even(x2d, params, big, idx, hist_a, hist_b):
    b, d = x2d.shape
    ops, specs = zip(*params)
    win, wout, pw = big
    any_spec = pl.BlockSpec(memory_space=pl.ANY)
    return pl.pallas_call(
        functools.partial(_seven_kernel, idx=idx),
        grid=(1,),
        in_specs=[_const_spec((b, d)), *specs, _const_spec(hist_a.shape), _const_spec(hist_b.shape),
                  any_spec, any_spec, any_spec],
        out_specs=[_whole_spec((b, d)), _whole_spec(hist_a.shape), _whole_spec(hist_b.shape)],
        out_shape=[jax.ShapeDtypeStruct((b, d), F32),
                   jax.ShapeDtypeStruct(hist_a.shape, F32),
                   jax.ShapeDtypeStruct(hist_b.shape, F32)],
        scratch_shapes=[*_weight_scratch(*win.shape[1:]), *_weight_scratch(*wout.shape[1:]),
                        *_weight_scratch(*pw.shape[1:])],
        compiler_params=_params("arbitrary"),
        name="sample_even",
    )(x2d, *ops, hist_a, hist_b, win, wout, pw)


def _podd_kernel(x_ref, g_ref, lg_ref, lb_ref, ws_ref, bst_ref, cw_ref, win_hbm, wout_hbm,
                 xo_ref, sv_ref, sd_ref, g_ext, win_ref, win_stage, win_sem, wout_ref, wout_stage, wout_sem,
                 *, idx, tm, d_c, hd_rows):
    t = pl.program_id(1)
    kd, d_d = cw_ref.shape
    heads = ws_ref.shape[0]
    dh = d_c // heads
    ncd = g_ext.shape[0]

    @pl.when(jnp.logical_and(pl.program_id(0) == 0, t == 0))
    def _():
        _fetch_bf16((win_hbm, idx, win_ref, win_stage, win_sem), (wout_hbm, idx, wout_ref, wout_stage, wout_sem))

    @pl.when(t == 0)
    def _():
        g_ext[:, 0:hd_rows, :] = jnp.zeros((ncd, hd_rows, _LANE), F32)

    x = x_ref[...]
    h = _rms(x, g_ref[...]).astype(BF16)
    p = _dot(h, win_ref[...])
    u = p[:, :d_c]
    v = _layernorm(p[:, d_c:2 * d_c], lg_ref[...], lb_ref[...])
    o = 2 * d_c
    gb = p[:, o:o + d_d]
    gated = p[:, o + d_d:o + 2 * d_d] * p[:, o + 2 * d_d:]

    ri = lax.broadcasted_iota(jnp.int32, (_CHUNK, _CHUNK), 0)
    ci = lax.broadcasted_iota(jnp.int32, (_CHUNK, _CHUNK), 1)
    wm = [jnp.where(ci <= ri, ws_ref[hh], 0.0).astype(BF16) for hh in range(heads)]
    vb = v.astype(BF16)
    rows = []
    for c in range(tm // _CHUNK):
        rs = slice(c * _CHUNK, (c + 1) * _CHUNK)
        parts = [_dot(wm[hh], vb[rs, hh * dh:(hh + 1) * dh]) + bst_ref[:, hh:hh + 1] for hh in range(heads)]
        rows.append(jnp.concatenate(parts, axis=1))
    yc = u * jnp.concatenate(rows, axis=0)

    base = hd_rows - (kd - 1)
    cols = []
    for c in range(ncd):
        ls = slice(c * _LANE, (c + 1) * _LANE)
        g_ext[c, hd_rows:hd_rows + tm, :] = gated[:, ls]
        cd = cw_ref[0:1, ls] * g_ext[c, base:base + tm, :]
        for k in range(1, kd):
            cd = cd + cw_ref[k:k + 1, ls] * g_ext[c, base + k:base + k + tm, :]
        cols.append(cd)
    yd = gb * jnp.concatenate(cols, axis=1)

    y = jnp.concatenate([yc, yd], axis=1).astype(BF16)
    xo_ref[...] = x + _dot(y, wout_ref[...])

    @pl.when(t == pl.num_programs(1) - 1)
    def _():
        n_open = sv_ref.shape[0]
        sv_ref[...] = v[tm - n_open:, :]
        for c in range(ncd):
            sd_ref[:, c * _LANE:(c + 1) * _LANE] = g_ext[c, hd_rows + tm - (kd - 1):hd_rows + tm, :]

    g_ext[:, 0:hd_rows, :] = g_ext[:, tm:tm + hd_rows, :]


def _prompt_odd(x, params, big, idx, tm):
    nb, t, d = x.shape
    ops, specs = zip(*params)
    win, wout = big
    d_c = ops[1].shape[-1]
    kd, d_d = ops[5].shape[-2:]
    n_open = t - ((t - 1) // _CHUNK) * _CHUNK
    hd_rows = _round_up(kd - 1, _SUBLANE)
    assert t % tm == 0 and tm % _CHUNK == 0 and n_open % _SUBLANE == 0 and n_open <= tm
    x_spec = pl.BlockSpec((None, tm, d), lambda n, i: (n, i, 0))
    any_spec = pl.BlockSpec(memory_space=pl.ANY)
    kern = functools.partial(_podd_kernel, idx=idx, tm=tm, d_c=d_c, hd_rows=hd_rows)
    return pl.pallas_call(
        kern,
        grid=(nb, t // tm),
        in_specs=[x_spec, *specs, any_spec, any_spec],
        out_specs=[x_spec,
                   pl.BlockSpec((None, n_open, d_c), lambda n, i: (n, 0, 0)),
                   pl.BlockSpec((None, kd - 1, d_d), lambda n, i: (n, 0, 0))],
        out_shape=[jax.ShapeDtypeStruct((nb, t, d), F32),
                   jax.ShapeDtypeStruct((nb, n_open, d_c), F32),
                   jax.ShapeDtypeStruct((nb, kd - 1, d_d), F32)],
        scratch_shapes=[pltpu.VMEM((d_d // _LANE, hd_rows + tm, _LANE), F32),
                        *_weight_scratch(*win.shape[1:]), *_weight_scratch(*wout.shape[1:])],
        compiler_params=_params("arbitrary", "arbitrary"),
        name="prompt_odd",
    )(x, *ops, win, wout)


def _sodd_kernel(x_ref, g_ref, lg_ref, lb_ref, ws0_ref, bs0_ref, cw_ref, hd_ref, win_hbm, wout_hbm,
                 xo_ref, sv_ref, sd_ref, win_ref, win_stage, win_sem, wout_ref, wout_stage, wout_sem, *, idx):
    _fetch_bf16((win_hbm, idx, win_ref, win_stage, win_sem), (wout_hbm, idx, wout_ref, wout_stage, wout_sem))
    kd, d_d = cw_ref.shape
    d_c = lg_ref.shape[1]
    x = x_ref[...]
    h = _rms(x, g_ref[...]).astype(BF16)
    p = _dot(h, win_ref[...])
    u = p[:, :d_c]
    v = _layernorm(p[:, d_c:2 * d_c], lg_ref[...], lb_ref[...])
    o = 2 * d_c
    gb = p[:, o:o + d_d]
    gated = p[:, o + d_d:o + 2 * d_d] * p[:, o + 2 * d_d:]

    yc = u * (ws0_ref[...] * v + bs0_ref[...])

    cd = cw_ref[kd - 1:kd, :] * gated
    for k in range(kd - 1):
        cd = cd + cw_ref[k:k + 1, :] * hd_ref[:, k, :]
    yd = gb * cd

    y = jnp.concatenate([yc, yd], axis=1).astype(BF16)
    xo_ref[...] = x + _dot(y, wout_ref[...])

    sv_ref[...] = v
    for k in range(kd - 2):
        sd_ref[:, k, :] = hd_ref[:, k + 1, :]
    sd_ref[:, kd - 2, :] = gated


def _sample_odd(x2d, params, big, idx, ws0, bs0, hist_d):
    b, d = x2d.shape
    ops, specs = zip(*params)
    win, wout = big
    d_c = ops[1].shape[-1]
    ops = ops[:3] + (ws0, bs0) + ops[5:]
    specs = specs[:3] + (_const_spec(ws0.shape), _const_spec(bs0.shape)) + specs[5:]
    any_spec = pl.BlockSpec(memory_space=pl.ANY)
    return pl.pallas_call(
        functools.partial(_sodd_kernel, idx=idx),
        grid=(1,),
        in_specs=[_const_spec((b, d)), *specs, _const_spec(hist_d.shape), any_spec, any_spec],
        out_specs=[_whole_spec((b, d)), _whole_spec((b, d_c)), _whole_spec(hist_d.shape)],
        out_shape=[jax.ShapeDtypeStruct((b, d), F32),
                   jax.ShapeDtypeStruct((b, d_c), F32),
                   jax.ShapeDtypeStruct(hist_d.shape, F32)],
        scratch_shapes=[*_weight_scratch(*win.shape[1:]), *_weight_scratch(*wout.shape[1:])],
        compiler_params=_params("arbitrary"),
        name="sample_odd",
    )(x2d, *ops, hist_d, win, wout)


def kernel(x_prompt, x_sample, mem_prompt, state_convA, state_poolB, state_convD, cache_mem_k, cache_mem_v, norm_mix, norm_x, norm_ffn, norm_final, w_in_even, conv_a_w, conv_a_b, ln_a_g, ln_a_b, pool_b_w, pool_b_scale, w_out_even, w_in_odd, ln_c_g, ln_c_b, ws_c, bs_c, conv_d_w, w_out_odd, wq_x, wk_x, wv_x, wo_x, w_gate, w_up, w_down):
    nb, seq, d = x_prompt.shape
    db, dec_seq, _ = x_sample.shape
    assert dec_seq == 1
    depth = wq_x.shape[0]
    heads = cache_mem_k.shape[3]

    def unstack(parts):
        return parts[0][None] if len(parts) == 1 else jnp.stack(parts)

    g_fin = norm_final[None]
    pool_w2d = pool_b_w.reshape(pool_b_w.shape[0], -1, pool_b_w.shape[-1])

    p_k, p_v = _kv_proj(mem_prompt, wk_x, wv_x, heads)

    xp = x_prompt
    xs = x_sample.reshape(db, d)
    p_a, p_b, p_c, p_d = [], [], [], []
    s_a, s_b, s_c, s_d = [], [], [], []
    for layer in range(depth):
        i = layer // 2
        if layer % 2 == 0:
            params = [_sel(norm_mix, layer), _sel(conv_a_w, i), _sel(conv_a_b, i), _sel(ln_a_g, i), _sel(ln_a_b, i),
                      _sel(pool_b_scale, i)]
            big = (w_in_even, w_out_even, pool_w2d)
            xp, a, b = _prompt_even(xp, params, big, i, tm=_PROMPT_TILE)
            p_a.append(a)
            p_b.append(b)
            xs, a, b = _sample_even(xs, params, big, i, state_convA[i].transpose(1, 0, 2),
                                    state_poolB[i].transpose(1, 0, 2))
            s_a.append(a.transpose(1, 0, 2))
            s_b.append(b.transpose(1, 0, 2))
        else:
            params = [_sel(norm_mix, layer), _sel(ln_c_g, i), _sel(ln_c_b, i), _sel(ws_c, i),
                      _sel(bs_c.transpose(0, 2, 1), i), _sel(conv_d_w, i)]
            big = (w_in_odd, w_out_odd)
            xp, c, dd = _prompt_odd(xp, params, big, i, tm=_PROMPT_TILE)
            dc = c.shape[-1]
            p_c.append(c.reshape(nb, -1, _C_HEADS, dc // _C_HEADS))
            p_d.append(dd)
            ws0 = jnp.repeat(ws_c[i, :, 0, 0], dc // _C_HEADS)[None]
            bs0 = jnp.repeat(bs_c[i, :, 0], dc // _C_HEADS)[None]
            xs, c, dd = _sample_odd(xs, params, big, i, ws0, bs0, state_convD[i])
            s_c.append(c.reshape(db, 1, _C_HEADS, dc // _C_HEADS))
            s_d.append(dd)

        xp, qs = _attn(xp, xs, norm_x, wq_x, p_k, p_v, wo_x, layer, tm=_ATTN_TILE)

        xp2d, xs = _ffn(xp.reshape(nb * seq, d), xs, qs, cache_mem_k, cache_mem_v, wo_x, norm_ffn, w_gate, w_up,
                        w_down, g_fin, layer, tm=_FFN_TILE, final_norm=layer == depth - 1)
        xp = xp2d.reshape(nb, seq, d)

    return (xp, xs.reshape(db, 1, d), unstack(p_a), unstack(p_b), unstack(p_c), unstack(p_d), p_k, p_v,
            unstack(s_a), unstack(s_b), unstack(s_c), unstack(s_d))
```

```python
import functools

import jax
import jax.numpy as jnp
from jax import lax
from jax.experimental import pallas as pl
from jax.experimental.pallas import tpu as pltpu

F32 = jnp.float32
BF16 = jnp.bfloat16

_EPS = 1e-6
_X_HEADS = 4
_C_HEADS = 4
_CHUNK = 128
_POOL_WINDOWS = (2, 4, 8, 16)
_PAST_LEN = 16384

_SUBLANE = 8
_LANE = 128
_VMEM_LIMIT = 56 * 1024 * 1024
_STAGE_BYTES = 1024 * 1024
_STAGE_SLOTS = 4

_PROMPT_TILE = 1024
_ATTN_TILE = 1024
_ATTN_PARTS = 2
_FFN_TILE = 512
_KV_BATCHES = 2
_FF_CHUNKS = (768, 768, 768, 512)
_CONV_ROWS = 16


def _round_up(n, m):
    return -(-n // m) * m


def _params(*sem):
    return pltpu.CompilerParams(dimension_semantics=sem, vmem_limit_bytes=_VMEM_LIMIT)


def _const_spec(shape):
    nd = len(shape)
    return pl.BlockSpec(shape, lambda *_: (0,) * nd, pipeline_mode=pl.Buffered(1))


def _whole_spec(shape):
    nd = len(shape)
    return pl.BlockSpec(shape, lambda *_: (0,) * nd)


def _sel(stacked, idx):
    if stacked.ndim == 2:
        stacked = stacked[:, None, :]
    tail = stacked.shape[1:]
    spec = pl.BlockSpec((None,) + tail, lambda *_: (idx,) + (0,) * len(tail), pipeline_mode=pl.Buffered(1))
    return stacked, spec


def _chunk_rows(rows, cols):
    cands = [r for r in range(2 * _SUBLANE, rows + 1, 2 * _SUBLANE)
             if rows % r == 0 and r * cols * 4 <= _STAGE_BYTES]
    return max(cands)


def _weight_scratch(rows, cols):
    return [pltpu.VMEM((rows, cols), BF16), pltpu.VMEM((_STAGE_SLOTS, _chunk_rows(rows, cols), cols), F32),
            pltpu.SemaphoreType.DMA((_STAGE_SLOTS,))]


def _fetch_bf16(*jobs):
    rings = {}
    for w_hbm, idx, dst, stage, sem in jobs:
        slots, rc = stage.shape[:2]
        ring = rings.setdefault(id(stage), [])
        for r0 in range(0, dst.shape[0], rc):
            slot = len(ring) % slots
            copy = pltpu.make_async_copy(w_hbm.at[idx, pl.ds(r0, rc), :], stage.at[slot], sem.at[slot])
            ring.append((copy, dst, r0, rc, stage, slot))
    rings = list(rings.values())
    ahead = [ring[0][4].shape[0] - 1 for ring in rings]
    for ring, n in zip(rings, ahead):
        for copy, *_ in ring[:n]:
            copy.start()
    for k in range(max(len(ring) for ring in rings)):
        for ring, n in zip(rings, ahead):
            if k >= len(ring):
                continue
            if k + n < len(ring):
                ring[k + n][0].start()
            copy, dst, r0, rc, stage, slot = ring[k]
            copy.wait()
            dst[r0:r0 + rc, :] = stage[slot].astype(BF16)


def _dot(a, b):
    return jnp.dot(a, b, preferred_element_type=F32)


def _rms(x, g):
    y = x * lax.rsqrt(jnp.mean(x * x, axis=-1, keepdims=True) + _EPS)
    return y * g


def _layernorm(x, g, b):
    mu = jnp.mean(x, axis=-1, keepdims=True)
    d = x - mu
    var = jnp.mean(d * d, axis=-1, keepdims=True)
    return d * lax.rsqrt(var + _EPS) * g + b


def _silu(x):
    return x * jax.nn.sigmoid(x)


def _softmax_rows(s):
    m = jnp.max(s, axis=-1, keepdims=True)
    e = jnp.exp(s - m)
    return e * (1.0 / jnp.sum(e, axis=-1, keepdims=True))


_DONE = object()


def _run_staggered(seqs):
    live = []
    pending = list(seqs)
    while live or pending:
        if pending:
            live.append(pending.pop(0))
        live = [s for s in live if next(s, _DONE) is not _DONE]


def _kv_out_copies(kbuf, vbuf, k_hbm, v_hbm, sem, layer, blk, slot):
    _, heads, nbk = kbuf.shape[:3]
    rows = pl.ds(blk * nbk, nbk)
    cps = []
    for hd in range(heads):
        cps.append(pltpu.make_async_copy(kbuf.at[slot, hd], k_hbm.at[layer, rows, :, hd, :], sem.at[slot, 0, hd]))
        cps.append(pltpu.make_async_copy(vbuf.at[slot, hd], v_hbm.at[layer, rows, :, hd, :], sem.at[slot, 1, hd]))
    return cps


def _kv_kernel(m_ref, wk_hbm, wv_hbm, k_hbm, v_hbm, kbuf, vbuf, sem, wk_ref, wv_ref, stage, wsem):
    nj = pl.num_programs(1)
    layer = pl.program_id(0)
    blk = pl.program_id(1)
    step = layer * nj + blk
    last = pl.num_programs(0) * nj - 1
    slot = lax.rem(step, 2)
    _, heads, nbk, mem, dh = kbuf.shape
    copies = functools.partial(_kv_out_copies, kbuf, vbuf, k_hbm, v_hbm, sem)

    @pl.when(blk == 0)
    def _():
        _fetch_bf16((wk_hbm, layer, wk_ref, stage, wsem), (wv_hbm, layer, wv_ref, stage, wsem))

    @pl.when(step >= 2)
    def _():
        prev = step - 2
        for cp in copies(lax.div(prev, nj), lax.rem(prev, nj), slot):
            cp.wait()

    m = m_ref[...].reshape(nbk * mem, m_ref.shape[-1]).astype(BF16)
    k = _dot(m, wk_ref[...])
    v = _dot(m, wv_ref[...])
    for hd in range(heads):
        for b in range(nbk):
            kbuf[slot, hd, b] = k[b * mem:(b + 1) * mem, hd * dh:(hd + 1) * dh]
            vbuf[slot, hd, b] = v[b * mem:(b + 1) * mem, hd * dh:(hd + 1) * dh]
    for cp in copies(layer, blk, slot):
        cp.start()

    @pl.when(step == last)
    def _():
        @pl.when(step >= 1)
        def _():
            prev = step - 1
            for cp in copies(lax.div(prev, nj), lax.rem(prev, nj), 1 - slot):
                cp.wait()
        for cp in copies(layer, blk, slot):
            cp.wait()


def _kv_proj(mem3, wk, wv, heads):
    nb, mem, d = mem3.shape
    depth = wk.shape[0]
    dh = d // heads
    nbk = _KV_BATCHES
    assert nb % nbk == 0 and mem % _SUBLANE == 0
    any_spec = pl.BlockSpec(memory_space=pl.ANY)
    buf = pltpu.VMEM((2, heads, nbk, mem, dh), F32)
    wk_scr, stage, wsem = _weight_scratch(d, d)
    return pl.pallas_call(
        _kv_kernel,
        grid=(depth, nb // nbk),
        in_specs=[pl.BlockSpec((nbk, mem, d), lambda l, j: (j, 0, 0)), any_spec, any_spec],
        out_specs=[any_spec, any_spec],
        out_shape=[jax.ShapeDtypeStruct((depth, nb, mem, heads, dh), F32)] * 2,
        scratch_shapes=[buf, buf, pltpu.SemaphoreType.DMA((2, 2, heads)), wk_scr, wk_scr, stage, wsem],
        compiler_params=_params("arbitrary", "arbitrary"),
        name="kv_proj",
    )(mem3, wk, wv)


def _sattn_copies(k_hbm, v_hbm, cbuf, sem, layer, row0, slot):
    _, _, heads, hb = cbuf.shape[:4]
    rows = pl.ds(row0, hb)
    cps = []
    for j, src in enumerate((k_hbm, v_hbm)):
        for hd in range(heads):
            cps.append(pltpu.make_async_copy(src.at[layer, rows, :, hd, :], cbuf.at[slot, j, hd], sem.at[slot, j, hd]))
    return cps


def _sample_attn_rows(qs_ref, o_scr, cbuf, slot, row0):
    _, _, heads, hb, _, dh = cbuf.shape
    for j in range(hb):
        outs = []
        for hd in range(heads):
            q_row = qs_ref[pl.ds(row0 + j, 1), hd * dh:(hd + 1) * dh]
            s = jnp.sum(cbuf[slot, 0, hd, j] * q_row, axis=1, keepdims=True)
            e = jnp.exp(s - jnp.max(s, axis=0, keepdims=True))
            den = jnp.sum(e, axis=0, keepdims=True)
            num = jnp.sum(e * cbuf[slot, 1, hd, j], axis=0, keepdims=True)
            outs.append(num * (1.0 / den))
            yield
        o_scr[pl.ds(row0 + j, 1), :] = jnp.concatenate(outs, axis=1)


def _ffn_kernel(xp_ref, xs_ref, qs_ref, g_ref, gf_ref, ck_hbm, cv_hbm, wg_hbm, wu_hbm, wd_hbm, wo_hbm,
                op_ref, os_ref, wg_ref, wu_ref, up_stage, up_sem, wd_ref, dn_stage, dn_sem, wo_ref,
                o_scr, cbuf, csem, *, layer, chunks, final_norm):
    step = pl.program_id(0)
    n_tiles = pl.num_programs(0) - 1
    slots, hb = cbuf.shape[0], cbuf.shape[3]
    per_step = slots * hb
    ccopies = functools.partial(_sattn_copies, ck_hbm, cv_hbm, cbuf, csem, layer)

    @pl.when(step == 0)
    def _():
        for sl in range(slots):
            for cp in ccopies(sl * hb, sl):
                cp.start()
        _fetch_bf16((wg_hbm, layer, wg_ref, up_stage, up_sem), (wu_hbm, layer, wu_ref, up_stage, up_sem),
                    (wd_hbm, layer, wd_ref, dn_stage, dn_sem), (wo_hbm, layer, wo_ref, dn_stage, dn_sem))

    def swiglu(x, o_ref):
        h = _rms(x, g_ref[...]).astype(BF16)
        yield
        acc = x
        lo = 0
        for width in chunks:
            gate = _dot(h, wg_ref[:, lo:lo + width])
            up = _dot(h, wu_ref[:, lo:lo + width])
            yield
            act = (_silu(gate) * up).astype(BF16)
            yield
            acc = acc + _dot(act, wd_ref[lo:lo + width, :])
            yield
            lo += width
        if final_norm:
            acc = _rms(acc, gf_ref[...])
        o_ref[...] = acc

    def sample_rows():
        nxt = jnp.minimum(step + 1, n_tiles - 1) * per_step
        for sl in range(slots):
            row0 = step * per_step + sl * hb
            for cp in ccopies(row0, sl):
                cp.wait()
            yield from _sample_attn_rows(qs_ref, o_scr, cbuf, sl, row0)
            for cp in ccopies(nxt + sl * hb, sl):
                cp.start()
            yield

    @pl.when(step < n_tiles)
    def _():
        _run_staggered([sample_rows()])
        _run_staggered([swiglu(xp_ref[...], op_ref)])

    @pl.when(step == n_tiles)
    def _():
        for sl in range(slots):
            for cp in ccopies((n_tiles - 1) * per_step + sl * hb, sl):
                cp.wait()
        x = xs_ref[...] + _dot(o_scr[...].astype(BF16), wo_ref[...])
        _run_staggered([swiglu(x, os_ref)])


def _ffn(xp2d, xs2d, qs2d, ck5, cv5, wo, g, wg, wu, wd, gf, layer, tm, final_norm):
    rows, d = xp2d.shape
    b = xs2d.shape[0]
    ff = wg.shape[-1]
    _, _, mem_len, heads, dh = ck5.shape
    n_tiles = rows // tm
    slots = 2
    assert sum(_FF_CHUNKS) == ff and rows % tm == 0 and b % (n_tiles * slots) == 0
    hb = b // (n_tiles * slots)
    kern = functools.partial(_ffn_kernel, layer=layer, chunks=_FF_CHUNKS, final_norm=final_norm)
    row_spec = pl.BlockSpec((tm, d), lambda i: (jnp.minimum(i, n_tiles - 1), 0))
    any_spec = pl.BlockSpec(memory_space=pl.ANY)
    ops, specs = zip(_sel(g, layer), _sel(gf, 0))
    wg_scr, up_stage, up_sem = _weight_scratch(d, ff)
    wd_scr, dn_stage, dn_sem = _weight_scratch(ff, d)
    assert dn_stage.shape[1:] == _weight_scratch(d, d)[1].shape[1:]
    return pl.pallas_call(
        kern,
        grid=(n_tiles + 1,),
        in_specs=[row_spec, _const_spec((b, d)), _const_spec((b, d)), *specs] + [any_spec] * 6,
        out_specs=[row_spec, _whole_spec((b, d))],
        out_shape=[jax.ShapeDtypeStruct((rows, d), F32), jax.ShapeDtypeStruct((b, d), F32)],
        scratch_shapes=[wg_scr, wg_scr, up_stage, up_sem, wd_scr, dn_stage, dn_sem, pltpu.VMEM((d, d), BF16),
                        pltpu.VMEM((b, d), F32), pltpu.VMEM((slots, 2, heads, hb, mem_len, dh), F32),
                        pltpu.SemaphoreType.DMA((slots, 2, heads))],
        compiler_params=_params("arbitrary"),
        name="ffn",
    )(xp2d, xs2d, qs2d, *ops, ck5, cv5, wg, wu, wd, wo)


def _pattn_kv_copies(k_hbm, v_hbm, kvbuf, sem, layer, n, slot):
    heads = kvbuf.shape[2]
    cps = []
    for j, src in enumerate((k_hbm, v_hbm)):
        for hd in range(heads):
            cps.append(pltpu.make_async_copy(src.at[layer, n, :, hd, :], kvbuf.at[slot, j, hd], sem.at[slot, j, hd]))
    return cps


def _attn_kernel(xp_ref, xs_ref, g_ref, pk_hbm, pv_hbm, wq_hbm, wo_hbm, op_ref, qs_ref,
                 kvbuf, kv16, psem, wq_ref, wo_ref, stage, wsem, *, layer, parts):
    n = pl.program_id(0)
    i = pl.program_id(1)
    heads, _, dh = kv16.shape[1:]
    tm = xp_ref.shape[0]
    rows = tm // parts
    pslot = lax.rem(n, 2)
    pcopies = functools.partial(_pattn_kv_copies, pk_hbm, pv_hbm, kvbuf, psem, layer)

    @pl.when(jnp.logical_and(n == 0, i == 0))
    def _():
        for cp in pcopies(0, 0):
            cp.start()
        _fetch_bf16((wq_hbm, layer, wq_ref, stage, wsem), (wo_hbm, layer, wo_ref, stage, wsem))
        h = _rms(xs_ref[...], g_ref[...]).astype(BF16)
        qs_ref[...] = _dot(h, wq_ref[...]) * (dh ** -0.5)

    @pl.when(i == 0)
    def _():
        @pl.when(n + 1 < pl.num_programs(0))
        def _():
            for cp in pcopies(n + 1, 1 - pslot):
                cp.start()

        for cp in pcopies(n, pslot):
            cp.wait()
        kv16[...] = kvbuf[pslot].astype(BF16)

    def attend(r0):
        x = xp_ref[r0:r0 + rows, :]
        h = _rms(x, g_ref[...]).astype(BF16)
        yield
        q = _dot(h, wq_ref[...]).astype(BF16)
        yield
        outs = []
        for hd in range(heads):
            s = lax.dot_general(q[:, hd * dh:(hd + 1) * dh], kv16[0, hd], (((1,), (1,)), ((), ())),
                                preferred_element_type=F32)
            yield
            a = _softmax_rows(s * (dh ** -0.5)).astype(BF16)
            yield
            outs.append(_dot(a, kv16[1, hd]))
            yield
        o = jnp.concatenate(outs, axis=1).astype(BF16)
        op_ref[r0:r0 + rows, :] = x + _dot(o, wo_ref[...])
        yield

    _run_staggered([attend(p * rows) for p in range(parts)])


def _attn(xp, xs, g, wq, pk5, pv5, wo, layer, tm):
    nb, t, d = xp.shape
    b = xs.shape[0]
    _, _, mem_len, heads, dh = pk5.shape
    assert t % tm == 0 and tm % (_ATTN_PARTS * _SUBLANE) == 0
    x_spec = pl.BlockSpec((None, tm, d), lambda n, i: (n, i, 0))
    any_spec = pl.BlockSpec(memory_space=pl.ANY)
    g, g_spec = _sel(g, layer)
    wq_scr, stage, wsem = _weight_scratch(d, d)
    return pl.pallas_call(
        functools.partial(_attn_kernel, layer=layer, parts=_ATTN_PARTS),
        grid=(nb, t // tm),
        in_specs=[x_spec, _const_spec((b, d)), g_spec] + [any_spec] * 4,
        out_specs=[x_spec, _whole_spec((b, d))],
        out_shape=[jax.ShapeDtypeStruct((nb, t, d), F32), jax.ShapeDtypeStruct((b, d), F32)],
        scratch_shapes=[pltpu.VMEM((2, 2, heads, mem_len, dh), F32), pltpu.VMEM((2, heads, mem_len, dh), BF16),
                        pltpu.SemaphoreType.DMA((2, 2, heads)), wq_scr, wq_scr, stage, wsem],
        compiler_params=_params("arbitrary", "arbitrary"),
        name="attn",
    )(xp, xs, g, pk5, pv5, wq, wo)


def _peven_kernel(x_ref, g_ref, cw_ref, cb_ref, lg_ref, lb_ref, ps_ref, win_hbm, wout_hbm, pw_hbm,
                  xo_ref, sa_ref, sb_ref, za_ext, zb_ext, ca_buf,
                  win_ref, win_stage, win_sem, wout_ref, wout_stage, wout_sem, pw_ref, pw_stage, pw_sem,
                  *, idx, tm, d_a, ha, hb):
    t = pl.program_id(1)
    ka = cw_ref.shape[0]
    nca = za_ext.shape[0]
    ncb = zb_ext.shape[0]
    pb = sb_ref.shape[0]

    @pl.when(jnp.logical_and(pl.program_id(0) == 0, t == 0))
    def _():
        _fetch_bf16((win_hbm, idx, win_ref, win_stage, win_sem), (wout_hbm, idx, wout_ref, wout_stage, wout_sem),
                    (pw_hbm, idx, pw_ref, pw_stage, pw_sem))

    @pl.when(t == 0)
    def _():
        za_ext[:, 0:ha, :] = jnp.zeros((nca, ha, _LANE), F32)
        zb_ext[:, 0:hb, :] = jnp.zeros((ncb, hb, _LANE), F32)

    x = x_ref[...]
    h = _rms(x, g_ref[...]).astype(BF16)
    p = _dot(h, win_ref[...])
    za = p[:, :d_a] * jax.nn.sigmoid(p[:, d_a:2 * d_a])
    zb = p[:, 2 * d_a:]
    for c in range(nca):
        za_ext[c, ha:ha + tm, :] = za[:, c * _LANE:(c + 1) * _LANE]
    for c in range(ncb):
        zb_ext[c, hb:hb + tm, :] = zb[:, c * _LANE:(c + 1) * _LANE]

    base = ha - (ka - 1)
    for c in range(nca):
        ls = slice(c * _LANE, (c + 1) * _LANE)
        for r in range(tm // _CONV_ROWS):
            r0 = r * _CONV_ROWS
            acc = jnp.broadcast_to(cb_ref[:, ls], (_CONV_ROWS, _LANE))
            for k in range(ka):
                acc = acc + cw_ref[k:k + 1, ls] * za_ext[c, base + r0 + k:base + r0 + k + _CONV_ROWS, :]
            ca_buf[r0:r0 + _CONV_ROWS, ls] = acc
    ya = _silu(_layernorm(ca_buf[...], lg_ref[...], lb_ref[...]))

    pos = t * tm + lax.broadcasted_iota(jnp.int32, (tm, 1), 0)
    parts = []
    for gi, w in enumerate(_POOL_WINDOWS):
        zg = zb[:, gi * _LANE:(gi + 1) * _LANE]
        s = zg
        for j in range(1, w):
            s = s + zb_ext[gi, hb - j:hb - j + tm, :]
        inv_cnt = 1.0 / jnp.minimum(pos + 1, w).astype(F32)
        pooled = (s * inv_cnt - zg).astype(BF16)
        parts.append(_dot(pooled, pw_ref[gi * _LANE:(gi + 1) * _LANE, :]))
    yb = jnp.concatenate(parts, axis=1) * ps_ref[...]

    y = jnp.concatenate([ya, yb], axis=1).astype(BF16)
    xo_ref[...] = x + _dot(y, wout_ref[...])

    @pl.when(t == pl.num_programs(1) - 1)
    def _():
        for c in range(nca):
            sa_ref[:, c * _LANE:(c + 1) * _LANE] = za_ext[c, ha + tm - (ka - 1):ha + tm, :]
        for c in range(ncb):
            sb_ref[:, c * _LANE:(c + 1) * _LANE] = zb_ext[c, hb + tm - pb:hb + tm, :]

    za_ext[:, 0:ha, :] = za_ext[:, tm:tm + ha, :]
    zb_ext[:, 0:hb, :] = zb_ext[:, tm:tm + hb, :]


def _prompt_even(x, params, big, idx, tm):
    nb, t, d = x.shape
    ops, specs = zip(*params)
    win, wout, pw = big
    ka, d_a = ops[1].shape[-2:]
    d_b = ops[5].shape[-1]
    pb = max(_POOL_WINDOWS) - 1
    ha = _round_up(ka - 1, _SUBLANE)
    hb = _round_up(pb, _SUBLANE)
    assert t % tm == 0 and tm % _CONV_ROWS == 0 and tm >= max(ha, hb)
    assert d_a % _LANE == 0 and d_b == _LANE * len(_POOL_WINDOWS)
    x_spec = pl.BlockSpec((None, tm, d), lambda n, i: (n, i, 0))
    any_spec = pl.BlockSpec(memory_space=pl.ANY)
    kern = functools.partial(_peven_kernel, idx=idx, tm=tm, d_a=d_a, ha=ha, hb=hb)
    return pl.pallas_call(
        kern,
        grid=(nb, t // tm),
        in_specs=[x_spec, *specs, any_spec, any_spec, any_spec],
        out_specs=[x_spec,
                   pl.BlockSpec((None, ka - 1, d_a), lambda n, i: (n, 0, 0)),
                   pl.BlockSpec((None, pb, d_b), lambda n, i: (n, 0, 0))],
        out_shape=[jax.ShapeDtypeStruct((nb, t, d), F32),
                   jax.ShapeDtypeStruct((nb, ka - 1, d_a), F32),
                   jax.ShapeDtypeStruct((nb, pb, d_b), F32)],
        scratch_shapes=[pltpu.VMEM((d_a // _LANE, ha + tm, _LANE), F32),
                        pltpu.VMEM((d_b // _LANE, hb + tm, _LANE), F32),
                        pltpu.VMEM((tm, d_a), F32),
                        *_weight_scratch(*win.shape[1:]), *_weight_scratch(*wout.shape[1:]),
                        *_weight_scratch(*pw.shape[1:])],
        compiler_params=_params("arbitrary", "arbitrary"),
        name="prompt_even",
    )(x, *ops, win, wout, pw)


def _seven_kernel(x_ref, g_ref, cw_ref, cb_ref, lg_ref, lb_ref, ps_ref, ha_ref, hb_ref, win_hbm, wout_hbm, pw_hbm,
                  xo_ref, sa_ref, sb_ref,
                  win_ref, win_stage, win_sem, wout_ref, wout_stage, wout_sem, pw_ref, pw_stage, pw_sem, *, idx):
    _fetch_bf16((win_hbm, idx, win_ref, win_stage, win_sem), (wout_hbm, idx, wout_ref, wout_stage, wout_sem),
                (pw_hbm, idx, pw_ref, pw_stage, pw_sem))
    ka, d_a = cw_ref.shape
    pb = hb_ref.shape[0]
    d_b = hb_ref.shape[2]
    dg = d_b // len(_POOL_WINDOWS)
    x = x_ref[...]
    h = _rms(x, g_ref[...]).astype(BF16)
    p = _dot(h, win_ref[...])
    za = p[:, :d_a] * jax.nn.sigmoid(p[:, d_a:2 * d_a])
    zb = p[:, 2 * d_a:]

    acc = cb_ref[...] + cw_ref[ka - 1:ka, :] * za
    for k in range(ka - 1):
        acc = acc + cw_ref[k:k + 1, :] * ha_ref[k]
    ya = _silu(_layernorm(acc, lg_ref[...], lb_ref[...]))

    parts = []
    for gi, w in enumerate(_POOL_WINDOWS):
        ls = slice(gi * dg, (gi + 1) * dg)
        s = zb[:, ls]
        for j in range(1, w):
            s = s + hb_ref[pb - j, :, ls]
        inv_cnt = 1.0 / float(min(_PAST_LEN + 1, w))
        pooled = (s * inv_cnt - zb[:, ls]).astype(BF16)
        parts.append(_dot(pooled, pw_ref[gi * dg:(gi + 1) * dg, :]))
    yb = jnp.concatenate(parts, axis=1) * ps_ref[...]

    y = jnp.concatenate([ya, yb], axis=1).astype(BF16)
    xo_ref[...] = x + _dot(y, wout_ref[...])

    sa_ref[0:ka - 2] = ha_ref[1:ka - 1]
    sa_ref[ka - 2] = za
    sb_ref[0:pb - 1] = hb_ref[1:pb]
    sb_ref[pb - 1] = zb


def _sample_even(x2d, params, big, idx, hist_a, hist_b):
    b, d = x2d.shape
    ops, specs = zip(*params)
    win, wout, pw = big
    any_spec = pl.BlockSpec(memory_space=pl.ANY)
    return pl.pallas_call(
        functools.partial(_seven_kernel, idx=idx),
        grid=(1,),
        in_specs=[_const_spec((b, d)), *specs, _const_spec(hist_a.shape), _const_spec(hist_b.shape),
                  any_spec, any_spec, any_spec],
        out_specs=[_whole_spec((b, d)), _whole_spec(hist_a.shape), _whole_spec(hist_b.shape)],
        out_shape=[jax.ShapeDtypeStruct((b, d), F32),
                   jax.ShapeDtypeStruct(hist_a.shape, F32),
                   jax.ShapeDtypeStruct(hist_b.shape, F32)],
        scratch_shapes=[*_weight_scratch(*win.shape[1:]), *_weight_scratch(*wout.shape[1:]),
                        *_weight_scratch(*pw.shape[1:])],
        compiler_params=_params("arbitrary"),
        name="sample_even",
    )(x2d, *ops, hist_a, hist_b, win, wout, pw)


def _podd_kernel(x_ref, g_ref, lg_ref, lb_ref, ws_ref, bst_ref, cw_ref, win_hbm, wout_hbm,
                 xo_ref, sv_ref, sd_ref, g_ext, win_ref, win_stage, win_sem, wout_ref, wout_stage, wout_sem,
                 *, idx, tm, d_c, hd_rows):
    t = pl.program_id(1)
    kd, d_d = cw_ref.shape
    heads = ws_ref.shape[0]
    dh = d_c // heads
    ncd = g_ext.shape[0]

    @pl.when(jnp.logical_and(pl.program_id(0) == 0, t == 0))
    def _():
        _fetch_bf16((win_hbm, idx, win_ref, win_stage, win_sem), (wout_hbm, idx, wout_ref, wout_stage, wout_sem))

    @pl.when(t == 0)
    def _():
        g_ext[:, 0:hd_rows, :] = jnp.zeros((ncd, hd_rows, _LANE), F32)

    x = x_ref[...]
    h = _rms(x, g_ref[...]).astype(BF16)
    p = _dot(h, win_ref[...])
    u = p[:, :d_c]
    v = _layernorm(p[:, d_c:2 * d_c], lg_ref[...], lb_ref[...])
    o = 2 * d_c
    gb = p[:, o:o + d_d]
    gated = p[:, o + d_d:o + 2 * d_d] * p[:, o + 2 * d_d:]

    ri = lax.broadcasted_iota(jnp.int32, (_CHUNK, _CHUNK), 0)
    ci = lax.broadcasted_iota(jnp.int32, (_CHUNK, _CHUNK), 1)
    wm = [jnp.where(ci <= ri, ws_ref[hh], 0.0).astype(BF16) for hh in range(heads)]
    vb = v.astype(BF16)
    rows = []
    for c in range(tm // _CHUNK):
        rs = slice(c * _CHUNK, (c + 1) * _CHUNK)
        parts = [_dot(wm[hh], vb[rs, hh * dh:(hh + 1) * dh]) + bst_ref[:, hh:hh + 1] for hh in range(heads)]
        rows.append(jnp.concatenate(parts, axis=1))
    yc = u * jnp.concatenate(rows, axis=0)

    base = hd_rows - (kd - 1)
    cols = []
    for c in range(ncd):
        ls = slice(c * _LANE, (c + 1) * _LANE)
        g_ext[c, hd_rows:hd_rows + tm, :] = gated[:, ls]
        cd = cw_ref[0:1, ls] * g_ext[c, base:base + tm, :]
        for k in range(1, kd):
            cd = cd + cw_ref[k:k + 1, ls] * g_ext[c, base + k:base + k + tm, :]
        cols.append(cd)
    yd = gb * jnp.concatenate(cols, axis=1)

    y = jnp.concatenate([yc, yd], axis=1).astype(BF16)
    xo_ref[...] = x + _dot(y, wout_ref[...])

    @pl.when(t == pl.num_programs(1) - 1)
    def _():
        n_open = sv_ref.shape[0]
        sv_ref[...] = v[tm - n_open:, :]
        for c in range(ncd):
            sd_ref[:, c * _LANE:(c + 1) * _LANE] = g_ext[c, hd_rows + tm - (kd - 1):hd_rows + tm, :]

    g_ext[:, 0:hd_rows, :] = g_ext[:, tm:tm + hd_rows, :]


def _prompt_odd(x, params, big, idx, tm):
    nb, t, d = x.shape
    ops, specs = zip(*params)
    win, wout = big
    d_c = ops[1].shape[-1]
    kd, d_d = ops[5].shape[-2:]
    n_open = t - ((t - 1) // _CHUNK) * _CHUNK
    hd_rows = _round_up(kd - 1, _SUBLANE)
    assert t % tm == 0 and tm % _CHUNK == 0 and n_open % _SUBLANE == 0 and n_open <= tm
    x_spec = pl.BlockSpec((None, tm, d), lambda n, i: (n, i, 0))
    any_spec = pl.BlockSpec(memory_space=pl.ANY)
    kern = functools.partial(_podd_kernel, idx=idx, tm=tm, d_c=d_c, hd_rows=hd_rows)
    return pl.pallas_call(
        kern,
        grid=(nb, t // tm),
        in_specs=[x_spec, *specs, any_spec, any_spec],
        out_specs=[x_spec,
                   pl.BlockSpec((None, n_open, d_c), lambda n, i: (n, 0, 0)),
                   pl.BlockSpec((None, kd - 1, d_d), lambda n, i: (n, 0, 0))],
        out_shape=[jax.ShapeDtypeStruct((nb, t, d), F32),
                   jax.ShapeDtypeStruct((nb, n_open, d_c), F32),
                   jax.ShapeDtypeStruct((nb, kd - 1, d_d), F32)],
        scratch_shapes=[pltpu.VMEM((d_d // _LANE, hd_rows + tm, _LANE), F32),
                        *_weight_scratch(*win.shape[1:]), *_weight_scratch(*wout.shape[1:])],
        compiler_params=_params("arbitrary", "arbitrary"),
        name="prompt_odd",
    )(x, *ops, win, wout)


def _sodd_kernel(x_ref, g_ref, lg_ref, lb_ref, ws0_ref, bs0_ref, cw_ref, hd_ref, win_hbm, wout_hbm,
                 xo_ref, sv_ref, sd_ref, win_ref, win_stage, win_sem, wout_ref, wout_stage, wout_sem, *, idx):
    _fetch_bf16((win_hbm, idx, win_ref, win_stage, win_sem), (wout_hbm, idx, wout_ref, wout_stage, wout_sem))
    kd, d_d = cw_ref.shape
    d_c = lg_ref.shape[1]
    x = x_ref[...]
    h = _rms(x, g_ref[...]).astype(BF16)
    p = _dot(h, win_ref[...])
    u = p[:, :d_c]
    v = _layernorm(p[:, d_c:2 * d_c], lg_ref[...], lb_ref[...])
    o = 2 * d_c
    gb = p[:, o:o + d_d]
    gated = p[:, o + d_d:o + 2 * d_d] * p[:, o + 2 * d_d:]

    yc = u * (ws0_ref[...] * v + bs0_ref[...])

    cd = cw_ref[kd - 1:kd, :] * gated
    for k in range(kd - 1):
        cd = cd + cw_ref[k:k + 1, :] * hd_ref[:, k, :]
    yd = gb * cd

    y = jnp.concatenate([yc, yd], axis=1).astype(BF16)
    xo_ref[...] = x + _dot(y, wout_ref[...])

    sv_ref[...] = v
    for k in range(kd - 2):
        sd_ref[:, k, :] = hd_ref[:, k + 1, :]
    sd_ref[:, kd - 2, :] = gated


def _sample_odd(x2d, params, big, idx, ws0, bs0, hist_d):
    b, d = x2d.shape
    ops, specs = zip(*params)
    win, wout = big
    d_c = ops[1].shape[-1]
    ops = ops[:3] + (ws0, bs0) + ops[5:]
    specs = specs[:3] + (_const_spec(ws0.shape), _const_spec(bs0.shape)) + specs[5:]
    any_spec = pl.BlockSpec(memory_space=pl.ANY)
    return pl.pallas_call(
        functools.partial(_sodd_kernel, idx=idx),
        grid=(1,),
        in_specs=[_const_spec((b, d)), *specs, _const_spec(hist_d.shape), any_spec, any_spec],
        out_specs=[_whole_spec((b, d)), _whole_spec((b, d_c)), _whole_spec(hist_d.shape)],
        out_shape=[jax.ShapeDtypeStruct((b, d), F32),
                   jax.ShapeDtypeStruct((b, d_c), F32),
                   jax.ShapeDtypeStruct(hist_d.shape, F32)],
        scratch_shapes=[*_weight_scratch(*win.shape[1:]), *_weight_scratch(*wout.shape[1:])],
        compiler_params=_params("arbitrary"),
        name="sample_odd",
    )(x2d, *ops, hist_d, win, wout)


def kernel(x_prompt, x_sample, mem_prompt, state_convA, state_poolB, state_convD, cache_mem_k, cache_mem_v, norm_mix, norm_x, norm_ffn, norm_final, w_in_even, conv_a_w, conv_a_b, ln_a_g, ln_a_b, pool_b_w, pool_b_scale, w_out_even, w_in_odd, ln_c_g, ln_c_b, ws_c, bs_c, conv_d_w, w_out_odd, wq_x, wk_x, wv_x, wo_x, w_gate, w_up, w_down):
    nb, seq, d = x_prompt.shape
    db, dec_seq, _ = x_sample.shape
    assert dec_seq == 1
    depth = wq_x.shape[0]
    heads = cache_mem_k.shape[3]

    def unstack(parts):
        return parts[0][None] if len(parts) == 1 else jnp.stack(parts)

    g_fin = norm_final[None]
    pool_w2d = pool_b_w.reshape(pool_b_w.shape[0], -1, pool_b_w.shape[-1])

    p_k, p_v = _kv_proj(mem_prompt, wk_x, wv_x, heads)

    xp = x_prompt
    xs = x_sample.reshape(db, d)
    p_a, p_b, p_c, p_d = [], [], [], []
    s_a, s_b, s_c, s_d = [], [], [], []
    for layer in range(depth):
        i = layer // 2
        if layer % 2 == 0:
            params = [_sel(norm_mix, layer), _sel(conv_a_w, i), _sel(conv_a_b, i), _sel(ln_a_g, i), _sel(ln_a_b, i),
                      _sel(pool_b_scale, i)]
            big = (w_in_even, w_out_even, pool_w2d)
            xp, a, b = _prompt_even(xp, params, big, i, tm=_PROMPT_TILE)
            p_a.append(a)
            p_b.append(b)
            xs, a, b = _sample_even(xs, params, big, i, state_convA[i].transpose(1, 0, 2),
                                    state_poolB[i].transpose(1, 0, 2))
            s_a.append(a.transpose(1, 0, 2))
            s_b.append(b.transpose(1, 0, 2))
        else:
            params = [_sel(norm_mix, layer), _sel(ln_c_g, i), _sel(ln_c_b, i), _sel(ws_c, i),
                      _sel(bs_c.transpose(0, 2, 1), i), _sel(conv_d_w, i)]
            big = (w_in_odd, w_out_odd)
            xp, c, dd = _prompt_odd(xp, params, big, i, tm=_PROMPT_TILE)
            dc = c.shape[-1]
            p_c.append(c.reshape(nb, -1, _C_HEADS, dc // _C_HEADS))
            p_d.append(dd)
            ws0 = jnp.repeat(ws_c[i, :, 0, 0], dc // _C_HEADS)[None]
            bs0 = jnp.repeat(bs_c[i, :, 0], dc // _C_HEADS)[None]
            xs, c, dd = _sample_odd(xs, params, big, i, ws0, bs0, state_convD[i])
            s_c.append(c.reshape(db, 1, _C_HEADS, dc // _C_HEADS))
            s_d.append(dd)

        xp, qs = _attn(xp, xs, norm_x, wq_x, p_k, p_v, wo_x, layer, tm=_ATTN_TILE)

        xp2d, xs = _ffn(xp.reshape(nb * seq, d), xs, qs, cache_mem_k, cache_mem_v, wo_x, norm_ffn, w_gate, w_up,
                        w_down, g_fin, layer, tm=_FFN_TILE, final_norm=layer == depth - 1)
        xp = xp2d.reshape(nb, seq, d)

    return (xp, xs.reshape(db, 1, d), unstack(p_a), unstack(p_b), unstack(p_c), unstack(p_d), p_k, p_v,
            unstack(s_a), unstack(s_b), unstack(s_c), unstack(s_d))
```

```python
import functools

import jax
import jax.numpy as jnp
from jax import lax
from jax.experimental import pallas as pl
from jax.experimental.pallas import tpu as pltpu

F32 = jnp.float32
BF16 = jnp.bfloat16

_EPS = 1e-6
_X_HEADS = 4
_C_HEADS = 4
_CHUNK = 128
_POOL_WINDOWS = (2, 4, 8, 16)
_PAST_LEN = 16384

_SUBLANE = 8
_LANE = 128
_VMEM_LIMIT = 56 * 1024 * 1024
_STAGE_BYTES = 1024 * 1024
_STAGE_SLOTS = 4

_PROMPT_TILE = 1024
_ATTN_TILE = 1024
_ATTN_PARTS = 2
_FFN_TILE = 512
_KV_BATCHES = 4
_FF_CHUNKS = (768, 768, 768, 512)
_CONV_ROWS = 16


def _round_up(n, m):
    return -(-n // m) * m


def _params(*sem):
    return pltpu.CompilerParams(dimension_semantics=sem, vmem_limit_bytes=_VMEM_LIMIT)


def _const_spec(shape):
    nd = len(shape)
    return pl.BlockSpec(shape, lambda *_: (0,) * nd, pipeline_mode=pl.Buffered(1))


def _whole_spec(shape):
    nd = len(shape)
    return pl.BlockSpec(shape, lambda *_: (0,) * nd)


def _sel(stacked, idx):
    if stacked.ndim == 2:
        stacked = stacked[:, None, :]
    tail = stacked.shape[1:]
    spec = pl.BlockSpec((None,) + tail, lambda *_: (idx,) + (0,) * len(tail), pipeline_mode=pl.Buffered(1))
    return stacked, spec


def _chunk_rows(rows, cols):
    cands = [r for r in range(2 * _SUBLANE, rows + 1, 2 * _SUBLANE)
             if rows % r == 0 and r * cols * 4 <= _STAGE_BYTES]
    return max(cands)


def _weight_scratch(rows, cols):
    return [pltpu.VMEM((rows, cols), BF16), pltpu.VMEM((_STAGE_SLOTS, _chunk_rows(rows, cols), cols), F32),
            pltpu.SemaphoreType.DMA((_STAGE_SLOTS,))]


def _fetch_bf16(*jobs):
    rings = {}
    for w_hbm, idx, dst, stage, sem in jobs:
        slots, rc = stage.shape[:2]
        ring = rings.setdefault(id(stage), [])
        for r0 in range(0, dst.shape[0], rc):
            slot = len(ring) % slots
            copy = pltpu.make_async_copy(w_hbm.at[idx, pl.ds(r0, rc), :], stage.at[slot], sem.at[slot])
            ring.append((copy, dst, r0, rc, stage, slot))
    rings = list(rings.values())
    ahead = [ring[0][4].shape[0] - 1 for ring in rings]
    for ring, n in zip(rings, ahead):
        for copy, *_ in ring[:n]:
            copy.start()
    for k in range(max(len(ring) for ring in rings)):
        for ring, n in zip(rings, ahead):
            if k >= len(ring):
                continue
            if k + n < len(ring):
                ring[k + n][0].start()
            copy, dst, r0, rc, stage, slot = ring[k]
            copy.wait()
            dst[r0:r0 + rc, :] = stage[slot].astype(BF16)


def _dot(a, b):
    return jnp.dot(a, b, preferred_element_type=F32)


def _rms(x, g):
    y = x * lax.rsqrt(jnp.mean(x * x, axis=-1, keepdims=True) + _EPS)
    return y * g


def _layernorm(x, g, b):
    mu = jnp.mean(x, axis=-1, keepdims=True)
    d = x - mu
    var = jnp.mean(d * d, axis=-1, keepdims=True)
    return d * lax.rsqrt(var + _EPS) * g + b


def _silu(x):
    return x * jax.nn.sigmoid(x)


_LOG2E = 1.4426950408889634


def _softmax2_rows(s2):
    m = jnp.max(s2, axis=-1, keepdims=True)
    e = jnp.exp2(s2 - m)
    return e * (1.0 / jnp.sum(e, axis=-1, keepdims=True))


_DONE = object()


def _run_staggered(seqs):
    live = []
    pending = list(seqs)
    while live or pending:
        if pending:
            live.append(pending.pop(0))
        live = [s for s in live if next(s, _DONE) is not _DONE]


def _kv_out_copies(kbuf, vbuf, k_hbm, v_hbm, sem, layer, blk, slot):
    _, heads, nbk = kbuf.shape[:3]
    rows = pl.ds(blk * nbk, nbk)
    cps = []
    for hd in range(heads):
        cps.append(pltpu.make_async_copy(kbuf.at[slot, hd], k_hbm.at[layer, rows, :, hd, :], sem.at[slot, 0, hd]))
        cps.append(pltpu.make_async_copy(vbuf.at[slot, hd], v_hbm.at[layer, rows, :, hd, :], sem.at[slot, 1, hd]))
    return cps


def _kv_kernel(m_ref, wk_hbm, wv_hbm, k_hbm, v_hbm, kbuf, vbuf, sem, wk_ref, wv_ref, stage, wsem):
    nj = pl.num_programs(1)
    layer = pl.program_id(0)
    blk = pl.program_id(1)
    step = layer * nj + blk
    last = pl.num_programs(0) * nj - 1
    slot = lax.rem(step, 2)
    _, heads, nbk, mem, dh = kbuf.shape
    copies = functools.partial(_kv_out_copies, kbuf, vbuf, k_hbm, v_hbm, sem)

    @pl.when(blk == 0)
    def _():
        _fetch_bf16((wk_hbm, layer, wk_ref, stage, wsem), (wv_hbm, layer, wv_ref, stage, wsem))

    @pl.when(step >= 2)
    def _():
        prev = step - 2
        for cp in copies(lax.div(prev, nj), lax.rem(prev, nj), slot):
            cp.wait()

    m = m_ref[...].reshape(nbk * mem, m_ref.shape[-1]).astype(BF16)
    k = _dot(m, wk_ref[...])
    v = _dot(m, wv_ref[...])
    for hd in range(heads):
        for b in range(nbk):
            kbuf[slot, hd, b] = k[b * mem:(b + 1) * mem, hd * dh:(hd + 1) * dh]
            vbuf[slot, hd, b] = v[b * mem:(b + 1) * mem, hd * dh:(hd + 1) * dh]
    for cp in copies(layer, blk, slot):
        cp.start()

    @pl.when(step == last)
    def _():
        @pl.when(step >= 1)
        def _():
            prev = step - 1
            for cp in copies(lax.div(prev, nj), lax.rem(prev, nj), 1 - slot):
                cp.wait()
        for cp in copies(layer, blk, slot):
            cp.wait()


def _kv_proj(mem3, wk, wv, heads):
    nb, mem, d = mem3.shape
    depth = wk.shape[0]
    dh = d // heads
    nbk = _KV_BATCHES
    assert nb % nbk == 0 and mem % _SUBLANE == 0
    any_spec = pl.BlockSpec(memory_space=pl.ANY)
    buf = pltpu.VMEM((2, heads, nbk, mem, dh), F32)
    wk_scr, stage, wsem = _weight_scratch(d, d)
    return pl.pallas_call(
        _kv_kernel,
        grid=(depth, nb // nbk),
        in_specs=[pl.BlockSpec((nbk, mem, d), lambda l, j: (j, 0, 0)), any_spec, any_spec],
        out_specs=[any_spec, any_spec],
        out_shape=[jax.ShapeDtypeStruct((depth, nb, mem, heads, dh), F32)] * 2,
        scratch_shapes=[buf, buf, pltpu.SemaphoreType.DMA((2, 2, heads)), wk_scr, wk_scr, stage, wsem],
        compiler_params=_params("arbitrary", "arbitrary"),
        name="kv_proj",
    )(mem3, wk, wv)


def _sattn_copies(k_hbm, v_hbm, cbuf, sem, layer, row0, slot):
    _, _, heads, hb = cbuf.shape[:4]
    rows = pl.ds(row0, hb)
    cps = []
    for j, src in enumerate((k_hbm, v_hbm)):
        for hd in range(heads):
            cps.append(pltpu.make_async_copy(src.at[layer, rows, :, hd, :], cbuf.at[slot, j, hd], sem.at[slot, j, hd]))
    return cps


def _sample_attn_rows(qs_ref, o_scr, cbuf, slot, row0):
    _, _, heads, hb, _, dh = cbuf.shape
    for j in range(hb):
        outs = []
        for hd in range(heads):
            q_row = qs_ref[pl.ds(row0 + j, 1), hd * dh:(hd + 1) * dh]
            s = jnp.sum(cbuf[slot, 0, hd, j] * q_row, axis=1, keepdims=True)
            e = jnp.exp2(s - jnp.max(s, axis=0, keepdims=True))
            den = jnp.sum(e, axis=0, keepdims=True)
            num = jnp.sum(e * cbuf[slot, 1, hd, j], axis=0, keepdims=True)
            outs.append(num * (1.0 / den))
            yield
        o_scr[pl.ds(row0 + j, 1), :] = jnp.concatenate(outs, axis=1)


def _ffn_kernel(xp_ref, xs_ref, qs_ref, g_ref, gf_ref, ck_hbm, cv_hbm, wg_hbm, wu_hbm, wd_hbm, wo_hbm,
                op_ref, os_ref, wg_ref, wu_ref, up_stage, up_sem, wd_ref, dn_stage, dn_sem, wo_ref,
                o_scr, cbuf, csem, *, layer, chunks, final_norm):
    step = pl.program_id(0)
    n_tiles = pl.num_programs(0) - 1
    slots, hb = cbuf.shape[0], cbuf.shape[3]
    per_step = slots * hb
    ccopies = functools.partial(_sattn_copies, ck_hbm, cv_hbm, cbuf, csem, layer)

    @pl.when(step == 0)
    def _():
        for sl in range(slots):
            for cp in ccopies(sl * hb, sl):
                cp.start()
        _fetch_bf16((wg_hbm, layer, wg_ref, up_stage, up_sem), (wu_hbm, layer, wu_ref, up_stage, up_sem),
                    (wd_hbm, layer, wd_ref, dn_stage, dn_sem), (wo_hbm, layer, wo_ref, dn_stage, dn_sem))

    def swiglu(x, o_ref):
        h = _rms(x, g_ref[...]).astype(BF16)
        yield
        acc = x
        lo = 0
        for width in chunks:
            gate = _dot(h, wg_ref[:, lo:lo + width])
            up = _dot(h, wu_ref[:, lo:lo + width])
            yield
            act = (_silu(gate) * up).astype(BF16)
            yield
            acc = acc + _dot(act, wd_ref[lo:lo + width, :])
            yield
            lo += width
        if final_norm:
            acc = _rms(acc, gf_ref[...])
        o_ref[...] = acc

    def sample_rows():
        nxt = jnp.minimum(step + 1, n_tiles - 1) * per_step
        for sl in range(slots):
            row0 = step * per_step + sl * hb
            for cp in ccopies(row0, sl):
                cp.wait()
            yield from _sample_attn_rows(qs_ref, o_scr, cbuf, sl, row0)
            for cp in ccopies(nxt + sl * hb, sl):
                cp.start()
            yield

    @pl.when(step < n_tiles)
    def _():
        _run_staggered([swiglu(xp_ref[...], op_ref), sample_rows()])

    @pl.when(step == n_tiles)
    def _():
        for sl in range(slots):
            for cp in ccopies((n_tiles - 1) * per_step + sl * hb, sl):
                cp.wait()
        x = xs_ref[...] + _dot(o_scr[...].astype(BF16), wo_ref[...])
        _run_staggered([swiglu(x, os_ref)])


def _ffn(xp2d, xs2d, qs2d, ck5, cv5, wo, g, wg, wu, wd, gf, layer, tm, final_norm):
    rows, d = xp2d.shape
    b = xs2d.shape[0]
    ff = wg.shape[-1]
    _, _, mem_len, heads, dh = ck5.shape
    n_tiles = rows // tm
    slots = 2
    assert sum(_FF_CHUNKS) == ff and rows % tm == 0 and b % (n_tiles * slots) == 0
    hb = b // (n_tiles * slots)
    kern = functools.partial(_ffn_kernel, layer=layer, chunks=_FF_CHUNKS, final_norm=final_norm)
    row_spec = pl.BlockSpec((tm, d), lambda i: (jnp.minimum(i, n_tiles - 1), 0))
    any_spec = pl.BlockSpec(memory_space=pl.ANY)
    ops, specs = zip(_sel(g, layer), _sel(gf, 0))
    wg_scr, up_stage, up_sem = _weight_scratch(d, ff)
    wd_scr, dn_stage, dn_sem = _weight_scratch(ff, d)
    assert dn_stage.shape[1:] == _weight_scratch(d, d)[1].shape[1:]
    return pl.pallas_call(
        kern,
        grid=(n_tiles + 1,),
        in_specs=[row_spec, _const_spec((b, d)), _const_spec((b, d)), *specs] + [any_spec] * 6,
        out_specs=[row_spec, _whole_spec((b, d))],
        out_shape=[jax.ShapeDtypeStruct((rows, d), F32), jax.ShapeDtypeStruct((b, d), F32)],
        scratch_shapes=[wg_scr, wg_scr, up_stage, up_sem, wd_scr, dn_stage, dn_sem, pltpu.VMEM((d, d), BF16),
                        pltpu.VMEM((b, d), F32), pltpu.VMEM((slots, 2, heads, hb, mem_len, dh), F32),
                        pltpu.SemaphoreType.DMA((slots, 2, heads))],
        compiler_params=_params("arbitrary"),
        name="ffn",
    )(xp2d, xs2d, qs2d, *ops, ck5, cv5, wg, wu, wd, wo)


def _pattn_kv_copies(k_hbm, v_hbm, kvbuf, sem, layer, n, slot):
    heads = kvbuf.shape[2]
    cps = []
    for j, src in enumerate((k_hbm, v_hbm)):
        for hd in range(heads):
            cps.append(pltpu.make_async_copy(src.at[layer, n, :, hd, :], kvbuf.at[slot, j, hd], sem.at[slot, j, hd]))
    return cps


def _attn_kernel(xp_ref, xs_ref, g_ref, pk_hbm, pv_hbm, wq_hbm, wo_hbm, op_ref, qs_ref,
                 kvbuf, kv16, psem, wq_ref, wo_ref, stage, wsem, *, layer, parts):
    n = pl.program_id(0)
    i = pl.program_id(1)
    heads, _, dh = kv16.shape[1:]
    tm = xp_ref.shape[0]
    rows = tm // parts
    pslot = lax.rem(n, 2)
    pcopies = functools.partial(_pattn_kv_copies, pk_hbm, pv_hbm, kvbuf, psem, layer)

    @pl.when(jnp.logical_and(n == 0, i == 0))
    def _():
        for cp in pcopies(0, 0):
            cp.start()
        _fetch_bf16((wq_hbm, layer, wq_ref, stage, wsem), (wo_hbm, layer, wo_ref, stage, wsem))
        h = _rms(xs_ref[...], g_ref[...]).astype(BF16)
        qs_ref[...] = _dot(h, wq_ref[...]) * (dh ** -0.5 * _LOG2E)

    @pl.when(i == 0)
    def _():
        @pl.when(n + 1 < pl.num_programs(0))
        def _():
            for cp in pcopies(n + 1, 1 - pslot):
                cp.start()

        for cp in pcopies(n, pslot):
            cp.wait()
        kv16[...] = kvbuf[pslot].astype(BF16)

    def attend(r0):
        x = xp_ref[r0:r0 + rows, :]
        h = _rms(x, g_ref[...]).astype(BF16)
        yield
        q = _dot(h, wq_ref[...]).astype(BF16)
        yield
        outs = []
        for hd in range(heads):
            s = lax.dot_general(q[:, hd * dh:(hd + 1) * dh], kv16[0, hd], (((1,), (1,)), ((), ())),
                                preferred_element_type=F32)
            yield
            a = _softmax2_rows(s * (dh ** -0.5 * _LOG2E)).astype(BF16)
            yield
            outs.append(_dot(a, kv16[1, hd]))
            yield
        o = jnp.concatenate(outs, axis=1).astype(BF16)
        op_ref[r0:r0 + rows, :] = x + _dot(o, wo_ref[...])
        yield

    _run_staggered([attend(p * rows) for p in range(parts)])


def _attn(xp, xs, g, wq, pk5, pv5, wo, layer, tm):
    nb, t, d = xp.shape
    b = xs.shape[0]
    _, _, mem_len, heads, dh = pk5.shape
    assert t % tm == 0 and tm % (_ATTN_PARTS * _SUBLANE) == 0
    x_spec = pl.BlockSpec((None, tm, d), lambda n, i: (n, i, 0))
    any_spec = pl.BlockSpec(memory_space=pl.ANY)
    g, g_spec = _sel(g, layer)
    wq_scr, stage, wsem = _weight_scratch(d, d)
    return pl.pallas_call(
        functools.partial(_attn_kernel, layer=layer, parts=_ATTN_PARTS),
        grid=(nb, t // tm),
        in_specs=[x_spec, _const_spec((b, d)), g_spec] + [any_spec] * 4,
        out_specs=[x_spec, _whole_spec((b, d))],
        out_shape=[jax.ShapeDtypeStruct((nb, t, d), F32), jax.ShapeDtypeStruct((b, d), F32)],
        scratch_shapes=[pltpu.VMEM((2, 2, heads, mem_len, dh), F32), pltpu.VMEM((2, heads, mem_len, dh), BF16),
                        pltpu.SemaphoreType.DMA((2, 2, heads)), wq_scr, wq_scr, stage, wsem],
        compiler_params=_params("arbitrary", "arbitrary"),
        name="attn",
    )(xp, xs, g, pk5, pv5, wq, wo)


def _peven_kernel(x_ref, g_ref, cw_ref, cb_ref, lg_ref, lb_ref, ps_ref, win_hbm, wout_hbm, pw_hbm,
                  xo_ref, sa_ref, sb_ref, za_ext, zb_ext, ca_buf,
                  win_ref, win_stage, win_sem, wout_ref, wout_stage, wout_sem, pw_ref, pw_stage, pw_sem,
                  *, idx, tm, d_a, ha, hb):
    t = pl.program_id(1)
    ka = cw_ref.shape[0]
    nca = za_ext.shape[0]
    ncb = zb_ext.shape[0]
    pb = sb_ref.shape[0]

    @pl.when(jnp.logical_and(pl.program_id(0) == 0, t == 0))
    def _():
        _fetch_bf16((win_hbm, idx, win_ref, win_stage, win_sem), (wout_hbm, idx, wout_ref, wout_stage, wout_sem),
                    (pw_hbm, idx, pw_ref, pw_stage, pw_sem))

    @pl.when(t == 0)
    def _():
        za_ext[:, 0:ha, :] = jnp.zeros((nca, ha, _LANE), F32)
        zb_ext[:, 0:hb, :] = jnp.zeros((ncb, hb, _LANE), F32)

    x = x_ref[...]
    h = _rms(x, g_ref[...]).astype(BF16)
    p = _dot(h, win_ref[...])
    za = p[:, :d_a] * jax.nn.sigmoid(p[:, d_a:2 * d_a])
    zb = p[:, 2 * d_a:]
    for c in range(nca):
        za_ext[c, ha:ha + tm, :] = za[:, c * _LANE:(c + 1) * _LANE]
    for c in range(ncb):
        zb_ext[c, hb:hb + tm, :] = zb[:, c * _LANE:(c + 1) * _LANE]

    base = ha - (ka - 1)
    for c in range(nca):
        ls = slice(c * _LANE, (c + 1) * _LANE)
        for r in range(tm // _CONV_ROWS):
            r0 = r * _CONV_ROWS
            acc = jnp.broadcast_to(cb_ref[:, ls], (_CONV_ROWS, _LANE))
            for k in range(ka):
                acc = acc + cw_ref[k:k + 1, ls] * za_ext[c, base + r0 + k:base + r0 + k + _CONV_ROWS, :]
            ca_buf[r0:r0 + _CONV_ROWS, ls] = acc
    ya = _silu(_layernorm(ca_buf[...], lg_ref[...], lb_ref[...]))

    pos = t * tm + lax.broadcasted_iota(jnp.int32, (tm, 1), 0)
    parts = []
    for gi, w in enumerate(_POOL_WINDOWS):
        zg = zb[:, gi * _LANE:(gi + 1) * _LANE]
        s = zg
        for j in range(1, w):
            s = s + zb_ext[gi, hb - j:hb - j + tm, :]
        inv_cnt = 1.0 / jnp.minimum(pos + 1, w).astype(F32)
        pooled = (s * inv_cnt - zg).astype(BF16)
        parts.append(_dot(pooled, pw_ref[gi * _LANE:(gi + 1) * _LANE, :]))
    yb = jnp.concatenate(parts, axis=1) * ps_ref[...]

    y = jnp.concatenate([ya, yb], axis=1).astype(BF16)
    xo_ref[...] = x + _dot(y, wout_ref[...])

    @pl.when(t == pl.num_programs(1) - 1)
    def _():
        for c in range(nca):
            sa_ref[:, c * _LANE:(c + 1) * _LANE] = za_ext[c, ha + tm - (ka - 1):ha + tm, :]
        for c in range(ncb):
            sb_ref[:, c * _LANE:(c + 1) * _LANE] = zb_ext[c, hb + tm - pb:hb + tm, :]

    za_ext[:, 0:ha, :] = za_ext[:, tm:tm + ha, :]
    zb_ext[:, 0:hb, :] = zb_ext[:, tm:tm + hb, :]


def _prompt_even(x, params, big, idx, tm):
    nb, t, d = x.shape
    ops, specs = zip(*params)
    win, wout, pw = big
    ka, d_a = ops[1].shape[-2:]
    d_b = ops[5].shape[-1]
    pb = max(_POOL_WINDOWS) - 1
    ha = _round_up(ka - 1, _SUBLANE)
    hb = _round_up(pb, _SUBLANE)
    assert t % tm == 0 and tm % _CONV_ROWS == 0 and tm >= max(ha, hb)
    assert d_a % _LANE == 0 and d_b == _LANE * len(_POOL_WINDOWS)
    x_spec = pl.BlockSpec((None, tm, d), lambda n, i: (n, i, 0))
    any_spec = pl.BlockSpec(memory_space=pl.ANY)
    kern = functools.partial(_peven_kernel, idx=idx, tm=tm, d_a=d_a, ha=ha, hb=hb)
    return pl.pallas_call(
        kern,
        grid=(nb, t // tm),
        in_specs=[x_spec, *specs, any_spec, any_spec, any_spec],
        out_specs=[x_spec,
                   pl.BlockSpec((None, ka - 1, d_a), lambda n, i: (n, 0, 0)),
                   pl.BlockSpec((None, pb, d_b), lambda n, i: (n, 0, 0))],
        out_shape=[jax.ShapeDtypeStruct((nb, t, d), F32),
                   jax.ShapeDtypeStruct((nb, ka - 1, d_a), F32),
                   jax.ShapeDtypeStruct((nb, pb, d_b), F32)],
        scratch_shapes=[pltpu.VMEM((d_a // _LANE, ha + tm, _LANE), F32),
                        pltpu.VMEM((d_b // _LANE, hb + tm, _LANE), F32),
                        pltpu.VMEM((tm, d_a), F32),
                        *_weight_scratch(*win.shape[1:]), *_weight_scratch(*wout.shape[1:]),
                        *_weight_scratch(*pw.shape[1:])],
        compiler_params=_params("arbitrary", "arbitrary"),
        name="prompt_even",
    )(x, *ops, win, wout, pw)


def _seven_kernel(x_ref, g_ref, cw_ref, cb_ref, lg_ref, lb_ref, ps_ref, ha_ref, hb_ref, win_hbm, wout_hbm, pw_hbm,
                  xo_ref, sa_ref, sb_ref,
                  win_ref, win_stage, win_sem, wout_ref, wout_stage, wout_sem, pw_ref, pw_stage, pw_sem, *, idx):
    _fetch_bf16((win_hbm, idx, win_ref, win_stage, win_sem), (wout_hbm, idx, wout_ref, wout_stage, wout_sem),
                (pw_hbm, idx, pw_ref, pw_stage, pw_sem))
    ka, d_a = cw_ref.shape
    pb = hb_ref.shape[0]
    d_b = hb_ref.shape[2]
    dg = d_b // len(_POOL_WINDOWS)
    x = x_ref[...]
    h = _rms(x, g_ref[...]).astype(BF16)
    p = _dot(h, win_ref[...])
    za = p[:, :d_a] * jax.nn.sigmoid(p[:, d_a:2 * d_a])
    zb = p[:, 2 * d_a:]

    acc = cb_ref[...] + cw_ref[ka - 1:ka, :] * za
    for k in range(ka - 1):
        acc = acc + cw_ref[k:k + 1, :] * ha_ref[k]
    ya = _silu(_layernorm(acc, lg_ref[...], lb_ref[...]))

    parts = []
    for gi, w in enumerate(_POOL_WINDOWS):
        ls = slice(gi * dg, (gi + 1) * dg)
        s = zb[:, ls]
        for j in range(1, w):
            s = s + hb_ref[pb - j, :, ls]
        inv_cnt = 1.0 / float(min(_PAST_LEN + 1, w))
        pooled = (s * inv_cnt - zb[:, ls]).astype(BF16)
        parts.append(_dot(pooled, pw_ref[gi * dg:(gi + 1) * dg, :]))
    yb = jnp.concatenate(parts, axis=1) * ps_ref[...]

    y = jnp.concatenate([ya, yb], axis=1).astype(BF16)
    xo_ref[...] = x + _dot(y, wout_ref[...])

    sa_ref[0:ka - 2] = ha_ref[1:ka - 1]
    sa_ref[ka - 2] = za
    sb_ref[0:pb - 1] = hb_ref[1:pb]
    sb_ref[pb - 1] = zb


def _sample_even(x2d, params, big, idx, hist_a, hist_b):
    b, d = x2d.shape
    ops, specs = zip(*params)
    win, wout, pw = big
    any_spec = pl.BlockSpec(memory_space=pl.ANY)
    return pl.pallas_call(
        functools.partial(_seven_kernel, idx=idx),
        grid=(1,),
        in_specs=[_const_spec((b, d)), *specs, _const_spec(hist_a.shape), _const_spec(hist_b.shape),
                  any_spec, any_spec, any_spec],
        out_specs=[_whole_spec((b, d)), _whole_spec(hist_a.shape), _whole_spec(hist_b.shape)],
        out_shape=[jax.ShapeDtypeStruct((b, d), F32),
                   jax.ShapeDtypeStruct(hist_a.shape, F32),
                   jax.ShapeDtypeStruct(hist_b.shape, F32)],
        scratch_shapes=[*_weight_scratch(*win.shape[1:]), *_weight_scratch(*wout.shape[1:]),
                        *_weight_scratch(*pw.shape[1:])],
        compiler_params=_params("arbitrary"),
        name="sample_even",
    )(x2d, *ops, hist_a, hist_b, win, wout, pw)


def _podd_kernel(x_ref, g_ref, lg_ref, lb_ref, ws_ref, bst_ref, cw_ref, win_hbm, wout_hbm,
                 xo_ref, sv_ref, sd_ref, g_ext, win_ref, win_stage, win_sem, wout_ref, wout_stage, wout_sem,
                 *, idx, tm, d_c, hd_rows):
    t = pl.program_id(1)
    kd, d_d = cw_ref.shape
    heads = ws_ref.shape[0]
    dh = d_c // heads
    ncd = g_ext.shape[0]

    @pl.when(jnp.logical_and(pl.program_id(0) == 0, t == 0))
    def _():
        _fetch_bf16((win_hbm, idx, win_ref, win_stage, win_sem), (wout_hbm, idx, wout_ref, wout_stage, wout_sem))

    @pl.when(t == 0)
    def _():
        g_ext[:, 0:hd_rows, :] = jnp.zeros((ncd, hd_rows, _LANE), F32)

    x = x_ref[...]
    h = _rms(x, g_ref[...]).astype(BF16)
    p = _dot(h, win_ref[...])
    u = p[:, :d_c]
    v = _layernorm(p[:, d_c:2 * d_c], lg_ref[...], lb_ref[...])
    o = 2 * d_c
    gb = p[:, o:o + d_d]
    gated = p[:, o + d_d:o + 2 * d_d] * p[:, o + 2 * d_d:]

    ri = lax.broadcasted_iota(jnp.int32, (_CHUNK, _CHUNK), 0)
    ci = lax.broadcasted_iota(jnp.int32, (_CHUNK, _CHUNK), 1)
    wm = [jnp.where(ci <= ri, ws_ref[hh], 0.0).astype(BF16) for hh in range(heads)]
    vb = v.astype(BF16)
    rows = []
    for c in range(tm // _CHUNK):
        rs = slice(c * _CHUNK, (c + 1) * _CHUNK)
        parts = [_dot(wm[hh], vb[rs, hh * dh:(hh + 1) * dh]) + bst_ref[:, hh:hh + 1] for hh in range(heads)]
        rows.append(jnp.concatenate(parts, axis=1))
    yc = u * jnp.concatenate(rows, axis=0)

    base = hd_rows - (kd - 1)
    cols = []
    for c in range(ncd):
        ls = slice(c * _LANE, (c + 1) * _LANE)
        g_ext[c, hd_rows:hd_rows + tm, :] = gated[:, ls]
        cd = cw_ref[0:1, ls] * g_ext[c, base:base + tm, :]
        for k in range(1, kd):
            cd = cd + cw_ref[k:k + 1, ls] * g_ext[c, base + k:base + k + tm, :]
        cols.append(cd)
    yd = gb * jnp.concatenate(cols, axis=1)

    y = jnp.concatenate([yc, yd], axis=1).astype(BF16)
    xo_ref[...] = x + _dot(y, wout_ref[...])

    @pl.when(t == pl.num_programs(1) - 1)
    def _():
        n_open = sv_ref.shape[0]
        sv_ref[...] = v[tm - n_open:, :]
        for c in range(ncd):
            sd_ref[:, c * _LANE:(c + 1) * _LANE] = g_ext[c, hd_rows + tm - (kd - 1):hd_rows + tm, :]

    g_ext[:, 0:hd_rows, :] = g_ext[:, tm:tm + hd_rows, :]


def _prompt_odd(x, params, big, idx, tm):
    nb, t, d = x.shape
    ops, specs = zip(*params)
    win, wout = big
    d_c = ops[1].shape[-1]
    kd, d_d = ops[5].shape[-2:]
    n_open = t - ((t - 1) // _CHUNK) * _CHUNK
    hd_rows = _round_up(kd - 1, _SUBLANE)
    assert t % tm == 0 and tm % _CHUNK == 0 and n_open % _SUBLANE == 0 and n_open <= tm
    x_spec = pl.BlockSpec((None, tm, d), lambda n, i: (n, i, 0))
    any_spec = pl.BlockSpec(memory_space=pl.ANY)
    kern = functools.partial(_podd_kernel, idx=idx, tm=tm, d_c=d_c, hd_rows=hd_rows)
    return pl.pallas_call(
        kern,
        grid=(nb, t // tm),
        in_specs=[x_spec, *specs, any_spec, any_spec],
        out_specs=[x_spec,
                   pl.BlockSpec((None, n_open, d_c), lambda n, i: (n, 0, 0)),
                   pl.BlockSpec((None, kd - 1, d_d), lambda n, i: (n, 0, 0))],
        out_shape=[jax.ShapeDtypeStruct((nb, t, d), F32),
                   jax.ShapeDtypeStruct((nb, n_open, d_c), F32),
                   jax.ShapeDtypeStruct((nb, kd - 1, d_d), F32)],
        scratch_shapes=[pltpu.VMEM((d_d // _LANE, hd_rows + tm, _LANE), F32),
                        *_weight_scratch(*win.shape[1:]), *_weight_scratch(*wout.shape[1:])],
        compiler_params=_params("arbitrary", "arbitrary"),
        name="prompt_odd",
    )(x, *ops, win, wout)


def _sodd_kernel(x_ref, g_ref, lg_ref, lb_ref, ws0_ref, bs0_ref, cw_ref, hd_ref, win_hbm, wout_hbm,
                 xo_ref, sv_ref, sd_ref, win_ref, win_stage, win_sem, wout_ref, wout_stage, wout_sem, *, idx):
    _fetch_bf16((win_hbm, idx, win_ref, win_stage, win_sem), (wout_hbm, idx, wout_ref, wout_stage, wout_sem))
    kd, d_d = cw_ref.shape
    d_c = lg_ref.shape[1]
    x = x_ref[...]
    h = _rms(x, g_ref[...]).astype(BF16)
    p = _dot(h, win_ref[...])
    u = p[:, :d_c]
    v = _layernorm(p[:, d_c:2 * d_c], lg_ref[...], lb_ref[...])
    o = 2 * d_c
    gb = p[:, o:o + d_d]
    gated = p[:, o + d_d:o + 2 * d_d] * p[:, o + 2 * d_d:]

    yc = u * (ws0_ref[...] * v + bs0_ref[...])

    cd = cw_ref[kd - 1:kd, :] * gated
    for k in range(kd - 1):
        cd = cd + cw_ref[k:k + 1, :] * hd_ref[:, k, :]
    yd = gb * cd

    y = jnp.concatenate([yc, yd], axis=1).astype(BF16)
    xo_ref[...] = x + _dot(y, wout_ref[...])

    sv_ref[...] = v
    for k in range(kd - 2):
        sd_ref[:, k, :] = hd_ref[:, k + 1, :]
    sd_ref[:, kd - 2, :] = gated


def _sample_odd(x2d, params, big, idx, ws0, bs0, hist_d):
    b, d = x2d.shape
    ops, specs = zip(*params)
    win, wout = big
    d_c = ops[1].shape[-1]
    ops = ops[:3] + (ws0, bs0) + ops[5:]
    specs = specs[:3] + (_const_spec(ws0.shape), _const_spec(bs0.shape)) + specs[5:]
    any_spec = pl.BlockSpec(memory_space=pl.ANY)
    return pl.pallas_call(
        functools.partial(_sodd_kernel, idx=idx),
        grid=(1,),
        in_specs=[_const_spec((b, d)), *specs, _const_spec(hist_d.shape), any_spec, any_spec],
        out_specs=[_whole_spec((b, d)), _whole_spec((b, d_c)), _whole_spec(hist_d.shape)],
        out_shape=[jax.ShapeDtypeStruct((b, d), F32),
                   jax.ShapeDtypeStruct((b, d_c), F32),
                   jax.ShapeDtypeStruct(hist_d.shape, F32)],
        scratch_shapes=[*_weight_scratch(*win.shape[1:]), *_weight_scratch(*wout.shape[1:])],
        compiler_params=_params("arbitrary"),
        name="sample_odd",
    )(x2d, *ops, hist_d, win, wout)


def kernel(x_prompt, x_sample, mem_prompt, state_convA, state_poolB, state_convD, cache_mem_k, cache_mem_v, norm_mix, norm_x, norm_ffn, norm_final, w_in_even, conv_a_w, conv_a_b, ln_a_g, ln_a_b, pool_b_w, pool_b_scale, w_out_even, w_in_odd, ln_c_g, ln_c_b, ws_c, bs_c, conv_d_w, w_out_odd, wq_x, wk_x, wv_x, wo_x, w_gate, w_up, w_down):
    nb, seq, d = x_prompt.shape
    db, dec_seq, _ = x_sample.shape
    assert dec_seq == 1
    depth = wq_x.shape[0]
    heads = cache_mem_k.shape[3]

    def unstack(parts):
        return parts[0][None] if len(parts) == 1 else jnp.stack(parts)

    g_fin = norm_final[None]
    pool_w2d = pool_b_w.reshape(pool_b_w.shape[0], -1, pool_b_w.shape[-1])

    p_k, p_v = _kv_proj(mem_prompt, wk_x, wv_x, heads)

    xp = x_prompt
    xs = x_sample.reshape(db, d)
    p_a, p_b, p_c, p_d = [], [], [], []
    s_a, s_b, s_c, s_d = [], [], [], []
    for layer in range(depth):
        i = layer // 2
        if layer % 2 == 0:
            params = [_sel(norm_mix, layer), _sel(conv_a_w, i), _sel(conv_a_b, i), _sel(ln_a_g, i), _sel(ln_a_b, i),
                      _sel(pool_b_scale, i)]
            big = (w_in_even, w_out_even, pool_w2d)
            xp, a, b = _prompt_even(xp, params, big, i, tm=_PROMPT_TILE)
            p_a.append(a)
            p_b.append(b)
            xs, a, b = _sample_even(xs, params, big, i, state_convA[i].transpose(1, 0, 2),
                                    state_poolB[i].transpose(1, 0, 2))
            s_a.append(a.transpose(1, 0, 2))
            s_b.append(b.transpose(1, 0, 2))
        else:
            params = [_sel(norm_mix, layer), _sel(ln_c_g, i), _sel(ln_c_b, i), _sel(ws_c, i),
                      _sel(bs_c.transpose(0, 2, 1), i), _sel(conv_d_w, i)]
            big = (w_in_odd, w_out_odd)
            xp, c, dd = _prompt_odd(xp, params, big, i, tm=_PROMPT_TILE)
            dc = c.shape[-1]
            p_c.append(c.reshape(nb, -1, _C_HEADS, dc // _C_HEADS))
            p_d.append(dd)
            ws0 = jnp.repeat(ws_c[i, :, 0, 0], dc // _C_HEADS)[None]
            bs0 = jnp.repeat(bs_c[i, :, 0], dc // _C_HEADS)[None]
            xs, c, dd = _sample_odd(xs, params, big, i, ws0, bs0, state_convD[i])
            s_c.append(c.reshape(db, 1, _C_HEADS, dc // _C_HEADS))
            s_d.append(dd)

        xp, qs = _attn(xp, xs, norm_x, wq_x, p_k, p_v, wo_x, layer, tm=_ATTN_TILE)

        xp2d, xs = _ffn(xp.reshape(nb * seq, d), xs, qs, cache_mem_k, cache_mem_v, wo_x, norm_ffn, w_gate, w_up,
                        w_down, g_fin, layer, tm=_FFN_TILE, final_norm=layer == depth - 1)
        xp = xp2d.reshape(nb, seq, d)

    return (xp, xs.reshape(db, 1, d), unstack(p_a), unstack(p_b), unstack(p_c), unstack(p_d), p_k, p_v,
            unstack(s_a), unstack(s_b), unstack(s_c), unstack(s_d))
```

```python
import functools

import jax
import jax.numpy as jnp
from jax import lax
from jax.experimental import pallas as pl
from jax.experimental.pallas import tpu as pltpu

F32 = jnp.float32
BF16 = jnp.bfloat16

_EPS = 1e-6
_X_HEADS = 4
_C_HEADS = 4
_CHUNK = 128
_POOL_WINDOWS = (2, 4, 8, 16)
_PAST_LEN = 16384

_SUBLANE = 8
_LANE = 128
_VMEM_LIMIT = 56 * 1024 * 1024
_STAGE_BYTES = 1024 * 1024
_STAGE_SLOTS = 4

_PROMPT_TILE = 512
_ATTN_TILE = 512
_ATTN_PARTS = 2
_FFN_TILE = 512
_KV_BATCHES = 4
_FF_CHUNKS = (768, 768, 768, 512)
_CONV_ROWS = 16


def _round_up(n, m):
    return -(-n // m) * m


def _params(*sem):
    return pltpu.CompilerParams(dimension_semantics=sem, vmem_limit_bytes=_VMEM_LIMIT)


def _const_spec(shape):
    nd = len(shape)
    return pl.BlockSpec(shape, lambda *_: (0,) * nd, pipeline_mode=pl.Buffered(1))


def _whole_spec(shape):
    nd = len(shape)
    return pl.BlockSpec(shape, lambda *_: (0,) * nd)


def _sel(stacked, idx):
    if stacked.ndim == 2:
        stacked = stacked[:, None, :]
    tail = stacked.shape[1:]
    spec = pl.BlockSpec((None,) + tail, lambda *_: (idx,) + (0,) * len(tail), pipeline_mode=pl.Buffered(1))
    return stacked, spec


def _chunk_rows(rows, cols):
    cands = [r for r in range(2 * _SUBLANE, rows + 1, 2 * _SUBLANE)
             if rows % r == 0 and r * cols * 4 <= _STAGE_BYTES]
    return max(cands)


def _weight_scratch(rows, cols):
    return [pltpu.VMEM((rows, cols), BF16), pltpu.VMEM((_STAGE_SLOTS, _chunk_rows(rows, cols), cols), F32),
            pltpu.SemaphoreType.DMA((_STAGE_SLOTS,))]


def _fetch_bf16(*jobs):
    rings = {}
    for w_hbm, idx, dst, stage, sem in jobs:
        slots, rc = stage.shape[:2]
        ring = rings.setdefault(id(stage), [])
        for r0 in range(0, dst.shape[0], rc):
            slot = len(ring) % slots
            copy = pltpu.make_async_copy(w_hbm.at[idx, pl.ds(r0, rc), :], stage.at[slot], sem.at[slot])
            ring.append((copy, dst, r0, rc, stage, slot))
    rings = list(rings.values())
    ahead = [ring[0][4].shape[0] - 1 for ring in rings]
    for ring, n in zip(rings, ahead):
        for copy, *_ in ring[:n]:
            copy.start()
    for k in range(max(len(ring) for ring in rings)):
        for ring, n in zip(rings, ahead):
            if k >= len(ring):
                continue
            if k + n < len(ring):
                ring[k + n][0].start()
            copy, dst, r0, rc, stage, slot = ring[k]
            copy.wait()
            dst[r0:r0 + rc, :] = stage[slot].astype(BF16)


def _dot(a, b):
    return jnp.dot(a, b, preferred_element_type=F32)


def _rms(x, g):
    y = x * lax.rsqrt(jnp.mean(x * x, axis=-1, keepdims=True) + _EPS)
    return y * g


def _layernorm(x, g, b):
    mu = jnp.mean(x, axis=-1, keepdims=True)
    d = x - mu
    var = jnp.mean(d * d, axis=-1, keepdims=True)
    return d * lax.rsqrt(var + _EPS) * g + b


def _silu(x):
    return x * jax.nn.sigmoid(x)


_LOG2E = 1.4426950408889634


def _softmax2_rows(s2):
    m = jnp.max(s2, axis=-1, keepdims=True)
    e = jnp.exp2(s2 - m)
    return e * (1.0 / jnp.sum(e, axis=-1, keepdims=True))


_DONE = object()


def _run_staggered(seqs):
    live = []
    pending = list(seqs)
    while live or pending:
        if pending:
            live.append(pending.pop(0))
        live = [s for s in live if next(s, _DONE) is not _DONE]


def _kv_out_copies(kbuf, vbuf, k_hbm, v_hbm, sem, layer, blk, slot):
    _, heads, nbk = kbuf.shape[:3]
    rows = pl.ds(blk * nbk, nbk)
    cps = []
    for hd in range(heads):
        cps.append(pltpu.make_async_copy(kbuf.at[slot, hd], k_hbm.at[layer, rows, :, hd, :], sem.at[slot, 0, hd]))
        cps.append(pltpu.make_async_copy(vbuf.at[slot, hd], v_hbm.at[layer, rows, :, hd, :], sem.at[slot, 1, hd]))
    return cps


def _kv_kernel(m_ref, wk_hbm, wv_hbm, k_hbm, v_hbm, kbuf, vbuf, sem, wk_ref, wv_ref, stage, wsem):
    nj = pl.num_programs(1)
    layer = pl.program_id(0)
    blk = pl.program_id(1)
    step = layer * nj + blk
    last = pl.num_programs(0) * nj - 1
    slot = lax.rem(step, 2)
    _, heads, nbk, mem, dh = kbuf.shape
    copies = functools.partial(_kv_out_copies, kbuf, vbuf, k_hbm, v_hbm, sem)

    @pl.when(blk == 0)
    def _():
        _fetch_bf16((wk_hbm, layer, wk_ref, stage, wsem), (wv_hbm, layer, wv_ref, stage, wsem))

    @pl.when(step >= 2)
    def _():
        prev = step - 2
        for cp in copies(lax.div(prev, nj), lax.rem(prev, nj), slot):
            cp.wait()

    m = m_ref[...].reshape(nbk * mem, m_ref.shape[-1]).astype(BF16)
    k = _dot(m, wk_ref[...])
    v = _dot(m, wv_ref[...])
    for hd in range(heads):
        for b in range(nbk):
            kbuf[slot, hd, b] = k[b * mem:(b + 1) * mem, hd * dh:(hd + 1) * dh]
            vbuf[slot, hd, b] = v[b * mem:(b + 1) * mem, hd * dh:(hd + 1) * dh]
    for cp in copies(layer, blk, slot):
        cp.start()

    @pl.when(step == last)
    def _():
        @pl.when(step >= 1)
        def _():
            prev = step - 1
            for cp in copies(lax.div(prev, nj), lax.rem(prev, nj), 1 - slot):
                cp.wait()
        for cp in copies(layer, blk, slot):
            cp.wait()


def _kv_proj(mem3, wk, wv, heads):
    nb, mem, d = mem3.shape
    depth = wk.shape[0]
    dh = d // heads
    nbk = _KV_BATCHES
    assert nb % nbk == 0 and mem % _SUBLANE == 0
    any_spec = pl.BlockSpec(memory_space=pl.ANY)
    buf = pltpu.VMEM((2, heads, nbk, mem, dh), F32)
    wk_scr, stage, wsem = _weight_scratch(d, d)
    return pl.pallas_call(
        _kv_kernel,
        grid=(depth, nb // nbk),
        in_specs=[pl.BlockSpec((nbk, mem, d), lambda l, j: (j, 0, 0)), any_spec, any_spec],
        out_specs=[any_spec, any_spec],
        out_shape=[jax.ShapeDtypeStruct((depth, nb, mem, heads, dh), F32)] * 2,
        scratch_shapes=[buf, buf, pltpu.SemaphoreType.DMA((2, 2, heads)), wk_scr, wk_scr, stage, wsem],
        compiler_params=_params("arbitrary", "arbitrary"),
        name="kv_proj",
    )(mem3, wk, wv)


def _sattn_copies(k_hbm, v_hbm, cbuf, sem, layer, row0, slot):
    _, _, heads, hb = cbuf.shape[:4]
    rows = pl.ds(row0, hb)
    cps = []
    for j, src in enumerate((k_hbm, v_hbm)):
        for hd in range(heads):
            cps.append(pltpu.make_async_copy(src.at[layer, rows, :, hd, :], cbuf.at[slot, j, hd], sem.at[slot, j, hd]))
    return cps


def _sample_attn_rows(qs_ref, o_scr, cbuf, slot, row0):
    _, _, heads, hb, _, dh = cbuf.shape
    for j in range(hb):
        outs = []
        for hd in range(heads):
            q_row = qs_ref[pl.ds(row0 + j, 1), hd * dh:(hd + 1) * dh]
            s = jnp.sum(cbuf[slot, 0, hd, j] * q_row, axis=1, keepdims=True)
            e = jnp.exp2(s - jnp.max(s, axis=0, keepdims=True))
            den = jnp.sum(e, axis=0, keepdims=True)
            num = jnp.sum(e * cbuf[slot, 1, hd, j], axis=0, keepdims=True)
            outs.append(num * (1.0 / den))
            yield
        o_scr[pl.ds(row0 + j, 1), :] = jnp.concatenate(outs, axis=1)


def _ffn_kernel(xp_ref, xs_ref, qs_ref, g_ref, gf_ref, ck_hbm, cv_hbm, wg_hbm, wu_hbm, wd_hbm, wo_hbm,
                op_ref, os_ref, wg_ref, wu_ref, up_stage, up_sem, wd_ref, dn_stage, dn_sem, wo_ref,
                o_scr, cbuf, csem, *, layer, chunks, final_norm):
    step = pl.program_id(0)
    n_tiles = pl.num_programs(0) - 1
    slots, hb = cbuf.shape[0], cbuf.shape[3]
    per_step = slots * hb
    ccopies = functools.partial(_sattn_copies, ck_hbm, cv_hbm, cbuf, csem, layer)

    @pl.when(step == 0)
    def _():
        for sl in range(slots):
            for cp in ccopies(sl * hb, sl):
                cp.start()
        _fetch_bf16((wg_hbm, layer, wg_ref, up_stage, up_sem), (wu_hbm, layer, wu_ref, up_stage, up_sem),
                    (wd_hbm, layer, wd_ref, dn_stage, dn_sem), (wo_hbm, layer, wo_ref, dn_stage, dn_sem))

    def swiglu(x, o_ref):
        h = _rms(x, g_ref[...]).astype(BF16)
        yield
        acc = x
        lo = 0
        for width in chunks:
            gate = _dot(h, wg_ref[:, lo:lo + width])
            up = _dot(h, wu_ref[:, lo:lo + width])
            yield
            act = (_silu(gate) * up).astype(BF16)
            yield
            acc = acc + _dot(act, wd_ref[lo:lo + width, :])
            yield
            lo += width
        if final_norm:
            acc = _rms(acc, gf_ref[...])
        o_ref[...] = acc

    def sample_rows():
        nxt = jnp.minimum(step + 1, n_tiles - 1) * per_step
        for sl in range(slots):
            row0 = step * per_step + sl * hb
            for cp in ccopies(row0, sl):
                cp.wait()
            yield from _sample_attn_rows(qs_ref, o_scr, cbuf, sl, row0)
            for cp in ccopies(nxt + sl * hb, sl):
                cp.start()
            yield

    @pl.when(step < n_tiles)
    def _():
        _run_staggered([swiglu(xp_ref[...], op_ref), sample_rows()])

    @pl.when(step == n_tiles)
    def _():
        for sl in range(slots):
            for cp in ccopies((n_tiles - 1) * per_step + sl * hb, sl):
                cp.wait()
        x = xs_ref[...] + _dot(o_scr[...].astype(BF16), wo_ref[...])
        _run_staggered([swiglu(x, os_ref)])


def _ffn(xp2d, xs2d, qs2d, ck5, cv5, wo, g, wg, wu, wd, gf, layer, tm, final_norm):
    rows, d = xp2d.shape
    b = xs2d.shape[0]
    ff = wg.shape[-1]
    _, _, mem_len, heads, dh = ck5.shape
    n_tiles = rows // tm
    slots = 2
    assert sum(_FF_CHUNKS) == ff and rows % tm == 0 and b % (n_tiles * slots) == 0
    hb = b // (n_tiles * slots)
    kern = functools.partial(_ffn_kernel, layer=layer, chunks=_FF_CHUNKS, final_norm=final_norm)
    row_spec = pl.BlockSpec((tm, d), lambda i: (jnp.minimum(i, n_tiles - 1), 0))
    any_spec = pl.BlockSpec(memory_space=pl.ANY)
    ops, specs = zip(_sel(g, layer), _sel(gf, 0))
    wg_scr, up_stage, up_sem = _weight_scratch(d, ff)
    wd_scr, dn_stage, dn_sem = _weight_scratch(ff, d)
    assert dn_stage.shape[1:] == _weight_scratch(d, d)[1].shape[1:]
    return pl.pallas_call(
        kern,
        grid=(n_tiles + 1,),
        in_specs=[row_spec, _const_spec((b, d)), _const_spec((b, d)), *specs] + [any_spec] * 6,
        out_specs=[row_spec, _whole_spec((b, d))],
        out_shape=[jax.ShapeDtypeStruct((rows, d), F32), jax.ShapeDtypeStruct((b, d), F32)],
        scratch_shapes=[wg_scr, wg_scr, up_stage, up_sem, wd_scr, dn_stage, dn_sem, pltpu.VMEM((d, d), BF16),
                        pltpu.VMEM((b, d), F32), pltpu.VMEM((slots, 2, heads, hb, mem_len, dh), F32),
                        pltpu.SemaphoreType.DMA((slots, 2, heads))],
        compiler_params=_params("arbitrary"),
        name="ffn",
    )(xp2d, xs2d, qs2d, *ops, ck5, cv5, wg, wu, wd, wo)


def _pattn_kv_copies(k_hbm, v_hbm, kvbuf, sem, layer, n, slot):
    heads = kvbuf.shape[2]
    cps = []
    for j, src in enumerate((k_hbm, v_hbm)):
        for hd in range(heads):
            cps.append(pltpu.make_async_copy(src.at[layer, n, :, hd, :], kvbuf.at[slot, j, hd], sem.at[slot, j, hd]))
    return cps


def _attn_kernel(xp_ref, xs_ref, g_ref, pk_hbm, pv_hbm, wq_hbm, wo_hbm, op_ref, qs_ref,
                 kvbuf, kv16, psem, wq_ref, wo_ref, stage, wsem, *, layer, parts):
    n = pl.program_id(0)
    i = pl.program_id(1)
    heads, _, dh = kv16.shape[1:]
    tm = xp_ref.shape[0]
    rows = tm // parts
    pslot = lax.rem(n, 2)
    pcopies = functools.partial(_pattn_kv_copies, pk_hbm, pv_hbm, kvbuf, psem, layer)

    @pl.when(jnp.logical_and(n == 0, i == 0))
    def _():
        for cp in pcopies(0, 0):
            cp.start()
        _fetch_bf16((wq_hbm, layer, wq_ref, stage, wsem), (wo_hbm, layer, wo_ref, stage, wsem))
        h = _rms(xs_ref[...], g_ref[...]).astype(BF16)
        qs_ref[...] = _dot(h, wq_ref[...]) * (dh ** -0.5 * _LOG2E)

    @pl.when(i == 0)
    def _():
        @pl.when(n + 1 < pl.num_programs(0))
        def _():
            for cp in pcopies(n + 1, 1 - pslot):
                cp.start()

        for cp in pcopies(n, pslot):
            cp.wait()
        kv16[...] = kvbuf[pslot].astype(BF16)

    def attend(r0):
        x = xp_ref[r0:r0 + rows, :]
        h = _rms(x, g_ref[...]).astype(BF16)
        yield
        q = _dot(h, wq_ref[...]).astype(BF16)
        yield
        outs = []
        for hd in range(heads):
            s = lax.dot_general(q[:, hd * dh:(hd + 1) * dh], kv16[0, hd], (((1,), (1,)), ((), ())),
                                preferred_element_type=F32)
            yield
            a = _softmax2_rows(s * (dh ** -0.5 * _LOG2E)).astype(BF16)
            yield
            outs.append(_dot(a, kv16[1, hd]))
            yield
        o = jnp.concatenate(outs, axis=1).astype(BF16)
        op_ref[r0:r0 + rows, :] = x + _dot(o, wo_ref[...])
        yield

    _run_staggered([attend(p * rows) for p in range(parts)])


def _attn(xp, xs, g, wq, pk5, pv5, wo, layer, tm):
    nb, t, d = xp.shape
    b = xs.shape[0]
    _, _, mem_len, heads, dh = pk5.shape
    assert t % tm == 0 and tm % (_ATTN_PARTS * _SUBLANE) == 0
    x_spec = pl.BlockSpec((None, tm, d), lambda n, i: (n, i, 0))
    any_spec = pl.BlockSpec(memory_space=pl.ANY)
    g, g_spec = _sel(g, layer)
    wq_scr, stage, wsem = _weight_scratch(d, d)
    return pl.pallas_call(
        functools.partial(_attn_kernel, layer=layer, parts=_ATTN_PARTS),
        grid=(nb, t // tm),
        in_specs=[x_spec, _const_spec((b, d)), g_spec] + [any_spec] * 4,
        out_specs=[x_spec, _whole_spec((b, d))],
        out_shape=[jax.ShapeDtypeStruct((nb, t, d), F32), jax.ShapeDtypeStruct((b, d), F32)],
        scratch_shapes=[pltpu.VMEM((2, 2, heads, mem_len, dh), F32), pltpu.VMEM((2, heads, mem_len, dh), BF16),
                        pltpu.SemaphoreType.DMA((2, 2, heads)), wq_scr, wq_scr, stage, wsem],
        compiler_params=_params("arbitrary", "arbitrary"),
        name="attn",
    )(xp, xs, g, pk5, pv5, wq, wo)


def _peven_kernel(x_ref, g_ref, cw_ref, cb_ref, lg_ref, lb_ref, ps_ref, win_hbm, wout_hbm, pw_hbm,
                  xo_ref, sa_ref, sb_ref, za_ext, zb_ext, ca_buf,
                  win_ref, win_stage, win_sem, wout_ref, wout_stage, wout_sem, pw_ref, pw_stage, pw_sem,
                  *, idx, tm, d_a, ha, hb):
    t = pl.program_id(1)
    ka = cw_ref.shape[0]
    nca = za_ext.shape[0]
    ncb = zb_ext.shape[0]
    pb = sb_ref.shape[0]

    @pl.when(jnp.logical_and(pl.program_id(0) == 0, t == 0))
    def _():
        _fetch_bf16((win_hbm, idx, win_ref, win_stage, win_sem), (wout_hbm, idx, wout_ref, wout_stage, wout_sem),
                    (pw_hbm, idx, pw_ref, pw_stage, pw_sem))

    @pl.when(t == 0)
    def _():
        za_ext[:, 0:ha, :] = jnp.zeros((nca, ha, _LANE), F32)
        zb_ext[:, 0:hb, :] = jnp.zeros((ncb, hb, _LANE), F32)

    x = x_ref[...]
    h = _rms(x, g_ref[...]).astype(BF16)
    p = _dot(h, win_ref[...])
    za = p[:, :d_a] * jax.nn.sigmoid(p[:, d_a:2 * d_a])
    zb = p[:, 2 * d_a:]
    for c in range(nca):
        za_ext[c, ha:ha + tm, :] = za[:, c * _LANE:(c + 1) * _LANE]
    for c in range(ncb):
        zb_ext[c, hb:hb + tm, :] = zb[:, c * _LANE:(c + 1) * _LANE]

    base = ha - (ka - 1)
    for c in range(nca):
        ls = slice(c * _LANE, (c + 1) * _LANE)
        for r in range(tm // _CONV_ROWS):
            r0 = r * _CONV_ROWS
            acc = jnp.broadcast_to(cb_ref[:, ls], (_CONV_ROWS, _LANE))
            for k in range(ka):
                acc = acc + cw_ref[k:k + 1, ls] * za_ext[c, base + r0 + k:base + r0 + k + _CONV_ROWS, :]
            ca_buf[r0:r0 + _CONV_ROWS, ls] = acc
    ya = _silu(_layernorm(ca_buf[...], lg_ref[...], lb_ref[...]))

    pos = t * tm + lax.broadcasted_iota(jnp.int32, (tm, 1), 0)
    parts = []
    for gi, w in enumerate(_POOL_WINDOWS):
        zg = zb[:, gi * _LANE:(gi + 1) * _LANE]
        s = zg
        for j in range(1, w):
            s = s + zb_ext[gi, hb - j:hb - j + tm, :]
        inv_cnt = 1.0 / jnp.minimum(pos + 1, w).astype(F32)
        pooled = (s * inv_cnt - zg).astype(BF16)
        parts.append(_dot(pooled, pw_ref[gi * _LANE:(gi + 1) * _LANE, :]))
    yb = jnp.concatenate(parts, axis=1) * ps_ref[...]

    y = jnp.concatenate([ya, yb], axis=1).astype(BF16)
    xo_ref[...] = x + _dot(y, wout_ref[...])

    @pl.when(t == pl.num_programs(1) - 1)
    def _():
        for c in range(nca):
            sa_ref[:, c * _LANE:(c + 1) * _LANE] = za_ext[c, ha + tm - (ka - 1):ha + tm, :]
        for c in range(ncb):
            sb_ref[:, c * _LANE:(c + 1) * _LANE] = zb_ext[c, hb + tm - pb:hb + tm, :]

    za_ext[:, 0:ha, :] = za_ext[:, tm:tm + ha, :]
    zb_ext[:, 0:hb, :] = zb_ext[:, tm:tm + hb, :]


def _prompt_even(x, params, big, idx, tm):
    nb, t, d = x.shape
    ops, specs = zip(*params)
    win, wout, pw = big
    ka, d_a = ops[1].shape[-2:]
    d_b = ops[5].shape[-1]
    pb = max(_POOL_WINDOWS) - 1
    ha = _round_up(ka - 1, _SUBLANE)
    hb = _round_up(pb, _SUBLANE)
    assert t % tm == 0 and tm % _CONV_ROWS == 0 and tm >= max(ha, hb)
    assert d_a % _LANE == 0 and d_b == _LANE * len(_POOL_WINDOWS)
    x_spec = pl.BlockSpec((None, tm, d), lambda n, i: (n, i, 0))
    any_spec = pl.BlockSpec(memory_space=pl.ANY)
    kern = functools.partial(_peven_kernel, idx=idx, tm=tm, d_a=d_a, ha=ha, hb=hb)
    return pl.pallas_call(
        kern,
        grid=(nb, t // tm),
        in_specs=[x_spec, *specs, any_spec, any_spec, any_spec],
        out_specs=[x_spec,
                   pl.BlockSpec((None, ka - 1, d_a), lambda n, i: (n, 0, 0)),
                   pl.BlockSpec((None, pb, d_b), lambda n, i: (n, 0, 0))],
        out_shape=[jax.ShapeDtypeStruct((nb, t, d), F32),
                   jax.ShapeDtypeStruct((nb, ka - 1, d_a), F32),
                   jax.ShapeDtypeStruct((nb, pb, d_b), F32)],
        scratch_shapes=[pltpu.VMEM((d_a // _LANE, ha + tm, _LANE), F32),
                        pltpu.VMEM((d_b // _LANE, hb + tm, _LANE), F32),
                        pltpu.VMEM((tm, d_a), F32),
                        *_weight_scratch(*win.shape[1:]), *_weight_scratch(*wout.shape[1:]),
                        *_weight_scratch(*pw.shape[1:])],
        compiler_params=_params("arbitrary", "arbitrary"),
        name="prompt_even",
    )(x, *ops, win, wout, pw)


def _seven_kernel(x_ref, g_ref, cw_ref, cb_ref, lg_ref, lb_ref, ps_ref, ha_ref, hb_ref, win_hbm, wout_hbm, pw_hbm,
                  xo_ref, sa_ref, sb_ref,
                  win_ref, win_stage, win_sem, wout_ref, wout_stage, wout_sem, pw_ref, pw_stage, pw_sem, *, idx):
    _fetch_bf16((win_hbm, idx, win_ref, win_stage, win_sem), (wout_hbm, idx, wout_ref, wout_stage, wout_sem),
                (pw_hbm, idx, pw_ref, pw_stage, pw_sem))
    ka, d_a = cw_ref.shape
    pb = hb_ref.shape[0]
    d_b = hb_ref.shape[2]
    dg = d_b // len(_POOL_WINDOWS)
    x = x_ref[...]
    h = _rms(x, g_ref[...]).astype(BF16)
    p = _dot(h, win_ref[...])
    za = p[:, :d_a] * jax.nn.sigmoid(p[:, d_a:2 * d_a])
    zb = p[:, 2 * d_a:]

    acc = cb_ref[...] + cw_ref[ka - 1:ka, :] * za
    for k in range(ka - 1):
        acc = acc + cw_ref[k:k + 1, :] * ha_ref[k]
    ya = _silu(_layernorm(acc, lg_ref[...], lb_ref[...]))

    parts = []
    for gi, w in enumerate(_POOL_WINDOWS):
        ls = slice(gi * dg, (gi + 1) * dg)
        s = zb[:, ls]
        for j in range(1, w):
            s = s + hb_ref[pb - j, :, ls]
        inv_cnt = 1.0 / float(min(_PAST_LEN + 1, w))
        pooled = (s * inv_cnt - zb[:, ls]).astype(BF16)
        parts.append(_dot(pooled, pw_ref[gi * dg:(gi + 1) * dg, :]))
    yb = jnp.concatenate(parts, axis=1) * ps_ref[...]

    y = jnp.concatenate([ya, yb], axis=1).astype(BF16)
    xo_ref[...] = x + _dot(y, wout_ref[...])

    sa_ref[0:ka - 2] = ha_ref[1:ka - 1]
    sa_ref[ka - 2] = za
    sb_ref[0:pb - 1] = hb_ref[1:pb]
    sb_ref[pb - 1] = zb


def _sample_even(x2d, params, big, idx, hist_a, hist_b):
    b, d = x2d.shape
    ops, specs = zip(*params)
    win, wout, pw = big
    any_spec = pl.BlockSpec(memory_space=pl.ANY)
    return pl.pallas_call(
        functools.partial(_seven_kernel, idx=idx),
        grid=(1,),
        in_specs=[_const_spec((b, d)), *specs, _const_spec(hist_a.shape), _const_spec(hist_b.shape),
                  any_spec, any_spec, any_spec],
        out_specs=[_whole_spec((b, d)), _whole_spec(hist_a.shape), _whole_spec(hist_b.shape)],
        out_shape=[jax.ShapeDtypeStruct((b, d), F32),
                   jax.ShapeDtypeStruct(hist_a.shape, F32),
                   jax.ShapeDtypeStruct(hist_b.shape, F32)],
        scratch_shapes=[*_weight_scratch(*win.shape[1:]), *_weight_scratch(*wout.shape[1:]),
                        *_weight_scratch(*pw.shape[1:])],
        compiler_params=_params("arbitrary"),
        name="sample_even",
    )(x2d, *ops, hist_a, hist_b, win, wout, pw)


def _podd_kernel(x_ref, g_ref, lg_ref, lb_ref, ws_ref, bst_ref, cw_ref, win_hbm, wout_hbm,
                 xo_ref, sv_ref, sd_ref, g_ext, win_ref, win_stage, win_sem, wout_ref, wout_stage, wout_sem,
                 *, idx, tm, d_c, hd_rows):
    t = pl.program_id(1)
    kd, d_d = cw_ref.shape
    heads = ws_ref.shape[0]
    dh = d_c // heads
    ncd = g_ext.shape[0]

    @pl.when(jnp.logical_and(pl.program_id(0) == 0, t == 0))
    def _():
        _fetch_bf16((win_hbm, idx, win_ref, win_stage, win_sem), (wout_hbm, idx, wout_ref, wout_stage, wout_sem))

    @pl.when(t == 0)
    def _():
        g_ext[:, 0:hd_rows, :] = jnp.zeros((ncd, hd_rows, _LANE), F32)

    x = x_ref[...]
    h = _rms(x, g_ref[...]).astype(BF16)
    p = _dot(h, win_ref[...])
    u = p[:, :d_c]
    v = _layernorm(p[:, d_c:2 * d_c], lg_ref[...], lb_ref[...])
    o = 2 * d_c
    gb = p[:, o:o + d_d]
    gated = p[:, o + d_d:o + 2 * d_d] * p[:, o + 2 * d_d:]

    ri = lax.broadcasted_iota(jnp.int32, (_CHUNK, _CHUNK), 0)
    ci = lax.broadcasted_iota(jnp.int32, (_CHUNK, _CHUNK), 1)
    wm = [jnp.where(ci <= ri, ws_ref[hh], 0.0).astype(BF16) for hh in range(heads)]
    vb = v.astype(BF16)
    rows = []
    for c in range(tm // _CHUNK):
        rs = slice(c * _CHUNK, (c + 1) * _CHUNK)
        parts = [_dot(wm[hh], vb[rs, hh * dh:(hh + 1) * dh]) + bst_ref[:, hh:hh + 1] for hh in range(heads)]
        rows.append(jnp.concatenate(parts, axis=1))
    yc = u * jnp.concatenate(rows, axis=0)

    base = hd_rows - (kd - 1)
    cols = []
    for c in range(ncd):
        ls = slice(c * _LANE, (c + 1) * _LANE)
        g_ext[c, hd_rows:hd_rows + tm, :] = gated[:, ls]
        cd = cw_ref[0:1, ls] * g_ext[c, base:base + tm, :]
        for k in range(1, kd):
            cd = cd + cw_ref[k:k + 1, ls] * g_ext[c, base + k:base + k + tm, :]
        cols.append(cd)
    yd = gb * jnp.concatenate(cols, axis=1)

    y = jnp.concatenate([yc, yd], axis=1).astype(BF16)
    xo_ref[...] = x + _dot(y, wout_ref[...])

    @pl.when(t == pl.num_programs(1) - 1)
    def _():
        n_open = sv_ref.shape[0]
        sv_ref[...] = v[tm - n_open:, :]
        for c in range(ncd):
            sd_ref[:, c * _LANE:(c + 1) * _LANE] = g_ext[c, hd_rows + tm - (kd - 1):hd_rows + tm, :]

    g_ext[:, 0:hd_rows, :] = g_ext[:, tm:tm + hd_rows, :]


def _prompt_odd(x, params, big, idx, tm):
    nb, t, d = x.shape
    ops, specs = zip(*params)
    win, wout = big
    d_c = ops[1].shape[-1]
    kd, d_d = ops[5].shape[-2:]
    n_open = t - ((t - 1) // _CHUNK) * _CHUNK
    hd_rows = _round_up(kd - 1, _SUBLANE)
    assert t % tm == 0 and tm % _CHUNK == 0 and n_open % _SUBLANE == 0 and n_open <= tm
    x_spec = pl.BlockSpec((None, tm, d), lambda n, i: (n, i, 0))
    any_spec = pl.BlockSpec(memory_space=pl.ANY)
    kern = functools.partial(_podd_kernel, idx=idx, tm=tm, d_c=d_c, hd_rows=hd_rows)
    return pl.pallas_call(
        kern,
        grid=(nb, t // tm),
        in_specs=[x_spec, *specs, any_spec, any_spec],
        out_specs=[x_spec,
                   pl.BlockSpec((None, n_open, d_c), lambda n, i: (n, 0, 0)),
                   pl.BlockSpec((None, kd - 1, d_d), lambda n, i: (n, 0, 0))],
        out_shape=[jax.ShapeDtypeStruct((nb, t, d), F32),
                   jax.ShapeDtypeStruct((nb, n_open, d_c), F32),
                   jax.ShapeDtypeStruct((nb, kd - 1, d_d), F32)],
        scratch_shapes=[pltpu.VMEM((d_d // _LANE, hd_rows + tm, _LANE), F32),
                        *_weight_scratch(*win.shape[1:]), *_weight_scratch(*wout.shape[1:])],
        compiler_params=_params("arbitrary", "arbitrary"),
        name="prompt_odd",
    )(x, *ops, win, wout)


def _sodd_kernel(x_ref, g_ref, lg_ref, lb_ref, ws0_ref, bs0_ref, cw_ref, hd_ref, win_hbm, wout_hbm,
                 xo_ref, sv_ref, sd_ref, win_ref, win_stage, win_sem, wout_ref, wout_stage, wout_sem, *, idx):
    _fetch_bf16((win_hbm, idx, win_ref, win_stage, win_sem), (wout_hbm, idx, wout_ref, wout_stage, wout_sem))
    kd, d_d = cw_ref.shape
    d_c = lg_ref.shape[1]
    x = x_ref[...]
    h = _rms(x, g_ref[...]).astype(BF16)
    p = _dot(h, win_ref[...])
    u = p[:, :d_c]
    v = _layernorm(p[:, d_c:2 * d_c], lg_ref[...], lb_ref[...])
    o = 2 * d_c
    gb = p[:, o:o + d_d]
    gated = p[:, o + d_d:o + 2 * d_d] * p[:, o + 2 * d_d:]

    yc = u * (ws0_ref[...] * v + bs0_ref[...])

    cd = cw_ref[kd - 1:kd, :] * gated
    for k in range(kd - 1):
        cd = cd + cw_ref[k:k + 1, :] * hd_ref[:, k, :]
    yd = gb * cd

    y = jnp.concatenate([yc, yd], axis=1).astype(BF16)
    xo_ref[...] = x + _dot(y, wout_ref[...])

    sv_ref[...] = v
    for k in range(kd - 2):
        sd_ref[:, k, :] = hd_ref[:, k + 1, :]
    sd_ref[:, kd - 2, :] = gated


def _sample_odd(x2d, params, big, idx, ws0, bs0, hist_d):
    b, d = x2d.shape
    ops, specs = zip(*params)
    win, wout = big
    d_c = ops[1].shape[-1]
    ops = ops[:3] + (ws0, bs0) + ops[5:]
    specs = specs[:3] + (_const_spec(ws0.shape), _const_spec(bs0.shape)) + specs[5:]
    any_spec = pl.BlockSpec(memory_space=pl.ANY)
    return pl.pallas_call(
        functools.partial(_sodd_kernel, idx=idx),
        grid=(1,),
        in_specs=[_const_spec((b, d)), *specs, _const_spec(hist_d.shape), any_spec, any_spec],
        out_specs=[_whole_spec((b, d)), _whole_spec((b, d_c)), _whole_spec(hist_d.shape)],
        out_shape=[jax.ShapeDtypeStruct((b, d), F32),
                   jax.ShapeDtypeStruct((b, d_c), F32),
                   jax.ShapeDtypeStruct(hist_d.shape, F32)],
        scratch_shapes=[*_weight_scratch(*win.shape[1:]), *_weight_scratch(*wout.shape[1:])],
        compiler_params=_params("arbitrary"),
        name="sample_odd",
    )(x2d, *ops, hist_d, win, wout)


def kernel(x_prompt, x_sample, mem_prompt, state_convA, state_poolB, state_convD, cache_mem_k, cache_mem_v, norm_mix, norm_x, norm_ffn, norm_final, w_in_even, conv_a_w, conv_a_b, ln_a_g, ln_a_b, pool_b_w, pool_b_scale, w_out_even, w_in_odd, ln_c_g, ln_c_b, ws_c, bs_c, conv_d_w, w_out_odd, wq_x, wk_x, wv_x, wo_x, w_gate, w_up, w_down):
    nb, seq, d = x_prompt.shape
    db, dec_seq, _ = x_sample.shape
    assert dec_seq == 1
    depth = wq_x.shape[0]
    heads = cache_mem_k.shape[3]

    def unstack(parts):
        return parts[0][None] if len(parts) == 1 else jnp.stack(parts)

    g_fin = norm_final[None]
    pool_w2d = pool_b_w.reshape(pool_b_w.shape[0], -1, pool_b_w.shape[-1])

    p_k, p_v = _kv_proj(mem_prompt, wk_x, wv_x, heads)

    xp = x_prompt
    xs = x_sample.reshape(db, d)
    p_a, p_b, p_c, p_d = [], [], [], []
    s_a, s_b, s_c, s_d = [], [], [], []
    for layer in range(depth):
        i = layer // 2
        if layer % 2 == 0:
            params = [_sel(norm_mix, layer), _sel(conv_a_w, i), _sel(conv_a_b, i), _sel(ln_a_g, i), _sel(ln_a_b, i),
                      _sel(pool_b_scale, i)]
            big = (w_in_even, w_out_even, pool_w2d)
            xp, a, b = _prompt_even(xp, params, big, i, tm=_PROMPT_TILE)
            p_a.append(a)
            p_b.append(b)
            xs, a, b = _sample_even(xs, params, big, i, state_convA[i].transpose(1, 0, 2),
                                    state_poolB[i].transpose(1, 0, 2))
            s_a.append(a.transpose(1, 0, 2))
            s_b.append(b.transpose(1, 0, 2))
        else:
            params = [_sel(norm_mix, layer), _sel(ln_c_g, i), _sel(ln_c_b, i), _sel(ws_c, i),
                      _sel(bs_c.transpose(0, 2, 1), i), _sel(conv_d_w, i)]
            big = (w_in_odd, w_out_odd)
            xp, c, dd = _prompt_odd(xp, params, big, i, tm=_PROMPT_TILE)
            dc = c.shape[-1]
            p_c.append(c.reshape(nb, -1, _C_HEADS, dc // _C_HEADS))
            p_d.append(dd)
            ws0 = jnp.repeat(ws_c[i, :, 0, 0], dc // _C_HEADS)[None]
            bs0 = jnp.repeat(bs_c[i, :, 0], dc // _C_HEADS)[None]
            xs, c, dd = _sample_odd(xs, params, big, i, ws0, bs0, state_convD[i])
            s_c.append(c.reshape(db, 1, _C_HEADS, dc // _C_HEADS))
            s_d.append(dd)

        xp, qs = _attn(xp, xs, norm_x, wq_x, p_k, p_v, wo_x, layer, tm=_ATTN_TILE)

        xp2d, xs = _ffn(xp.reshape(nb * seq, d), xs, qs, cache_mem_k, cache_mem_v, wo_x, norm_ffn, w_gate, w_up,
                        w_down, g_fin, layer, tm=_FFN_TILE, final_norm=layer == depth - 1)
        xp = xp2d.reshape(nb, seq, d)

    return (xp, xs.reshape(db, 1, d), unstack(p_a), unstack(p_b), unstack(p_c), unstack(p_d), p_k, p_v,
            unstack(s_a), unstack(s_b), unstack(s_c), unstack(s_d))
```

```python
import functools

import jax
import jax.numpy as jnp
from jax import lax
from jax.experimental import pallas as pl
from jax.experimental.pallas import tpu as pltpu

F32 = jnp.float32
BF16 = jnp.bfloat16

_EPS = 1e-6
_X_HEADS = 4
_C_HEADS = 4
_CHUNK = 128
_POOL_WINDOWS = (2, 4, 8, 16)
_PAST_LEN = 16384

_SUBLANE = 8
_LANE = 128
_VMEM_LIMIT = 56 * 1024 * 1024
_STAGE_BYTES = 1024 * 1024
_STAGE_SLOTS = 4

_PROMPT_TILE = 1024
_ATTN_TILE = 2048
_ATTN_PARTS = 4
_FFN_TILE = 512
_KV_BATCHES = 4
_FF_CHUNKS = (768, 768, 768, 512)
_CONV_ROWS = 16


def _round_up(n, m):
    return -(-n // m) * m


def _params(*sem):
    return pltpu.CompilerParams(dimension_semantics=sem, vmem_limit_bytes=_VMEM_LIMIT)


def _const_spec(shape):
    nd = len(shape)
    return pl.BlockSpec(shape, lambda *_: (0,) * nd, pipeline_mode=pl.Buffered(1))


def _whole_spec(shape):
    nd = len(shape)
    return pl.BlockSpec(shape, lambda *_: (0,) * nd)


def _sel(stacked, idx):
    if stacked.ndim == 2:
        stacked = stacked[:, None, :]
    tail = stacked.shape[1:]
    spec = pl.BlockSpec((None,) + tail, lambda *_: (idx,) + (0,) * len(tail), pipeline_mode=pl.Buffered(1))
    return stacked, spec


def _chunk_rows(rows, cols):
    cands = [r for r in range(2 * _SUBLANE, rows + 1, 2 * _SUBLANE)
             if rows % r == 0 and r * cols * 4 <= _STAGE_BYTES]
    return max(cands)


def _weight_scratch(rows, cols):
    return [pltpu.VMEM((rows, cols), BF16), pltpu.VMEM((_STAGE_SLOTS, _chunk_rows(rows, cols), cols), F32),
            pltpu.SemaphoreType.DMA((_STAGE_SLOTS,))]


def _fetch_bf16(*jobs):
    rings = {}
    for w_hbm, idx, dst, stage, sem in jobs:
        slots, rc = stage.shape[:2]
        ring = rings.setdefault(id(stage), [])
        for r0 in range(0, dst.shape[0], rc):
            slot = len(ring) % slots
            copy = pltpu.make_async_copy(w_hbm.at[idx, pl.ds(r0, rc), :], stage.at[slot], sem.at[slot])
            ring.append((copy, dst, r0, rc, stage, slot))
    rings = list(rings.values())
    ahead = [ring[0][4].shape[0] - 1 for ring in rings]
    for ring, n in zip(rings, ahead):
        for copy, *_ in ring[:n]:
            copy.start()
    for k in range(max(len(ring) for ring in rings)):
        for ring, n in zip(rings, ahead):
            if k >= len(ring):
                continue
            if k + n < len(ring):
                ring[k + n][0].start()
            copy, dst, r0, rc, stage, slot = ring[k]
            copy.wait()
            dst[r0:r0 + rc, :] = stage[slot].astype(BF16)


def _dot(a, b):
    return jnp.dot(a, b, preferred_element_type=F32)


def _rms(x, g):
    y = x * lax.rsqrt(jnp.mean(x * x, axis=-1, keepdims=True) + _EPS)
    return y * g


def _layernorm(x, g, b):
    mu = jnp.mean(x, axis=-1, keepdims=True)
    d = x - mu
    var = jnp.mean(d * d, axis=-1, keepdims=True)
    return d * lax.rsqrt(var + _EPS) * g + b


def _silu(x):
    return x * jax.nn.sigmoid(x)


_LOG2E = 1.4426950408889634


def _softmax2_rows(s2):
    m = jnp.max(s2, axis=-1, keepdims=True)
    e = jnp.exp2(s2 - m)
    return e * (1.0 / jnp.sum(e, axis=-1, keepdims=True))


_DONE = object()


def _run_staggered(seqs):
    live = []
    pending = list(seqs)
    while live or pending:
        if pending:
            live.append(pending.pop(0))
        live = [s for s in live if next(s, _DONE) is not _DONE]


def _kv_out_copies(kbuf, vbuf, k_hbm, v_hbm, sem, layer, blk, slot):
    _, heads, nbk = kbuf.shape[:3]
    rows = pl.ds(blk * nbk, nbk)
    cps = []
    for hd in range(heads):
        cps.append(pltpu.make_async_copy(kbuf.at[slot, hd], k_hbm.at[layer, rows, :, hd, :], sem.at[slot, 0, hd]))
        cps.append(pltpu.make_async_copy(vbuf.at[slot, hd], v_hbm.at[layer, rows, :, hd, :], sem.at[slot, 1, hd]))
    return cps


def _kv_kernel(m_ref, wk_hbm, wv_hbm, k_hbm, v_hbm, kbuf, vbuf, sem, wk_ref, wv_ref, stage, wsem):
    nj = pl.num_programs(1)
    layer = pl.program_id(0)
    blk = pl.program_id(1)
    step = layer * nj + blk
    last = pl.num_programs(0) * nj - 1
    slot = lax.rem(step, 2)
    _, heads, nbk, mem, dh = kbuf.shape
    copies = functools.partial(_kv_out_copies, kbuf, vbuf, k_hbm, v_hbm, sem)

    @pl.when(blk == 0)
    def _():
        _fetch_bf16((wk_hbm, layer, wk_ref, stage, wsem), (wv_hbm, layer, wv_ref, stage, wsem))

    @pl.when(step >= 2)
    def _():
        prev = step - 2
        for cp in copies(lax.div(prev, nj), lax.rem(prev, nj), slot):
            cp.wait()

    m = m_ref[...].reshape(nbk * mem, m_ref.shape[-1]).astype(BF16)
    k = _dot(m, wk_ref[...])
    v = _dot(m, wv_ref[...])
    for hd in range(heads):
        for b in range(nbk):
            kbuf[slot, hd, b] = k[b * mem:(b + 1) * mem, hd * dh:(hd + 1) * dh]
            vbuf[slot, hd, b] = v[b * mem:(b + 1) * mem, hd * dh:(hd + 1) * dh]
    for cp in copies(layer, blk, slot):
        cp.start()

    @pl.when(step == last)
    def _():
        @pl.when(step >= 1)
        def _():
            prev = step - 1
            for cp in copies(lax.div(prev, nj), lax.rem(prev, nj), 1 - slot):
                cp.wait()
        for cp in copies(layer, blk, slot):
            cp.wait()


def _kv_proj(mem3, wk, wv, heads):
    nb, mem, d = mem3.shape
    depth = wk.shape[0]
    dh = d // heads
    nbk = _KV_BATCHES
    assert nb % nbk == 0 and mem % _SUBLANE == 0
    any_spec = pl.BlockSpec(memory_space=pl.ANY)
    buf = pltpu.VMEM((2, heads, nbk, mem, dh), F32)
    wk_scr, stage, wsem = _weight_scratch(d, d)
    return pl.pallas_call(
        _kv_kernel,
        grid=(depth, nb // nbk),
        in_specs=[pl.BlockSpec((nbk, mem, d), lambda l, j: (j, 0, 0)), any_spec, any_spec],
        out_specs=[any_spec, any_spec],
        out_shape=[jax.ShapeDtypeStruct((depth, nb, mem, heads, dh), F32)] * 2,
        scratch_shapes=[buf, buf, pltpu.SemaphoreType.DMA((2, 2, heads)), wk_scr, wk_scr, stage, wsem],
        compiler_params=_params("arbitrary", "arbitrary"),
        name="kv_proj",
    )(mem3, wk, wv)


def _sattn_copies(k_hbm, v_hbm, cbuf, sem, layer, row0, slot):
    _, _, heads, hb = cbuf.shape[:4]
    rows = pl.ds(row0, hb)
    cps = []
    for j, src in enumerate((k_hbm, v_hbm)):
        for hd in range(heads):
            cps.append(pltpu.make_async_copy(src.at[layer, rows, :, hd, :], cbuf.at[slot, j, hd], sem.at[slot, j, hd]))
    return cps


def _sample_attn_rows(qs_ref, o_scr, cbuf, slot, row0):
    _, _, heads, hb, _, dh = cbuf.shape
    for j in range(hb):
        outs = []
        for hd in range(heads):
            q_row = qs_ref[pl.ds(row0 + j, 1), hd * dh:(hd + 1) * dh]
            s = jnp.sum(cbuf[slot, 0, hd, j] * q_row, axis=1, keepdims=True)
            e = jnp.exp2(s - jnp.max(s, axis=0, keepdims=True))
            den = jnp.sum(e, axis=0, keepdims=True)
            num = jnp.sum(e * cbuf[slot, 1, hd, j], axis=0, keepdims=True)
            outs.append(num * (1.0 / den))
            yield
        o_scr[pl.ds(row0 + j, 1), :] = jnp.concatenate(outs, axis=1)


def _ffn_kernel(xp_ref, xs_ref, qs_ref, g_ref, gf_ref, ck_hbm, cv_hbm, wg_hbm, wu_hbm, wd_hbm, wo_hbm,
                op_ref, os_ref, wg_ref, wu_ref, up_stage, up_sem, wd_ref, dn_stage, dn_sem, wo_ref,
                o_scr, cbuf, csem, *, layer, chunks, final_norm):
    step = pl.program_id(0)
    n_tiles = pl.num_programs(0) - 1
    slots, hb = cbuf.shape[0], cbuf.shape[3]
    per_step = slots * hb
    ccopies = functools.partial(_sattn_copies, ck_hbm, cv_hbm, cbuf, csem, layer)

    @pl.when(step == 0)
    def _():
        for sl in range(slots):
            for cp in ccopies(sl * hb, sl):
                cp.start()
        _fetch_bf16((wg_hbm, layer, wg_ref, up_stage, up_sem), (wu_hbm, layer, wu_ref, up_stage, up_sem),
                    (wd_hbm, layer, wd_ref, dn_stage, dn_sem), (wo_hbm, layer, wo_ref, dn_stage, dn_sem))

    def swiglu(x, o_ref):
        h = _rms(x, g_ref[...]).astype(BF16)
        yield
        acc = x
        lo = 0
        for width in chunks:
            gate = _dot(h, wg_ref[:, lo:lo + width])
            up = _dot(h, wu_ref[:, lo:lo + width])
            yield
            act = (_silu(gate) * up).astype(BF16)
            yield
            acc = acc + _dot(act, wd_ref[lo:lo + width, :])
            yield
            lo += width
        if final_norm:
            acc = _rms(acc, gf_ref[...])
        o_ref[...] = acc

    def sample_rows():
        nxt = jnp.minimum(step + 1, n_tiles - 1) * per_step
        for sl in range(slots):
            row0 = step * per_step + sl * hb
            for cp in ccopies(row0, sl):
                cp.wait()
            yield from _sample_attn_rows(qs_ref, o_scr, cbuf, sl, row0)
            for cp in ccopies(nxt + sl * hb, sl):
                cp.start()
            yield

    @pl.when(step < n_tiles)
    def _():
        _run_staggered([swiglu(xp_ref[...], op_ref), sample_rows()])

    @pl.when(step == n_tiles)
    def _():
        for sl in range(slots):
            for cp in ccopies((n_tiles - 1) * per_step + sl * hb, sl):
                cp.wait()
        x = xs_ref[...] + _dot(o_scr[...].astype(BF16), wo_ref[...])
        _run_staggered([swiglu(x, os_ref)])


def _ffn(xp2d, xs2d, qs2d, ck5, cv5, wo, g, wg, wu, wd, gf, layer, tm, final_norm):
    rows, d = xp2d.shape
    b = xs2d.shape[0]
    ff = wg.shape[-1]
    _, _, mem_len, heads, dh = ck5.shape
    n_tiles = rows // tm
    slots = 2
    assert sum(_FF_CHUNKS) == ff and rows % tm == 0 and b % (n_tiles * slots) == 0
    hb = b // (n_tiles * slots)
    kern = functools.partial(_ffn_kernel, layer=layer, chunks=_FF_CHUNKS, final_norm=final_norm)
    row_spec = pl.BlockSpec((tm, d), lambda i: (jnp.minimum(i, n_tiles - 1), 0))
    any_spec = pl.BlockSpec(memory_space=pl.ANY)
    ops, specs = zip(_sel(g, layer), _sel(gf, 0))
    wg_scr, up_stage, up_sem = _weight_scratch(d, ff)
    wd_scr, dn_stage, dn_sem = _weight_scratch(ff, d)
    assert dn_stage.shape[1:] == _weight_scratch(d, d)[1].shape[1:]
    return pl.pallas_call(
        kern,
        grid=(n_tiles + 1,),
        in_specs=[row_spec, _const_spec((b, d)), _const_spec((b, d)), *specs] + [any_spec] * 6,
        out_specs=[row_spec, _whole_spec((b, d))],
        out_shape=[jax.ShapeDtypeStruct((rows, d), F32), jax.ShapeDtypeStruct((b, d), F32)],
        scratch_shapes=[wg_scr, wg_scr, up_stage, up_sem, wd_scr, dn_stage, dn_sem, pltpu.VMEM((d, d), BF16),
                        pltpu.VMEM((b, d), F32), pltpu.VMEM((slots, 2, heads, hb, mem_len, dh), F32),
                        pltpu.SemaphoreType.DMA((slots, 2, heads))],
        compiler_params=_params("arbitrary"),
        name="ffn",
    )(xp2d, xs2d, qs2d, *ops, ck5, cv5, wg, wu, wd, wo)


def _pattn_kv_copies(k_hbm, v_hbm, kvbuf, sem, layer, n, slot):
    heads = kvbuf.shape[2]
    cps = []
    for j, src in enumerate((k_hbm, v_hbm)):
        for hd in range(heads):
            cps.append(pltpu.make_async_copy(src.at[layer, n, :, hd, :], kvbuf.at[slot, j, hd], sem.at[slot, j, hd]))
    return cps


def _attn_kernel(xp_ref, xs_ref, g_ref, pk_hbm, pv_hbm, wq_hbm, wo_hbm, op_ref, qs_ref,
                 kvbuf, kv16, psem, wq_ref, wo_ref, stage, wsem, *, layer, parts):
    n = pl.program_id(0)
    i = pl.program_id(1)
    heads, _, dh = kv16.shape[1:]
    tm = xp_ref.shape[0]
    rows = tm // parts
    pslot = lax.rem(n, 2)
    pcopies = functools.partial(_pattn_kv_copies, pk_hbm, pv_hbm, kvbuf, psem, layer)

    @pl.when(jnp.logical_and(n == 0, i == 0))
    def _():
        for cp in pcopies(0, 0):
            cp.start()
        _fetch_bf16((wq_hbm, layer, wq_ref, stage, wsem), (wo_hbm, layer, wo_ref, stage, wsem))
        h = _rms(xs_ref[...], g_ref[...]).astype(BF16)
        qs_ref[...] = _dot(h, wq_ref[...]) * (dh ** -0.5 * _LOG2E)

    @pl.when(i == 0)
    def _():
        @pl.when(n + 1 < pl.num_programs(0))
        def _():
            for cp in pcopies(n + 1, 1 - pslot):
                cp.start()

        for cp in pcopies(n, pslot):
            cp.wait()
        kv16[...] = kvbuf[pslot].astype(BF16)

    def attend(r0):
        x = xp_ref[r0:r0 + rows, :]
        h = _rms(x, g_ref[...]).astype(BF16)
        yield
        q = _dot(h, wq_ref[...]).astype(BF16)
        yield
        outs = []
        for hd in range(heads):
            s = lax.dot_general(q[:, hd * dh:(hd + 1) * dh], kv16[0, hd], (((1,), (1,)), ((), ())),
                                preferred_element_type=F32)
            yield
            a = _softmax2_rows(s * (dh ** -0.5 * _LOG2E)).astype(BF16)
            yield
            outs.append(_dot(a, kv16[1, hd]))
            yield
        o = jnp.concatenate(outs, axis=1).astype(BF16)
        op_ref[r0:r0 + rows, :] = x + _dot(o, wo_ref[...])
        yield

    _run_staggered([attend(p * rows) for p in range(parts)])


def _attn(xp, xs, g, wq, pk5, pv5, wo, layer, tm):
    nb, t, d = xp.shape
    b = xs.shape[0]
    _, _, mem_len, heads, dh = pk5.shape
    assert t % tm == 0 and tm % (_ATTN_PARTS * _SUBLANE) == 0
    x_spec = pl.BlockSpec((None, tm, d), lambda n, i: (n, i, 0))
    any_spec = pl.BlockSpec(memory_space=pl.ANY)
    g, g_spec = _sel(g, layer)
    wq_scr, stage, wsem = _weight_scratch(d, d)
    return pl.pallas_call(
        functools.partial(_attn_kernel, layer=layer, parts=_ATTN_PARTS),
        grid=(nb, t // tm),
        in_specs=[x_spec, _const_spec((b, d)), g_spec] + [any_spec] * 4,
        out_specs=[x_spec, _whole_spec((b, d))],
        out_shape=[jax.ShapeDtypeStruct((nb, t, d), F32), jax.ShapeDtypeStruct((b, d), F32)],
        scratch_shapes=[pltpu.VMEM((2, 2, heads, mem_len, dh), F32), pltpu.VMEM((2, heads, mem_len, dh), BF16),
                        pltpu.SemaphoreType.DMA((2, 2, heads)), wq_scr, wq_scr, stage, wsem],
        compiler_params=_params("arbitrary", "arbitrary"),
        name="attn",
    )(xp, xs, g, pk5, pv5, wq, wo)


def _peven_kernel(x_ref, g_ref, cw_ref, cb_ref, lg_ref, lb_ref, ps_ref, win_hbm, wout_hbm, pw_hbm,
                  xo_ref, sa_ref, sb_ref, za_ext, zb_ext, ca_buf,
                  win_ref, win_stage, win_sem, wout_ref, wout_stage, wout_sem, pw_ref, pw_stage, pw_sem,
                  *, idx, tm, d_a, ha, hb):
    t = pl.program_id(1)
    ka = cw_ref.shape[0]
    nca = za_ext.shape[0]
    ncb = zb_ext.shape[0]
    pb = sb_ref.shape[0]

    @pl.when(jnp.logical_and(pl.program_id(0) == 0, t == 0))
    def _():
        _fetch_bf16((win_hbm, idx, win_ref, win_stage, win_sem), (wout_hbm, idx, wout_ref, wout_stage, wout_sem),
                    (pw_hbm, idx, pw_ref, pw_stage, pw_sem))

    @pl.when(t == 0)
    def _():
        za_ext[:, 0:ha, :] = jnp.zeros((nca, ha, _LANE), F32)
        zb_ext[:, 0:hb, :] = jnp.zeros((ncb, hb, _LANE), F32)

    x = x_ref[...]
    h = _rms(x, g_ref[...]).astype(BF16)
    p = _dot(h, win_ref[...])
    za = p[:, :d_a] * jax.nn.sigmoid(p[:, d_a:2 * d_a])
    zb = p[:, 2 * d_a:]
    for c in range(nca):
        za_ext[c, ha:ha + tm, :] = za[:, c * _LANE:(c + 1) * _LANE]
    for c in range(ncb):
        zb_ext[c, hb:hb + tm, :] = zb[:, c * _LANE:(c + 1) * _LANE]

    base = ha - (ka - 1)
    for c in range(nca):
        ls = slice(c * _LANE, (c + 1) * _LANE)
        for r in range(tm // _CONV_ROWS):
            r0 = r * _CONV_ROWS
            acc = jnp.broadcast_to(cb_ref[:, ls], (_CONV_ROWS, _LANE))
            for k in range(ka):
                acc = acc + cw_ref[k:k + 1, ls] * za_ext[c, base + r0 + k:base + r0 + k + _CONV_ROWS, :]
            ca_buf[r0:r0 + _CONV_ROWS, ls] = acc
    ya = _silu(_layernorm(ca_buf[...], lg_ref[...], lb_ref[...]))

    pos = t * tm + lax.broadcasted_iota(jnp.int32, (tm, 1), 0)
    parts = []
    for gi, w in enumerate(_POOL_WINDOWS):
        zg = zb[:, gi * _LANE:(gi + 1) * _LANE]
        s = zg
        for j in range(1, w):
            s = s + zb_ext[gi, hb - j:hb - j + tm, :]
        inv_cnt = 1.0 / jnp.minimum(pos + 1, w).astype(F32)
        pooled = (s * inv_cnt - zg).astype(BF16)
        parts.append(_dot(pooled, pw_ref[gi * _LANE:(gi + 1) * _LANE, :]))
    yb = jnp.concatenate(parts, axis=1) * ps_ref[...]

    y = jnp.concatenate([ya, yb], axis=1).astype(BF16)
    xo_ref[...] = x + _dot(y, wout_ref[...])

    @pl.when(t == pl.num_programs(1) - 1)
    def _():
        for c in range(nca):
            sa_ref[:, c * _LANE:(c + 1) * _LANE] = za_ext[c, ha + tm - (ka - 1):ha + tm, :]
        for c in range(ncb):
            sb_ref[:, c * _LANE:(c + 1) * _LANE] = zb_ext[c, hb + tm - pb:hb + tm, :]

    za_ext[:, 0:ha, :] = za_ext[:, tm:tm + ha, :]
    zb_ext[:, 0:hb, :] = zb_ext[:, tm:tm + hb, :]


def _prompt_even(x, params, big, idx, tm):
    nb, t, d = x.shape
    ops, specs = zip(*params)
    win, wout, pw = big
    ka, d_a = ops[1].shape[-2:]
    d_b = ops[5].shape[-1]
    pb = max(_POOL_WINDOWS) - 1
    ha = _round_up(ka - 1, _SUBLANE)
    hb = _round_up(pb, _SUBLANE)
    assert t % tm == 0 and tm % _CONV_ROWS == 0 and tm >= max(ha, hb)
    assert d_a % _LANE == 0 and d_b == _LANE * len(_POOL_WINDOWS)
    x_spec = pl.BlockSpec((None, tm, d), lambda n, i: (n, i, 0))
    any_spec = pl.BlockSpec(memory_space=pl.ANY)
    kern = functools.partial(_peven_kernel, idx=idx, tm=tm, d_a=d_a, ha=ha, hb=hb)
    return pl.pallas_call(
        kern,
        grid=(nb, t // tm),
        in_specs=[x_spec, *specs, any_spec, any_spec, any_spec],
        out_specs=[x_spec,
                   pl.BlockSpec((None, ka - 1, d_a), lambda n, i: (n, 0, 0)),
                   pl.BlockSpec((None, pb, d_b), lambda n, i: (n, 0, 0))],
        out_shape=[jax.ShapeDtypeStruct((nb, t, d), F32),
                   jax.ShapeDtypeStruct((nb, ka - 1, d_a), F32),
                   jax.ShapeDtypeStruct((nb, pb, d_b), F32)],
        scratch_shapes=[pltpu.VMEM((d_a // _LANE, ha + tm, _LANE), F32),
                        pltpu.VMEM((d_b // _LANE, hb + tm, _LANE), F32),
                        pltpu.VMEM((tm, d_a), F32),
                        *_weight_scratch(*win.shape[1:]), *_weight_scratch(*wout.shape[1:]),
                        *_weight_scratch(*pw.shape[1:])],
        compiler_params=_params("arbitrary", "arbitrary"),
        name="prompt_even",
    )(x, *ops, win, wout, pw)


def _seven_kernel(x_ref, g_ref, cw_ref, cb_ref, lg_ref, lb_ref, ps_ref, ha_ref, hb_ref, win_hbm, wout_hbm, pw_hbm,
                  xo_ref, sa_ref, sb_ref,
                  win_ref, win_stage, win_sem, wout_ref, wout_stage, wout_sem, pw_ref, pw_stage, pw_sem, *, idx):
    _fetch_bf16((win_hbm, idx, win_ref, win_stage, win_sem), (wout_hbm, idx, wout_ref, wout_stage, wout_sem),
                (pw_hbm, idx, pw_ref, pw_stage, pw_sem))
    ka, d_a = cw_ref.shape
    pb = hb_ref.shape[0]
    d_b = hb_ref.shape[2]
    dg = d_b // len(_POOL_WINDOWS)
    x = x_ref[...]
    h = _rms(x, g_ref[...]).astype(BF16)
    p = _dot(h, win_ref[...])
    za = p[:, :d_a] * jax.nn.sigmoid(p[:, d_a:2 * d_a])
    zb = p[:, 2 * d_a:]

    acc = cb_ref[...] + cw_ref[ka - 1:ka, :] * za
    for k in range(ka - 1):
        acc = acc + cw_ref[k:k + 1, :] * ha_ref[k]
    ya = _silu(_layernorm(acc, lg_ref[...], lb_ref[...]))

    parts = []
    for gi, w in enumerate(_POOL_WINDOWS):
        ls = slice(gi * dg, (gi + 1) * dg)
        s = zb[:, ls]
        for j in range(1, w):
            s = s + hb_ref[pb - j, :, ls]
        inv_cnt = 1.0 / float(min(_PAST_LEN + 1, w))
        pooled = (s * inv_cnt - zb[:, ls]).astype(BF16)
        parts.append(_dot(pooled, pw_ref[gi * dg:(gi + 1) * dg, :]))
    yb = jnp.concatenate(parts, axis=1) * ps_ref[...]

    y = jnp.concatenate([ya, yb], axis=1).astype(BF16)
    xo_ref[...] = x + _dot(y, wout_ref[...])

    sa_ref[0:ka - 2] = ha_ref[1:ka - 1]
    sa_ref[ka - 2] = za
    sb_ref[0:pb - 1] = hb_ref[1:pb]
    sb_ref[pb - 1] = zb


def _sample_even(x2d, params, big, idx, hist_a, hist_b):
    b, d = x2d.shape
    ops, specs = zip(*params)
    win, wout, pw = big
    any_spec = pl.BlockSpec(memory_space=pl.ANY)
    return pl.pallas_call(
        functools.partial(_seven_kernel, idx=idx),
        grid=(1,),
        in_specs=[_const_spec((b, d)), *specs, _const_spec(hist_a.shape), _const_spec(hist_b.shape),
                  any_spec, any_spec, any_spec],
        out_specs=[_whole_spec((b, d)), _whole_spec(hist_a.shape), _whole_spec(hist_b.shape)],
        out_shape=[jax.ShapeDtypeStruct((b, d), F32),
                   jax.ShapeDtypeStruct(hist_a.shape, F32),
                   jax.ShapeDtypeStruct(hist_b.shape, F32)],
        scratch_shapes=[*_weight_scratch(*win.shape[1:]), *_weight_scratch(*wout.shape[1:]),
                        *_weight_scratch(*pw.shape[1:])],
        compiler_params=_params("arbitrary"),
        name="sample_even",
    )(x2d, *ops, hist_a, hist_b, win, wout, pw)


def _podd_kernel(x_ref, g_ref, lg_ref, lb_ref, ws_ref, bst_ref, cw_ref, win_hbm, wout_hbm,
                 xo_ref, sv_ref, sd_ref, g_ext, win_ref, win_stage, win_sem, wout_ref, wout_stage, wout_sem,
                 *, idx, tm, d_c, hd_rows):
    t = pl.program_id(1)
    kd, d_d = cw_ref.shape
    heads = ws_ref.shape[0]
    dh = d_c // heads
    ncd = g_ext.shape[0]

    @pl.when(jnp.logical_and(pl.program_id(0) == 0, t == 0))
    def _():
        _fetch_bf16((win_hbm, idx, win_ref, win_stage, win_sem), (wout_hbm, idx, wout_ref, wout_stage, wout_sem))

    @pl.when(t == 0)
    def _():
        g_ext[:, 0:hd_rows, :] = jnp.zeros((ncd, hd_rows, _LANE), F32)

    x = x_ref[...]
    h = _rms(x, g_ref[...]).astype(BF16)
    p = _dot(h, win_ref[...])
    u = p[:, :d_c]
    v = _layernorm(p[:, d_c:2 * d_c], lg_ref[...], lb_ref[...])
    o = 2 * d_c
    gb = p[:, o:o + d_d]
    gated = p[:, o + d_d:o + 2 * d_d] * p[:, o + 2 * d_d:]

    ri = lax.broadcasted_iota(jnp.int32, (_CHUNK, _CHUNK), 0)
    ci = lax.broadcasted_iota(jnp.int32, (_CHUNK, _CHUNK), 1)
    wm = [jnp.where(ci <= ri, ws_ref[hh], 0.0).astype(BF16) for hh in range(heads)]
    vb = v.astype(BF16)
    rows = []
    for c in range(tm // _CHUNK):
        rs = slice(c * _CHUNK, (c + 1) * _CHUNK)
        parts = [_dot(wm[hh], vb[rs, hh * dh:(hh + 1) * dh]) + bst_ref[:, hh:hh + 1] for hh in range(heads)]
        rows.append(jnp.concatenate(parts, axis=1))
    yc = u * jnp.concatenate(rows, axis=0)

    base = hd_rows - (kd - 1)
    cols = []
    for c in range(ncd):
        ls = slice(c * _LANE, (c + 1) * _LANE)
        g_ext[c, hd_rows:hd_rows + tm, :] = gated[:, ls]
        cd = cw_ref[0:1, ls] * g_ext[c, base:base + tm, :]
        for k in range(1, kd):
            cd = cd + cw_ref[k:k + 1, ls] * g_ext[c, base + k:base + k + tm, :]
        cols.append(cd)
    yd = gb * jnp.concatenate(cols, axis=1)

    y = jnp.concatenate([yc, yd], axis=1).astype(BF16)
    xo_ref[...] = x + _dot(y, wout_ref[...])

    @pl.when(t == pl.num_programs(1) - 1)
    def _():
        n_open = sv_ref.shape[0]
        sv_ref[...] = v[tm - n_open:, :]
        for c in range(ncd):
            sd_ref[:, c * _LANE:(c + 1) * _LANE] = g_ext[c, hd_rows + tm - (kd - 1):hd_rows + tm, :]

    g_ext[:, 0:hd_rows, :] = g_ext[:, tm:tm + hd_rows, :]


def _prompt_odd(x, params, big, idx, tm):
    nb, t, d = x.shape
    ops, specs = zip(*params)
    win, wout = big
    d_c = ops[1].shape[-1]
    kd, d_d = ops[5].shape[-2:]
    n_open = t - ((t - 1) // _CHUNK) * _CHUNK
    hd_rows = _round_up(kd - 1, _SUBLANE)
    assert t % tm == 0 and tm % _CHUNK == 0 and n_open % _SUBLANE == 0 and n_open <= tm
    x_spec = pl.BlockSpec((None, tm, d), lambda n, i: (n, i, 0))
    any_spec = pl.BlockSpec(memory_space=pl.ANY)
    kern = functools.partial(_podd_kernel, idx=idx, tm=tm, d_c=d_c, hd_rows=hd_rows)
    return pl.pallas_call(
        kern,
        grid=(nb, t // tm),
        in_specs=[x_spec, *specs, any_spec, any_spec],
        out_specs=[x_spec,
                   pl.BlockSpec((None, n_open, d_c), lambda n, i: (n, 0, 0)),
                   pl.BlockSpec((None, kd - 1, d_d), lambda n, i: (n, 0, 0))],
        out_shape=[jax.ShapeDtypeStruct((nb, t, d), F32),
                   jax.ShapeDtypeStruct((nb, n_open, d_c), F32),
                   jax.ShapeDtypeStruct((nb, kd - 1, d_d), F32)],
        scratch_shapes=[pltpu.VMEM((d_d // _LANE, hd_rows + tm, _LANE), F32),
                        *_weight_scratch(*win.shape[1:]), *_weight_scratch(*wout.shape[1:])],
        compiler_params=_params("arbitrary", "arbitrary"),
        name="prompt_odd",
    )(x, *ops, win, wout)


def _sodd_kernel(x_ref, g_ref, lg_ref, lb_ref, ws0_ref, bs0_ref, cw_ref, hd_ref, win_hbm, wout_hbm,
                 xo_ref, sv_ref, sd_ref, win_ref, win_stage, win_sem, wout_ref, wout_stage, wout_sem, *, idx):
    _fetch_bf16((win_hbm, idx, win_ref, win_stage, win_sem), (wout_hbm, idx, wout_ref, wout_stage, wout_sem))
    kd, d_d = cw_ref.shape
    d_c = lg_ref.shape[1]
    x = x_ref[...]
    h = _rms(x, g_ref[...]).astype(BF16)
    p = _dot(h, win_ref[...])
    u = p[:, :d_c]
    v = _layernorm(p[:, d_c:2 * d_c], lg_ref[...], lb_ref[...])
    o = 2 * d_c
    gb = p[:, o:o + d_d]
    gated = p[:, o + d_d:o + 2 * d_d] * p[:, o + 2 * d_d:]

    yc = u * (ws0_ref[...] * v + bs0_ref[...])

    cd = cw_ref[kd - 1:kd, :] * gated
    for k in range(kd - 1):
        cd = cd + cw_ref[k:k + 1, :] * hd_ref[:, k, :]
    yd = gb * cd

    y = jnp.concatenate([yc, yd], axis=1).astype(BF16)
    xo_ref[...] = x + _dot(y, wout_ref[...])

    sv_ref[...] = v
    for k in range(kd - 2):
        sd_ref[:, k, :] = hd_ref[:, k + 1, :]
    sd_ref[:, kd - 2, :] = gated


def _sample_odd(x2d, params, big, idx, ws0, bs0, hist_d):
    b, d = x2d.shape
    ops, specs = zip(*params)
    win, wout = big
    d_c = ops[1].shape[-1]
    ops = ops[:3] + (ws0, bs0) + ops[5:]
    specs = specs[:3] + (_const_spec(ws0.shape), _const_spec(bs0.shape)) + specs[5:]
    any_spec = pl.BlockSpec(memory_space=pl.ANY)
    return pl.pallas_call(
        functools.partial(_sodd_kernel, idx=idx),
        grid=(1,),
        in_specs=[_const_spec((b, d)), *specs, _const_spec(hist_d.shape), any_spec, any_spec],
        out_specs=[_whole_spec((b, d)), _whole_spec((b, d_c)), _whole_spec(hist_d.shape)],
        out_shape=[jax.ShapeDtypeStruct((b, d), F32),
                   jax.ShapeDtypeStruct((b, d_c), F32),
                   jax.ShapeDtypeStruct(hist_d.shape, F32)],
        scratch_shapes=[*_weight_scratch(*win.shape[1:]), *_weight_scratch(*wout.shape[1:])],
        compiler_params=_params("arbitrary"),
        name="sample_odd",
    )(x2d, *ops, hist_d, win, wout)


def kernel(x_prompt, x_sample, mem_prompt, state_convA, state_poolB, state_convD, cache_mem_k, cache_mem_v, norm_mix, norm_x, norm_ffn, norm_final, w_in_even, conv_a_w, conv_a_b, ln_a_g, ln_a_b, pool_b_w, pool_b_scale, w_out_even, w_in_odd, ln_c_g, ln_c_b, ws_c, bs_c, conv_d_w, w_out_odd, wq_x, wk_x, wv_x, wo_x, w_gate, w_up, w_down):
    nb, seq, d = x_prompt.shape
    db, dec_seq, _ = x_sample.shape
    assert dec_seq == 1
    depth = wq_x.shape[0]
    heads = cache_mem_k.shape[3]

    def unstack(parts):
        return parts[0][None] if len(parts) == 1 else jnp.stack(parts)

    g_fin = norm_final[None]
    pool_w2d = pool_b_w.reshape(pool_b_w.shape[0], -1, pool_b_w.shape[-1])

    p_k, p_v = _kv_proj(mem_prompt, wk_x, wv_x, heads)

    xp = x_prompt
    xs = x_sample.reshape(db, d)
    p_a, p_b, p_c, p_d = [], [], [], []
    s_a, s_b, s_c, s_d = [], [], [], []
    for layer in range(depth):
        i = layer // 2
        if layer % 2 == 0:
            params = [_sel(norm_mix, layer), _sel(conv_a_w, i), _sel(conv_a_b, i), _sel(ln_a_g, i), _sel(ln_a_b, i),
                      _sel(pool_b_scale, i)]
            big = (w_in_even, w_out_even, pool_w2d)
            xp, a, b = _prompt_even(xp, params, big, i, tm=_PROMPT_TILE)
            p_a.append(a)
            p_b.append(b)
            xs, a, b = _sample_even(xs, params, big, i, state_convA[i].transpose(1, 0, 2),
                                    state_poolB[i].transpose(1, 0, 2))
            s_a.append(a.transpose(1, 0, 2))
            s_b.append(b.transpose(1, 0, 2))
        else:
            params = [_sel(norm_mix, layer), _sel(ln_c_g, i), _sel(ln_c_b, i), _sel(ws_c, i),
                      _sel(bs_c.transpose(0, 2, 1), i), _sel(conv_d_w, i)]
            big = (w_in_odd, w_out_odd)
            xp, c, dd = _prompt_odd(xp, params, big, i, tm=_PROMPT_TILE)
            dc = c.shape[-1]
            p_c.append(c.reshape(nb, -1, _C_HEADS, dc // _C_HEADS))
            p_d.append(dd)
            ws0 = jnp.repeat(ws_c[i, :, 0, 0], dc // _C_HEADS)[None]
            bs0 = jnp.repeat(bs_c[i, :, 0], dc // _C_HEADS)[None]
            xs, c, dd = _sample_odd(xs, params, big, i, ws0, bs0, state_convD[i])
            s_c.append(c.reshape(db, 1, _C_HEADS, dc // _C_HEADS))
            s_d.append(dd)

        xp, qs = _attn(xp, xs, norm_x, wq_x, p_k, p_v, wo_x, layer, tm=_ATTN_TILE)

        xp2d, xs = _ffn(xp.reshape(nb * seq, d), xs, qs, cache_mem_k, cache_mem_v, wo_x, norm_ffn, w_gate, w_up,
                        w_down, g_fin, layer, tm=_FFN_TILE, final_norm=layer == depth - 1)
        xp = xp2d.reshape(nb, seq, d)

    return (xp, xs.reshape(db, 1, d), unstack(p_a), unstack(p_b), unstack(p_c), unstack(p_d), p_k, p_v,
            unstack(s_a), unstack(s_b), unstack(s_c), unstack(s_d))
```

```python
import functools

import jax
import jax.numpy as jnp
from jax import lax
from jax.experimental import pallas as pl
from jax.experimental.pallas import tpu as pltpu

F32 = jnp.float32
BF16 = jnp.bfloat16

_EPS = 1e-6
_X_HEADS = 4
_C_HEADS = 4
_CHUNK = 128
_POOL_WINDOWS = (2, 4, 8, 16)
_PAST_LEN = 16384

_SUBLANE = 8
_LANE = 128
_VMEM_LIMIT = 56 * 1024 * 1024
_STAGE_BYTES = 1024 * 1024
_STAGE_SLOTS = 4

_PROMPT_TILE = 1024
_ATTN_TILE = 2048
_ATTN_PARTS = 8
_FFN_TILE = 512
_KV_BATCHES = 4
_FF_CHUNKS = (768, 768, 768, 512)
_CONV_ROWS = 16


def _round_up(n, m):
    return -(-n // m) * m


def _params(*sem):
    return pltpu.CompilerParams(dimension_semantics=sem, vmem_limit_bytes=_VMEM_LIMIT)


def _const_spec(shape):
    nd = len(shape)
    return pl.BlockSpec(shape, lambda *_: (0,) * nd, pipeline_mode=pl.Buffered(1))


def _whole_spec(shape):
    nd = len(shape)
    return pl.BlockSpec(shape, lambda *_: (0,) * nd)


def _sel(stacked, idx):
    if stacked.ndim == 2:
        stacked = stacked[:, None, :]
    tail = stacked.shape[1:]
    spec = pl.BlockSpec((None,) + tail, lambda *_: (idx,) + (0,) * len(tail), pipeline_mode=pl.Buffered(1))
    return stacked, spec


def _chunk_rows(rows, cols):
    cands = [r for r in range(2 * _SUBLANE, rows + 1, 2 * _SUBLANE)
             if rows % r == 0 and r * cols * 4 <= _STAGE_BYTES]
    return max(cands)


def _weight_scratch(rows, cols):
    return [pltpu.VMEM((rows, cols), BF16), pltpu.VMEM((_STAGE_SLOTS, _chunk_rows(rows, cols), cols), F32),
            pltpu.SemaphoreType.DMA((_STAGE_SLOTS,))]


def _fetch_bf16(*jobs):
    rings = {}
    for w_hbm, idx, dst, stage, sem in jobs:
        slots, rc = stage.shape[:2]
        ring = rings.setdefault(id(stage), [])
        for r0 in range(0, dst.shape[0], rc):
            slot = len(ring) % slots
            copy = pltpu.make_async_copy(w_hbm.at[idx, pl.ds(r0, rc), :], stage.at[slot], sem.at[slot])
            ring.append((copy, dst, r0, rc, stage, slot))
    rings = list(rings.values())
    ahead = [ring[0][4].shape[0] - 1 for ring in rings]
    for ring, n in zip(rings, ahead):
        for copy, *_ in ring[:n]:
            copy.start()
    for k in range(max(len(ring) for ring in rings)):
        for ring, n in zip(rings, ahead):
            if k >= len(ring):
                continue
            if k + n < len(ring):
                ring[k + n][0].start()
            copy, dst, r0, rc, stage, slot = ring[k]
            copy.wait()
            dst[r0:r0 + rc, :] = stage[slot].astype(BF16)


def _dot(a, b):
    return jnp.dot(a, b, preferred_element_type=F32)


def _rms(x, g):
    y = x * lax.rsqrt(jnp.mean(x * x, axis=-1, keepdims=True) + _EPS)
    return y * g


def _layernorm(x, g, b):
    mu = jnp.mean(x, axis=-1, keepdims=True)
    d = x - mu
    var = jnp.mean(d * d, axis=-1, keepdims=True)
    return d * lax.rsqrt(var + _EPS) * g + b


def _silu(x):
    return x * jax.nn.sigmoid(x)


_LOG2E = 1.4426950408889634


def _softmax2_rows(s2):
    m = jnp.max(s2, axis=-1, keepdims=True)
    e = jnp.exp2(s2 - m)
    return e * (1.0 / jnp.sum(e, axis=-1, keepdims=True))


_DONE = object()


def _run_staggered(seqs):
    live = []
    pending = list(seqs)
    while live or pending:
        if pending:
            live.append(pending.pop(0))
        live = [s for s in live if next(s, _DONE) is not _DONE]


def _kv_out_copies(kbuf, vbuf, k_hbm, v_hbm, sem, layer, blk, slot):
    _, heads, nbk = kbuf.shape[:3]
    rows = pl.ds(blk * nbk, nbk)
    cps = []
    for hd in range(heads):
        cps.append(pltpu.make_async_copy(kbuf.at[slot, hd], k_hbm.at[layer, rows, :, hd, :], sem.at[slot, 0, hd]))
        cps.append(pltpu.make_async_copy(vbuf.at[slot, hd], v_hbm.at[layer, rows, :, hd, :], sem.at[slot, 1, hd]))
    return cps


def _kv_kernel(m_ref, wk_hbm, wv_hbm, k_hbm, v_hbm, kbuf, vbuf, sem, wk_ref, wv_ref, stage, wsem):
    nj = pl.num_programs(1)
    layer = pl.program_id(0)
    blk = pl.program_id(1)
    step = layer * nj + blk
    last = pl.num_programs(0) * nj - 1
    slot = lax.rem(step, 2)
    _, heads, nbk, mem, dh = kbuf.shape
    copies = functools.partial(_kv_out_copies, kbuf, vbuf, k_hbm, v_hbm, sem)

    @pl.when(blk == 0)
    def _():
        _fetch_bf16((wk_hbm, layer, wk_ref, stage, wsem), (wv_hbm, layer, wv_ref, stage, wsem))

    @pl.when(step >= 2)
    def _():
        prev = step - 2
        for cp in copies(lax.div(prev, nj), lax.rem(prev, nj), slot):
            cp.wait()

    m = m_ref[...].reshape(nbk * mem, m_ref.shape[-1]).astype(BF16)
    k = _dot(m, wk_ref[...])
    v = _dot(m, wv_ref[...])
    for hd in range(heads):
        for b in range(nbk):
            kbuf[slot, hd, b] = k[b * mem:(b + 1) * mem, hd * dh:(hd + 1) * dh]
            vbuf[slot, hd, b] = v[b * mem:(b + 1) * mem, hd * dh:(hd + 1) * dh]
    for cp in copies(layer, blk, slot):
        cp.start()

    @pl.when(step == last)
    def _():
        @pl.when(step >= 1)
        def _():
            prev = step - 1
            for cp in copies(lax.div(prev, nj), lax.rem(prev, nj), 1 - slot):
                cp.wait()
        for cp in copies(layer, blk, slot):
            cp.wait()


def _kv_proj(mem3, wk, wv, heads):
    nb, mem, d = mem3.shape
    depth = wk.shape[0]
    dh = d // heads
    nbk = _KV_BATCHES
    assert nb % nbk == 0 and mem % _SUBLANE == 0
    any_spec = pl.BlockSpec(memory_space=pl.ANY)
    buf = pltpu.VMEM((2, heads, nbk, mem, dh), F32)
    wk_scr, stage, wsem = _weight_scratch(d, d)
    return pl.pallas_call(
        _kv_kernel,
        grid=(depth, nb // nbk),
        in_specs=[pl.BlockSpec((nbk, mem, d), lambda l, j: (j, 0, 0)), any_spec, any_spec],
        out_specs=[any_spec, any_spec],
        out_shape=[jax.ShapeDtypeStruct((depth, nb, mem, heads, dh), F32)] * 2,
        scratch_shapes=[buf, buf, pltpu.SemaphoreType.DMA((2, 2, heads)), wk_scr, wk_scr, stage, wsem],
        compiler_params=_params("arbitrary", "arbitrary"),
        name="kv_proj",
    )(mem3, wk, wv)


def _sattn_copies(k_hbm, v_hbm, cbuf, sem, layer, row0, slot):
    _, _, heads, hb = cbuf.shape[:4]
    rows = pl.ds(row0, hb)
    cps = []
    for j, src in enumerate((k_hbm, v_hbm)):
        for hd in range(heads):
            cps.append(pltpu.make_async_copy(src.at[layer, rows, :, hd, :], cbuf.at[slot, j, hd], sem.at[slot, j, hd]))
    return cps


def _sample_attn_rows(qs_ref, o_scr, cbuf, slot, row0):
    _, _, heads, hb, _, dh = cbuf.shape
    for j in range(hb):
        outs = []
        for hd in range(heads):
            q_row = qs_ref[pl.ds(row0 + j, 1), hd * dh:(hd + 1) * dh]
            s = jnp.sum(cbuf[slot, 0, hd, j] * q_row, axis=1, keepdims=True)
            e = jnp.exp2(s - jnp.max(s, axis=0, keepdims=True))
            den = jnp.sum(e, axis=0, keepdims=True)
            num = jnp.sum(e * cbuf[slot, 1, hd, j], axis=0, keepdims=True)
            outs.append(num * (1.0 / den))
            yield
        o_scr[pl.ds(row0 + j, 1), :] = jnp.concatenate(outs, axis=1)


def _ffn_kernel(xp_ref, xs_ref, qs_ref, g_ref, gf_ref, ck_hbm, cv_hbm, wg_hbm, wu_hbm, wd_hbm, wo_hbm,
                op_ref, os_ref, wg_ref, wu_ref, up_stage, up_sem, wd_ref, dn_stage, dn_sem, wo_ref,
                o_scr, cbuf, csem, *, layer, chunks, final_norm):
    step = pl.program_id(0)
    n_tiles = pl.num_programs(0) - 1
    slots, hb = cbuf.shape[0], cbuf.shape[3]
    per_step = slots * hb
    ccopies = functools.partial(_sattn_copies, ck_hbm, cv_hbm, cbuf, csem, layer)

    @pl.when(step == 0)
    def _():
        for sl in range(slots):
            for cp in ccopies(sl * hb, sl):
                cp.start()
        _fetch_bf16((wg_hbm, layer, wg_ref, up_stage, up_sem), (wu_hbm, layer, wu_ref, up_stage, up_sem),
                    (wd_hbm, layer, wd_ref, dn_stage, dn_sem), (wo_hbm, layer, wo_ref, dn_stage, dn_sem))

    def swiglu(x, o_ref):
        h = _rms(x, g_ref[...]).astype(BF16)
        yield
        acc = x
        lo = 0
        for width in chunks:
            gate = _dot(h, wg_ref[:, lo:lo + width])
            up = _dot(h, wu_ref[:, lo:lo + width])
            yield
            act = (_silu(gate) * up).astype(BF16)
            yield
            acc = acc + _dot(act, wd_ref[lo:lo + width, :])
            yield
            lo += width
        if final_norm:
            acc = _rms(acc, gf_ref[...])
        o_ref[...] = acc

    def sample_rows():
        nxt = jnp.minimum(step + 1, n_tiles - 1) * per_step
        for sl in range(slots):
            row0 = step * per_step + sl * hb
            for cp in ccopies(row0, sl):
                cp.wait()
            yield from _sample_attn_rows(qs_ref, o_scr, cbuf, sl, row0)
            for cp in ccopies(nxt + sl * hb, sl):
                cp.start()
            yield

    @pl.when(step < n_tiles)
    def _():
        _run_staggered([swiglu(xp_ref[...], op_ref), sample_rows()])

    @pl.when(step == n_tiles)
    def _():
        for sl in range(slots):
            for cp in ccopies((n_tiles - 1) * per_step + sl * hb, sl):
                cp.wait()
        x = xs_ref[...] + _dot(o_scr[...].astype(BF16), wo_ref[...])
        _run_staggered([swiglu(x, os_ref)])


def _ffn(xp2d, xs2d, qs2d, ck5, cv5, wo, g, wg, wu, wd, gf, layer, tm, final_norm):
    rows, d = xp2d.shape
    b = xs2d.shape[0]
    ff = wg.shape[-1]
    _, _, mem_len, heads, dh = ck5.shape
    n_tiles = rows // tm
    slots = 2
    assert sum(_FF_CHUNKS) == ff and rows % tm == 0 and b % (n_tiles * slots) == 0
    hb = b // (n_tiles * slots)
    kern = functools.partial(_ffn_kernel, layer=layer, chunks=_FF_CHUNKS, final_norm=final_norm)
    row_spec = pl.BlockSpec((tm, d), lambda i: (jnp.minimum(i, n_tiles - 1), 0))
    any_spec = pl.BlockSpec(memory_space=pl.ANY)
    ops, specs = zip(_sel(g, layer), _sel(gf, 0))
    wg_scr, up_stage, up_sem = _weight_scratch(d, ff)
    wd_scr, dn_stage, dn_sem = _weight_scratch(ff, d)
    assert dn_stage.shape[1:] == _weight_scratch(d, d)[1].shape[1:]
    return pl.pallas_call(
        kern,
        grid=(n_tiles + 1,),
        in_specs=[row_spec, _const_spec((b, d)), _const_spec((b, d)), *specs] + [any_spec] * 6,
        out_specs=[row_spec, _whole_spec((b, d))],
        out_shape=[jax.ShapeDtypeStruct((rows, d), F32), jax.ShapeDtypeStruct((b, d), F32)],
        scratch_shapes=[wg_scr, wg_scr, up_stage, up_sem, wd_scr, dn_stage, dn_sem, pltpu.VMEM((d, d), BF16),
                        pltpu.VMEM((b, d), F32), pltpu.VMEM((slots, 2, heads, hb, mem_len, dh), F32),
                        pltpu.SemaphoreType.DMA((slots, 2, heads))],
        compiler_params=_params("arbitrary"),
        name="ffn",
    )(xp2d, xs2d, qs2d, *ops, ck5, cv5, wg, wu, wd, wo)


def _pattn_kv_copies(k_hbm, v_hbm, kvbuf, sem, layer, n, slot):
    heads = kvbuf.shape[2]
    cps = []
    for j, src in enumerate((k_hbm, v_hbm)):
        for hd in range(heads):
            cps.append(pltpu.make_async_copy(src.at[layer, n, :, hd, :], kvbuf.at[slot, j, hd], sem.at[slot, j, hd]))
    return cps


def _attn_kernel(xp_ref, xs_ref, g_ref, pk_hbm, pv_hbm, wq_hbm, wo_hbm, op_ref, qs_ref,
                 kvbuf, kv16, psem, wq_ref, wo_ref, stage, wsem, *, layer, parts):
    n = pl.program_id(0)
    i = pl.program_id(1)
    heads, _, dh = kv16.shape[1:]
    tm = xp_ref.shape[0]
    rows = tm // parts
    pslot = lax.rem(n, 2)
    pcopies = functools.partial(_pattn_kv_copies, pk_hbm, pv_hbm, kvbuf, psem, layer)

    @pl.when(jnp.logical_and(n == 0, i == 0))
    def _():
        for cp in pcopies(0, 0):
            cp.start()
        _fetch_bf16((wq_hbm, layer, wq_ref, stage, wsem), (wo_hbm, layer, wo_ref, stage, wsem))
        h = _rms(xs_ref[...], g_ref[...]).astype(BF16)
        qs_ref[...] = _dot(h, wq_ref[...]) * (dh ** -0.5 * _LOG2E)

    @pl.when(i == 0)
    def _():
        @pl.when(n + 1 < pl.num_programs(0))
        def _():
            for cp in pcopies(n + 1, 1 - pslot):
                cp.start()

        for cp in pcopies(n, pslot):
            cp.wait()
        kv16[...] = kvbuf[pslot].astype(BF16)

    def attend(r0):
        x = xp_ref[r0:r0 + rows, :]
        h = _rms(x, g_ref[...]).astype(BF16)
        yield
        q = _dot(h, wq_ref[...]).astype(BF16)
        yield
        outs = []
        for hd in range(heads):
            s = lax.dot_general(q[:, hd * dh:(hd + 1) * dh], kv16[0, hd], (((1,), (1,)), ((), ())),
                                preferred_element_type=F32)
            yield
            a = _softmax2_rows(s * (dh ** -0.5 * _LOG2E)).astype(BF16)
            yield
            outs.append(_dot(a, kv16[1, hd]))
            yield
        o = jnp.concatenate(outs, axis=1).astype(BF16)
        op_ref[r0:r0 + rows, :] = x + _dot(o, wo_ref[...])
        yield

    _run_staggered([attend(p * rows) for p in range(parts)])


def _attn(xp, xs, g, wq, pk5, pv5, wo, layer, tm):
    nb, t, d = xp.shape
    b = xs.shape[0]
    _, _, mem_len, heads, dh = pk5.shape
    assert t % tm == 0 and tm % (_ATTN_PARTS * _SUBLANE) == 0
    x_spec = pl.BlockSpec((None, tm, d), lambda n, i: (n, i, 0))
    any_spec = pl.BlockSpec(memory_space=pl.ANY)
    g, g_spec = _sel(g, layer)
    wq_scr, stage, wsem = _weight_scratch(d, d)
    return pl.pallas_call(
        functools.partial(_attn_kernel, layer=layer, parts=_ATTN_PARTS),
        grid=(nb, t // tm),
        in_specs=[x_spec, _const_spec((b, d)), g_spec] + [any_spec] * 4,
        out_specs=[x_spec, _whole_spec((b, d))],
        out_shape=[jax.ShapeDtypeStruct((nb, t, d), F32), jax.ShapeDtypeStruct((b, d), F32)],
        scratch_shapes=[pltpu.VMEM((2, 2, heads, mem_len, dh), F32), pltpu.VMEM((2, heads, mem_len, dh), BF16),
                        pltpu.SemaphoreType.DMA((2, 2, heads)), wq_scr, wq_scr, stage, wsem],
        compiler_params=_params("arbitrary", "arbitrary"),
        name="attn",
    )(xp, xs, g, pk5, pv5, wq, wo)


def _peven_kernel(x_ref, g_ref, cw_ref, cb_ref, lg_ref, lb_ref, ps_ref, win_hbm, wout_hbm, pw_hbm,
                  xo_ref, sa_ref, sb_ref, za_ext, zb_ext, ca_buf,
                  win_ref, win_stage, win_sem, wout_ref, wout_stage, wout_sem, pw_ref, pw_stage, pw_sem,
                  *, idx, tm, d_a, ha, hb):
    t = pl.program_id(1)
    ka = cw_ref.shape[0]
    nca = za_ext.shape[0]
    ncb = zb_ext.shape[0]
    pb = sb_ref.shape[0]

    @pl.when(jnp.logical_and(pl.program_id(0) == 0, t == 0))
    def _():
        _fetch_bf16((win_hbm, idx, win_ref, win_stage, win_sem), (wout_hbm, idx, wout_ref, wout_stage, wout_sem),
                    (pw_hbm, idx, pw_ref, pw_stage, pw_sem))

    @pl.when(t == 0)
    def _():
        za_ext[:, 0:ha, :] = jnp.zeros((nca, ha, _LANE), F32)
        zb_ext[:, 0:hb, :] = jnp.zeros((ncb, hb, _LANE), F32)

    x = x_ref[...]
    h = _rms(x, g_ref[...]).astype(BF16)
    p = _dot(h, win_ref[...])
    za = p[:, :d_a] * jax.nn.sigmoid(p[:, d_a:2 * d_a])
    zb = p[:, 2 * d_a:]
    for c in range(nca):
        za_ext[c, ha:ha + tm, :] = za[:, c * _LANE:(c + 1) * _LANE]
    for c in range(ncb):
        zb_ext[c, hb:hb + tm, :] = zb[:, c * _LANE:(c + 1) * _LANE]

    base = ha - (ka - 1)
    for c in range(nca):
        ls = slice(c * _LANE, (c + 1) * _LANE)
        for r in range(tm // _CONV_ROWS):
            r0 = r * _CONV_ROWS
            acc = jnp.broadcast_to(cb_ref[:, ls], (_CONV_ROWS, _LANE))
            for k in range(ka):
                acc = acc + cw_ref[k:k + 1, ls] * za_ext[c, base + r0 + k:base + r0 + k + _CONV_ROWS, :]
            ca_buf[r0:r0 + _CONV_ROWS, ls] = acc
    ya = _silu(_layernorm(ca_buf[...], lg_ref[...], lb_ref[...]))

    pos = t * tm + lax.broadcasted_iota(jnp.int32, (tm, 1), 0)
    parts = []
    for gi, w in enumerate(_POOL_WINDOWS):
        zg = zb[:, gi * _LANE:(gi + 1) * _LANE]
        s = zg
        for j in range(1, w):
            s = s + zb_ext[gi, hb - j:hb - j + tm, :]
        inv_cnt = 1.0 / jnp.minimum(pos + 1, w).astype(F32)
        pooled = (s * inv_cnt - zg).astype(BF16)
        parts.append(_dot(pooled, pw_ref[gi * _LANE:(gi + 1) * _LANE, :]))
    yb = jnp.concatenate(parts, axis=1) * ps_ref[...]

    y = jnp.concatenate([ya, yb], axis=1).astype(BF16)
    xo_ref[...] = x + _dot(y, wout_ref[...])

    @pl.when(t == pl.num_programs(1) - 1)
    def _():
        for c in range(nca):
            sa_ref[:, c * _LANE:(c + 1) * _LANE] = za_ext[c, ha + tm - (ka - 1):ha + tm, :]
        for c in range(ncb):
            sb_ref[:, c * _LANE:(c + 1) * _LANE] = zb_ext[c, hb + tm - pb:hb + tm, :]

    za_ext[:, 0:ha, :] = za_ext[:, tm:tm + ha, :]
    zb_ext[:, 0:hb, :] = zb_ext[:, tm:tm + hb, :]


def _prompt_even(x, params, big, idx, tm):
    nb, t, d = x.shape
    ops, specs = zip(*params)
    win, wout, pw = big
    ka, d_a = ops[1].shape[-2:]
    d_b = ops[5].shape[-1]
    pb = max(_POOL_WINDOWS) - 1
    ha = _round_up(ka - 1, _SUBLANE)
    hb = _round_up(pb, _SUBLANE)
    assert t % tm == 0 and tm % _CONV_ROWS == 0 and tm >= max(ha, hb)
    assert d_a % _LANE == 0 and d_b == _LANE * len(_POOL_WINDOWS)
    x_spec = pl.BlockSpec((None, tm, d), lambda n, i: (n, i, 0))
    any_spec = pl.BlockSpec(memory_space=pl.ANY)
    kern = functools.partial(_peven_kernel, idx=idx, tm=tm, d_a=d_a, ha=ha, hb=hb)
    return pl.pallas_call(
        kern,
        grid=(nb, t // tm),
        in_specs=[x_spec, *specs, any_spec, any_spec, any_spec],
        out_specs=[x_spec,
                   pl.BlockSpec((None, ka - 1, d_a), lambda n, i: (n, 0, 0)),
                   pl.BlockSpec((None, pb, d_b), lambda n, i: (n, 0, 0))],
        out_shape=[jax.ShapeDtypeStruct((nb, t, d), F32),
                   jax.ShapeDtypeStruct((nb, ka - 1, d_a), F32),
                   jax.ShapeDtypeStruct((nb, pb, d_b), F32)],
        scratch_shapes=[pltpu.VMEM((d_a // _LANE, ha + tm, _LANE), F32),
                        pltpu.VMEM((d_b // _LANE, hb + tm, _LANE), F32),
                        pltpu.VMEM((tm, d_a), F32),
                        *_weight_scratch(*win.shape[1:]), *_weight_scratch(*wout.shape[1:]),
                        *_weight_scratch(*pw.shape[1:])],
        compiler_params=_params("arbitrary", "arbitrary"),
        name="prompt_even",
    )(x, *ops, win, wout, pw)


def _seven_kernel(x_ref, g_ref, cw_ref, cb_ref, lg_ref, lb_ref, ps_ref, ha_ref, hb_ref, win_hbm, wout_hbm, pw_hbm,
                  xo_ref, sa_ref, sb_ref,
                  win_ref, win_stage, win_sem, wout_ref, wout_stage, wout_sem, pw_ref, pw_stage, pw_sem, *, idx):
    _fetch_bf16((win_hbm, idx, win_ref, win_stage, win_sem), (wout_hbm, idx, wout_ref, wout_stage, wout_sem),
                (pw_hbm, idx, pw_ref, pw_stage, pw_sem))
    ka, d_a = cw_ref.shape
    pb = hb_ref.shape[0]
    d_b = hb_ref.shape[2]
    dg = d_b // len(_POOL_WINDOWS)
    x = x_ref[...]
    h = _rms(x, g_ref[...]).astype(BF16)
    p = _dot(h, win_ref[...])
    za = p[:, :d_a] * jax.nn.sigmoid(p[:, d_a:2 * d_a])
    zb = p[:, 2 * d_a:]

    acc = cb_ref[...] + cw_ref[ka - 1:ka, :] * za
    for k in range(ka - 1):
        acc = acc + cw_ref[k:k + 1, :] * ha_ref[k]
    ya = _silu(_layernorm(acc, lg_ref[...], lb_ref[...]))

    parts = []
    for gi, w in enumerate(_POOL_WINDOWS):
        ls = slice(gi * dg, (gi + 1) * dg)
        s = zb[:, ls]
        for j in range(1, w):
            s = s + hb_ref[pb - j, :, ls]
        inv_cnt = 1.0 / float(min(_PAST_LEN + 1, w))
        pooled = (s * inv_cnt - zb[:, ls]).astype(BF16)
        parts.append(_dot(pooled, pw_ref[gi * dg:(gi + 1) * dg, :]))
    yb = jnp.concatenate(parts, axis=1) * ps_ref[...]

    y = jnp.concatenate([ya, yb], axis=1).astype(BF16)
    xo_ref[...] = x + _dot(y, wout_ref[...])

    sa_ref[0:ka - 2] = ha_ref[1:ka - 1]
    sa_ref[ka - 2] = za
    sb_ref[0:pb - 1] = hb_ref[1:pb]
    sb_ref[pb - 1] = zb


def _sample_even(x2d, params, big, idx, hist_a, hist_b):
    b, d = x2d.shape
    ops, specs = zip(*params)
    win, wout, pw = big
    any_spec = pl.BlockSpec(memory_space=pl.ANY)
    return pl.pallas_call(
        functools.partial(_seven_kernel, idx=idx),
        grid=(1,),
        in_specs=[_const_spec((b, d)), *specs, _const_spec(hist_a.shape), _const_spec(hist_b.shape),
                  any_spec, any_spec, any_spec],
        out_specs=[_whole_spec((b, d)), _whole_spec(hist_a.shape), _whole_spec(hist_b.shape)],
        out_shape=[jax.ShapeDtypeStruct((b, d), F32),
                   jax.ShapeDtypeStruct(hist_a.shape, F32),
                   jax.ShapeDtypeStruct(hist_b.shape, F32)],
        scratch_shapes=[*_weight_scratch(*win.shape[1:]), *_weight_scratch(*wout.shape[1:]),
                        *_weight_scratch(*pw.shape[1:])],
        compiler_params=_params("arbitrary"),
        name="sample_even",
    )(x2d, *ops, hist_a, hist_b, win, wout, pw)


def _podd_kernel(x_ref, g_ref, lg_ref, lb_ref, ws_ref, bst_ref, cw_ref, win_hbm, wout_hbm,
                 xo_ref, sv_ref, sd_ref, g_ext, win_ref, win_stage, win_sem, wout_ref, wout_stage, wout_sem,
                 *, idx, tm, d_c, hd_rows):
    t = pl.program_id(1)
    kd, d_d = cw_ref.shape
    heads = ws_ref.shape[0]
    dh = d_c // heads
    ncd = g_ext.shape[0]

    @pl.when(jnp.logical_and(pl.program_id(0) == 0, t == 0))
    def _():
        _fetch_bf16((win_hbm, idx, win_ref, win_stage, win_sem), (wout_hbm, idx, wout_ref, wout_stage, wout_sem))

    @pl.when(t == 0)
    def _():
        g_ext[:, 0:hd_rows, :] = jnp.zeros((ncd, hd_rows, _LANE), F32)

    x = x_ref[...]
    h = _rms(x, g_ref[...]).astype(BF16)
    p = _dot(h, win_ref[...])
    u = p[:, :d_c]
    v = _layernorm(p[:, d_c:2 * d_c], lg_ref[...], lb_ref[...])
    o = 2 * d_c
    gb = p[:, o:o + d_d]
    gated = p[:, o + d_d:o + 2 * d_d] * p[:, o + 2 * d_d:]

    ri = lax.broadcasted_iota(jnp.int32, (_CHUNK, _CHUNK), 0)
    ci = lax.broadcasted_iota(jnp.int32, (_CHUNK, _CHUNK), 1)
    wm = [jnp.where(ci <= ri, ws_ref[hh], 0.0).astype(BF16) for hh in range(heads)]
    vb = v.astype(BF16)
    rows = []
    for c in range(tm // _CHUNK):
        rs = slice(c * _CHUNK, (c + 1) * _CHUNK)
        parts = [_dot(wm[hh], vb[rs, hh * dh:(hh + 1) * dh]) + bst_ref[:, hh:hh + 1] for hh in range(heads)]
        rows.append(jnp.concatenate(parts, axis=1))
    yc = u * jnp.concatenate(rows, axis=0)

    base = hd_rows - (kd - 1)
    cols = []
    for c in range(ncd):
        ls = slice(c * _LANE, (c + 1) * _LANE)
        g_ext[c, hd_rows:hd_rows + tm, :] = gated[:, ls]
        cd = cw_ref[0:1, ls] * g_ext[c, base:base + tm, :]
        for k in range(1, kd):
            cd = cd + cw_ref[k:k + 1, ls] * g_ext[c, base + k:base + k + tm, :]
        cols.append(cd)
    yd = gb * jnp.concatenate(cols, axis=1)

    y = jnp.concatenate([yc, yd], axis=1).astype(BF16)
    xo_ref[...] = x + _dot(y, wout_ref[...])

    @pl.when(t == pl.num_programs(1) - 1)
    def _():
        n_open = sv_ref.shape[0]
        sv_ref[...] = v[tm - n_open:, :]
        for c in range(ncd):
            sd_ref[:, c * _LANE:(c + 1) * _LANE] = g_ext[c, hd_rows + tm - (kd - 1):hd_rows + tm, :]

    g_ext[:, 0:hd_rows, :] = g_ext[:, tm:tm + hd_rows, :]


def _prompt_odd(x, params, big, idx, tm):
    nb, t, d = x.shape
    ops, specs = zip(*params)
    win, wout = big
    d_c = ops[1].shape[-1]
    kd, d_d = ops[5].shape[-2:]
    n_open = t - ((t - 1) // _CHUNK) * _CHUNK
    hd_rows = _round_up(kd - 1, _SUBLANE)
    assert t % tm == 0 and tm % _CHUNK == 0 and n_open % _SUBLANE == 0 and n_open <= tm
    x_spec = pl.BlockSpec((None, tm, d), lambda n, i: (n, i, 0))
    any_spec = pl.BlockSpec(memory_space=pl.ANY)
    kern = functools.partial(_podd_kernel, idx=idx, tm=tm, d_c=d_c, hd_rows=hd_rows)
    return pl.pallas_call(
        kern,
        grid=(nb, t // tm),
        in_specs=[x_spec, *specs, any_spec, any_spec],
        out_specs=[x_spec,
                   pl.BlockSpec((None, n_open, d_c), lambda n, i: (n, 0, 0)),
                   pl.BlockSpec((None, kd - 1, d_d), lambda n, i: (n, 0, 0))],
        out_shape=[jax.ShapeDtypeStruct((nb, t, d), F32),
                   jax.ShapeDtypeStruct((nb, n_open, d_c), F32),
                   jax.ShapeDtypeStruct((nb, kd - 1, d_d), F32)],
        scratch_shapes=[pltpu.VMEM((d_d // _LANE, hd_rows + tm, _LANE), F32),
                        *_weight_scratch(*win.shape[1:]), *_weight_scratch(*wout.shape[1:])],
        compiler_params=_params("arbitrary", "arbitrary"),
        name="prompt_odd",
    )(x, *ops, win, wout)


def _sodd_kernel(x_ref, g_ref, lg_ref, lb_ref, ws0_ref, bs0_ref, cw_ref, hd_ref, win_hbm, wout_hbm,
                 xo_ref, sv_ref, sd_ref, win_ref, win_stage, win_sem, wout_ref, wout_stage, wout_sem, *, idx):
    _fetch_bf16((win_hbm, idx, win_ref, win_stage, win_sem), (wout_hbm, idx, wout_ref, wout_stage, wout_sem))
    kd, d_d = cw_ref.shape
    d_c = lg_ref.shape[1]
    x = x_ref[...]
    h = _rms(x, g_ref[...]).astype(BF16)
    p = _dot(h, win_ref[...])
    u = p[:, :d_c]
    v = _layernorm(p[:, d_c:2 * d_c], lg_ref[...], lb_ref[...])
    o = 2 * d_c
    gb = p[:, o:o + d_d]
    gated = p[:, o + d_d:o + 2 * d_d] * p[:, o + 2 * d_d:]

    yc = u * (ws0_ref[...] * v + bs0_ref[...])

    cd = cw_ref[kd - 1:kd, :] * gated
    for k in range(kd - 1):
        cd = cd + cw_ref[k:k + 1, :] * hd_ref[:, k, :]
    yd = gb * cd

    y = jnp.concatenate([yc, yd], axis=1).astype(BF16)
    xo_ref[...] = x + _dot(y, wout_ref[...])

    sv_ref[...] = v
    for k in range(kd - 2):
        sd_ref[:, k, :] = hd_ref[:, k + 1, :]
    sd_ref[:, kd - 2, :] = gated


def _sample_odd(x2d, params, big, idx, ws0, bs0, hist_d):
    b, d = x2d.shape
    ops, specs = zip(*params)
    win, wout = big
    d_c = ops[1].shape[-1]
    ops = ops[:3] + (ws0, bs0) + ops[5:]
    specs = specs[:3] + (_const_spec(ws0.shape), _const_spec(bs0.shape)) + specs[5:]
    any_spec = pl.BlockSpec(memory_space=pl.ANY)
    return pl.pallas_call(
        functools.partial(_sodd_kernel, idx=idx),
        grid=(1,),
        in_specs=[_const_spec((b, d)), *specs, _const_spec(hist_d.shape), any_spec, any_spec],
        out_specs=[_whole_spec((b, d)), _whole_spec((b, d_c)), _whole_spec(hist_d.shape)],
        out_shape=[jax.ShapeDtypeStruct((b, d), F32),
                   jax.ShapeDtypeStruct((b, d_c), F32),
                   jax.ShapeDtypeStruct(hist_d.shape, F32)],
        scratch_shapes=[*_weight_scratch(*win.shape[1:]), *_weight_scratch(*wout.shape[1:])],
        compiler_params=_params("arbitrary"),
        name="sample_odd",
    )(x2d, *ops, hist_d, win, wout)


def kernel(x_prompt, x_sample, mem_prompt, state_convA, state_poolB, state_convD, cache_mem_k, cache_mem_v, norm_mix, norm_x, norm_ffn, norm_final, w_in_even, conv_a_w, conv_a_b, ln_a_g, ln_a_b, pool_b_w, pool_b_scale, w_out_even, w_in_odd, ln_c_g, ln_c_b, ws_c, bs_c, conv_d_w, w_out_odd, wq_x, wk_x, wv_x, wo_x, w_gate, w_up, w_down):
    nb, seq, d = x_prompt.shape
    db, dec_seq, _ = x_sample.shape
    assert dec_seq == 1
    depth = wq_x.shape[0]
    heads = cache_mem_k.shape[3]

    def unstack(parts):
        return parts[0][None] if len(parts) == 1 else jnp.stack(parts)

    g_fin = norm_final[None]
    pool_w2d = pool_b_w.reshape(pool_b_w.shape[0], -1, pool_b_w.shape[-1])

    p_k, p_v = _kv_proj(mem_prompt, wk_x, wv_x, heads)

    xp = x_prompt
    xs = x_sample.reshape(db, d)
    p_a, p_b, p_c, p_d = [], [], [], []
    s_a, s_b, s_c, s_d = [], [], [], []
    for layer in range(depth):
        i = layer // 2
        if layer % 2 == 0:
            params = [_sel(norm_mix, layer), _sel(conv_a_w, i), _sel(conv_a_b, i), _sel(ln_a_g, i), _sel(ln_a_b, i),
                      _sel(pool_b_scale, i)]
            big = (w_in_even, w_out_even, pool_w2d)
            xp, a, b = _prompt_even(xp, params, big, i, tm=_PROMPT_TILE)
            p_a.append(a)
            p_b.append(b)
            xs, a, b = _sample_even(xs, params, big, i, state_convA[i].transpose(1, 0, 2),
                                    state_poolB[i].transpose(1, 0, 2))
            s_a.append(a.transpose(1, 0, 2))
            s_b.append(b.transpose(1, 0, 2))
        else:
            params = [_sel(norm_mix, layer), _sel(ln_c_g, i), _sel(ln_c_b, i), _sel(ws_c, i),
                      _sel(bs_c.transpose(0, 2, 1), i), _sel(conv_d_w, i)]
            big = (w_in_odd, w_out_odd)
            xp, c, dd = _prompt_odd(xp, params, big, i, tm=_PROMPT_TILE)
            dc = c.shape[-1]
            p_c.append(c.reshape(nb, -1, _C_HEADS, dc // _C_HEADS))
            p_d.append(dd)
            ws0 = jnp.repeat(ws_c[i, :, 0, 0], dc // _C_HEADS)[None]
            bs0 = jnp.repeat(bs_c[i, :, 0], dc // _C_HEADS)[None]
            xs, c, dd = _sample_odd(xs, params, big, i, ws0, bs0, state_convD[i])
            s_c.append(c.reshape(db, 1, _C_HEADS, dc // _C_HEADS))
            s_d.append(dd)

        xp, qs = _attn(xp, xs, norm_x, wq_x, p_k, p_v, wo_x, layer, tm=_ATTN_TILE)

        xp2d, xs = _ffn(xp.reshape(nb * seq, d), xs, qs, cache_mem_k, cache_mem_v, wo_x, norm_ffn, w_gate, w_up,
                        w_down, g_fin, layer, tm=_FFN_TILE, final_norm=layer == depth - 1)
        xp = xp2d.reshape(nb, seq, d)

    return (xp, xs.reshape(db, 1, d), unstack(p_a), unstack(p_b), unstack(p_c), unstack(p_d), p_k, p_v,
            unstack(s_a), unstack(s_b), unstack(s_c), unstack(s_d))
```
